```python
import jax, jax.numpy as jnp
from jax import lax
import numpy as np

D_MODEL = 1024
BATCH = 4
SEQ = 4096
DEPTH = 2

D_FF = 2816
RET_HEADS = 4
RET_QK_DIM = 128
RET_V_DIM = 128
RET_CHUNK = 128
MLA_HEADS = 8
MLA_NOPE = 64
MLA_ROPE = 32
MLA_V = 64
Q_LORA = 256
KV_LORA = 128
Q_BLOCK = 128
ROPE_THETA = 10000.0
EPS = 1e-6
RET_Q_W = RET_HEADS * RET_QK_DIM
RET_V_W = RET_HEADS * RET_V_DIM
IN_SPLITS = [RET_Q_W, RET_Q_W, RET_V_W, RET_V_W, Q_LORA, KV_LORA, MLA_ROPE]
IN_WIDTH = sum(IN_SPLITS)
IN_OFFSETS = list(np.cumsum(IN_SPLITS)[:-1])
MIX_WIDTH = RET_HEADS * RET_V_DIM + MLA_HEADS * MLA_V

kernel_name = "hybrid_retention_mla_macaron"


def rms_norm(x, g):
    xf = x.astype(jnp.float32)
    y = xf * lax.rsqrt(jnp.mean(xf * xf, axis=-1, keepdims=True) + EPS)
    return (y * g.astype(jnp.float32)).astype(x.dtype)


def rope_tables(positions, dim):
    inv = 1.0 / (ROPE_THETA ** (jnp.arange(0, dim, 2, dtype=jnp.float32) / dim))
    ang = positions.astype(jnp.float32)[..., None] * inv
    return jnp.cos(ang)[:, :, None, :], jnp.sin(ang)[:, :, None, :]


def apply_rope(x, cos, sin):
    xf = x.astype(jnp.float32)
    x1, x2 = jnp.split(xf, 2, axis=-1)
    return jnp.concatenate([x1 * cos - x2 * sin, x2 * cos + x1 * sin], axis=-1).astype(x.dtype)


def swiglu_ffn(h, w_gate, w_up, w_down):
    return (jax.nn.silu(h @ w_gate) * (h @ w_up)) @ w_down


def retention(q, k, v, g, gn, cos, sin):
    out_dtype = v.dtype
    B, S, H, dk = q.shape
    n = S // RET_CHUNK
    C = RET_CHUNK
    q = apply_rope(q, cos, sin).astype(jnp.float32)
    k = apply_rope(k, cos, sin).astype(jnp.float32) * (dk ** -0.5)
    v = v.astype(jnp.float32)

    def chunk(t):
        return t.reshape(B, n, C, H, -1).transpose(0, 3, 1, 2, 4)

    qc, kc, vc = chunk(q), chunk(k), chunk(v)
    lg = jnp.log(1.0 - 2.0 ** (-5.0 - jnp.arange(H, dtype=jnp.float32)))
    idx = jnp.arange(C, dtype=jnp.float32)
    rel = idx[:, None] - idx[None, :]
    decay = jnp.where(rel >= 0, jnp.exp(jnp.maximum(rel, 0.0)[None] * lg[:, None, None]), 0.0)
    scores = jnp.einsum('bhncd,bhnmd->bhncm', qc, kc) * decay[None, :, None]
    y_intra = jnp.einsum('bhncm,bhnme->bhnce', scores, vc)
    zeta = jnp.exp((C - 1 - idx)[None, :] * lg[:, None])
    chunk_kv = jnp.einsum('bhncd,bhnce->bhnde', kc * zeta[None, :, None, :, None], vc)
    chunk_decay = jnp.exp(C * lg)[None, :, None, None]

    def step(state, kv_n):
        return state * chunk_decay + kv_n, state

    init = jnp.zeros((B, H, dk, vc.shape[-1]), jnp.float32)
    _, prev = lax.scan(step, init, jnp.moveaxis(chunk_kv, 2, 0))
    prev = jnp.moveaxis(prev, 0, 2)
    xi = jnp.exp((idx + 1.0)[None, :] * lg[:, None])
    y_cross = jnp.einsum('bhncd,bhnde->bhnce', qc, prev) * xi[None, :, None, :, None]
    y = (y_intra + y_cross).transpose(0, 2, 3, 1, 4).reshape(B, S, H, -1)
    y = rms_norm(y, gn) * jax.nn.silu(g.astype(jnp.float32))
    return y.reshape(B, S, -1).astype(out_dtype)


def mla_attention(c_q, c_kv, k_rope, q_lat_norm, w_uq, kv_lat_norm, w_ukv,
                  qn_nope, qn_rope, kn_nope, kn_rope, cos, sin):
    B, S, _ = c_q.shape
    q = (rms_norm(c_q, q_lat_norm) @ w_uq).reshape(B, S, MLA_HEADS, MLA_NOPE + MLA_ROPE)
    kv = (rms_norm(c_kv, kv_lat_norm) @ w_ukv).reshape(B, S, MLA_HEADS, MLA_NOPE + MLA_V)
    q_nope, q_rope = q[..., :MLA_NOPE], q[..., MLA_NOPE:]
    k_nope, v = kv[..., :MLA_NOPE], kv[..., MLA_NOPE:]
    q_nope = rms_norm(q_nope, qn_nope)
    q_rope = apply_rope(rms_norm(q_rope, qn_rope), cos, sin)
    k_nope = rms_norm(k_nope, kn_nope)
    k_rope = apply_rope(rms_norm(k_rope[:, :, None, :], kn_rope), cos, sin)
    k_rope = jnp.broadcast_to(k_rope, (B, S, MLA_HEADS, MLA_ROPE))
    qf = jnp.concatenate([q_nope, q_rope], -1).transpose(0, 2, 1, 3).astype(jnp.float32)
    kf = jnp.concatenate([k_nope, k_rope], -1).transpose(0, 2, 1, 3).astype(jnp.float32)
    vf = v.transpose(0, 2, 1, 3).astype(jnp.float32)
    scale = (MLA_NOPE + MLA_ROPE) ** -0.5
    n_blk = S // Q_BLOCK
    q_blocks = qf.reshape(B, MLA_HEADS, n_blk, Q_BLOCK, -1).transpose(2, 0, 1, 3, 4)
    key_pos = jnp.arange(S)

    def block(args):
        qb, bi = args
        s = jnp.einsum('bhqd,bhkd->bhqk', qb, kf) * scale
        qpos = bi * Q_BLOCK + jnp.arange(Q_BLOCK)
        s = jnp.where(qpos[:, None] >= key_pos[None, :], s, -1e30)
        p = jax.nn.softmax(s, axis=-1)
        return jnp.einsum('bhqk,bhkd->bhqd', p, vf)

    o = lax.map(block, (q_blocks, jnp.arange(n_blk)))
    o = o.transpose(1, 0, 3, 2, 4).reshape(B, S, MLA_HEADS * MLA_V)
    return o.astype(c_q.dtype)


def setup_inputs(seed: int = 0) -> dict:
    key = jax.random.key(seed)
    ks = iter(jax.random.split(key, 32))
    L, D, F = DEPTH, D_MODEL, D_FF

    def w(shape, fan_in):
        return jax.random.normal(next(ks), shape, jnp.float32) * (fan_in ** -0.5)

    def gain(shape):
        return 1.0 + 0.02 * jax.random.normal(next(ks), shape, jnp.float32)

    x = jax.random.normal(next(ks), (BATCH, SEQ, D), jnp.float32)
    positions = jnp.broadcast_to(jnp.arange(SEQ, dtype=jnp.int32)[None], (BATCH, SEQ))
    return {
        "x": x,
        "positions": positions,
        "ffn1_norm": gain((L, D)),
        "ffn1_w_gate": w((L, D, F), D),
        "ffn1_w_up": w((L, D, F), D),
        "ffn1_w_down": w((L, F, D), F),
        "mix_norm": gain((L, D)),
        "w_in": w((L, D, IN_WIDTH), D),
        "ret_head_norm": gain((L, RET_HEADS, RET_V_DIM)),
        "q_lat_norm": gain((L, Q_LORA)),
        "w_uq": w((L, Q_LORA, MLA_HEADS * (MLA_NOPE + MLA_ROPE)), Q_LORA),
        "kv_lat_norm": gain((L, KV_LORA)),
        "w_ukv": w((L, KV_LORA, MLA_HEADS * (MLA_NOPE + MLA_V)), KV_LORA),
        "qn_nope": gain((L, MLA_NOPE)),
        "qn_rope": gain((L, MLA_ROPE)),
        "kn_nope": gain((L, MLA_NOPE)),
        "kn_rope": gain((L, MLA_ROPE)),
        "w_o": w((L, MIX_WIDTH, D), MIX_WIDTH),
        "ffn2_norm": gain((L, D)),
        "ffn2_w_gate": w((L, D, F), D),
        "ffn2_w_up": w((L, D, F), D),
        "ffn2_w_down": w((L, F, D), F),
    }


def reference(x, positions, ffn1_norm, ffn1_w_gate, ffn1_w_up, ffn1_w_down, mix_norm,
              w_in, ret_head_norm, q_lat_norm, w_uq, kv_lat_norm, w_ukv,
              qn_nope, qn_rope, kn_nope, kn_rope, w_o,
              ffn2_norm, ffn2_w_gate, ffn2_w_up, ffn2_w_down):
    B, S, _ = x.shape
    cos_r, sin_r = rope_tables(positions, RET_QK_DIM)
    cos_m, sin_m = rope_tables(positions, MLA_ROPE)
    for l in range(DEPTH):
        x = x + 0.5 * swiglu_ffn(rms_norm(x, ffn1_norm[l]), ffn1_w_gate[l], ffn1_w_up[l], ffn1_w_down[l])
        h = rms_norm(x, mix_norm[l])
        proj = h @ w_in[l]
        rq, rk, rv, rg, c_q, c_kv, k_rope = jnp.split(proj, IN_OFFSETS, axis=-1)
        y_ret = retention(rq.reshape(B, S, RET_HEADS, RET_QK_DIM), rk.reshape(B, S, RET_HEADS, RET_QK_DIM),
                          rv.reshape(B, S, RET_HEADS, RET_V_DIM), rg.reshape(B, S, RET_HEADS, RET_V_DIM),
                          ret_head_norm[l], cos_r, sin_r)
        y_mla = mla_attention(c_q, c_kv, k_rope, q_lat_norm[l], w_uq[l], kv_lat_norm[l], w_ukv[l],
                              qn_nope[l], qn_rope[l], kn_nope[l], kn_rope[l], cos_m, sin_m)
        x = x + jnp.concatenate([y_ret, y_mla], axis=-1) @ w_o[l]
        x = x + 0.5 * swiglu_ffn(rms_norm(x, ffn2_norm[l]), ffn2_w_gate[l], ffn2_w_up[l], ffn2_w_down[l])
    return x
```

```python
import functools

import jax
import jax.numpy as jnp
import numpy as np
from jax import lax
from jax.experimental import pallas as pl
from jax.experimental.pallas import tpu as pltpu

D_MODEL = 1024
D_FF = 2816
RET_HEADS = 4
RET_DIM = 128
RET_CHUNK = 128
MLA_HEADS = 8
MLA_NOPE = 64
MLA_ROPE = 32
MLA_V = 64
Q_LORA = 256
KV_LORA = 128
ROPE_THETA = 10000.0
EPS = 1e-6

LANES = 128
RET_W = RET_HEADS * RET_DIM
IN_W = 4 * RET_W + Q_LORA + KV_LORA + LANES
HALF_ROPE = MLA_ROPE // 2
SLAB_SHIFT = LANES - MLA_ROPE

TOK_TILE = 512
FF_TILE = 256
ATT_TQ = 512
ATT_TK = 512
RET_TILE = 1024
VMEM_LIMIT = 48 * 1024 * 1024

F32 = jnp.float32
BF16 = jnp.bfloat16


def _rms(x, gain):
    return x * lax.rsqrt(jnp.mean(x * x, axis=-1, keepdims=True) + EPS) * gain


def _resident(shape):
    nd = len(shape)
    return pl.BlockSpec(shape, lambda *_: (0,) * nd)


def _tables_body(pos_ref, c_ref, o_ref):
    p = pos_ref[...].astype(F32)
    ang_r = p * c_ref[0:1, :]
    o_ref[0] = jnp.cos(ang_r)
    o_ref[1] = jnp.sin(ang_r) * c_ref[1:2, :]
    ang_m = p * c_ref[2:3, :]
    o_ref[2] = c_ref[3:4, :] + c_ref[4:5, :] * jnp.cos(ang_m)
    o_ref[3] = jnp.sin(ang_m) * c_ref[5:6, :]


def _rope_tables(positions):
    t = positions.size
    pos = positions.reshape(t, 1)
    inv_r = 1.0 / (ROPE_THETA ** (jnp.arange(0, RET_DIM, 2, dtype=F32) / RET_DIM))
    inv_m = 1.0 / (ROPE_THETA ** (jnp.arange(0, MLA_ROPE, 2, dtype=F32) / MLA_ROPE))
    half = RET_DIM // 2
    z = lambda n: jnp.zeros((n,), F32)
    o = lambda n: jnp.ones((n,), F32)
    rows = jnp.stack([
        jnp.concatenate([inv_r, inv_r]),
        jnp.concatenate([-o(half), o(half)]),
        jnp.concatenate([z(MLA_NOPE), inv_m, inv_m, z(MLA_ROPE)]),
        jnp.concatenate([o(MLA_NOPE), z(LANES - MLA_NOPE)]),
        jnp.concatenate([z(MLA_NOPE), o(MLA_ROPE), z(MLA_ROPE)]),
        jnp.concatenate([z(MLA_NOPE), -o(HALF_ROPE), o(HALF_ROPE), z(MLA_ROPE)]),
        z(LANES), z(LANES)])
    tm = 1024
    return pl.pallas_call(
        _tables_body,
        grid=(t // tm,),
        in_specs=[pl.BlockSpec((tm, 1), lambda i: (i, 0)), _resident((8, LANES))],
        out_specs=pl.BlockSpec((4, tm, LANES), lambda i: (0, i, 0)),
        out_shape=jax.ShapeDtypeStruct((4, t, LANES), F32),
        compiler_params=pltpu.CompilerParams(dimension_semantics=("parallel",)),
        name="rope_tables",
    )(pos, rows)


def _ffn_body(*refs, with_proj):
    if with_proj:
        (x_ref, yr_ref, ym_ref, wo_ref, g_ref, wg_ref, wu_ref, wd_ref, o_ref, a_ref) = refs
        x = (x_ref[...]
             + jnp.dot(yr_ref[...], wo_ref[0:RET_W, :], preferred_element_type=F32)
             + jnp.dot(ym_ref[...], wo_ref[RET_W:, :], preferred_element_type=F32))
    else:
        (x_ref, g_ref, wg_ref, wu_ref, wd_ref, o_ref, a_ref) = refs
        x = x_ref[...]
    h = _rms(x, g_ref[...]).astype(BF16)
    for c in range(D_FF // FF_TILE):
        cols = slice(c * FF_TILE, (c + 1) * FF_TILE)
        g = jnp.dot(h, wg_ref[:, cols], preferred_element_type=F32)
        u = jnp.dot(h, wu_ref[:, cols], preferred_element_type=F32)
        a_ref[:, cols] = (g * jax.nn.sigmoid(g) * u).astype(BF16)
    y = jnp.dot(a_ref[...], wd_ref[...], preferred_element_type=F32)
    o_ref[...] = x + 0.5 * y


def _ffn(x, gain, wg, wu, wd, proj=None):
    t = x.shape[0]
    tm = TOK_TILE
    row = lambda w: pl.BlockSpec((tm, w), lambda i: (i, 0))
    ins, specs = [x], [row(D_MODEL)]
    if proj is not None:
        yr, ym, wo = proj
        ins += [yr, ym, wo]
        specs += [row(RET_W), row(MLA_HEADS * MLA_V), _resident(wo.shape)]
    ins += [gain, wg, wu, wd]
    specs += [_resident(gain.shape), _resident(wg.shape), _resident(wu.shape), _resident(wd.shape)]
    return pl.pallas_call(
        functools.partial(_ffn_body, with_proj=proj is not None),
        grid=(t // tm,),
        in_specs=specs,
        out_specs=row(D_MODEL),
        out_shape=jax.ShapeDtypeStruct((t, D_MODEL), F32),
        scratch_shapes=[pltpu.VMEM((tm, D_FF), BF16)],
        compiler_params=pltpu.CompilerParams(
            dimension_semantics=("parallel",), vmem_limit_bytes=VMEM_LIMIT),
        name="ffn_proj" if proj is not None else "ffn",
    )(*ins)


def _group_mean(x, gmat):
    x2 = x * x
    hi = x2.astype(BF16)
    lo = (x2 - hi.astype(F32)).astype(BF16)
    return (jnp.dot(hi, gmat, preferred_element_type=F32)
            + jnp.dot(lo, gmat, preferred_element_type=F32))


def _inproj_body(x_ref, gmix_ref, win_ref, tab_ref, gq_ref, wuq_ref, gkv_ref, wk_ref, wv_ref,
                 gmat_ref, vec_ref,
                 rq_ref, rk_ref, rv_ref, rg_ref, qc_ref, kc_ref, v_ref):
    h = _rms(x_ref[...], gmix_ref[...]).astype(BF16)
    proj = jnp.dot(h, win_ref[...], preferred_element_type=F32)
    cos_r, sin_r, cos_m, sin_m = tab_ref[0], tab_ref[1], tab_ref[2], tab_ref[3]
    ret_scale = RET_DIM ** -0.5
    for hd in range(RET_HEADS):
        sl = slice(hd * RET_DIM, (hd + 1) * RET_DIM)
        xq = proj[:, sl]
        rq_ref[:, sl] = (xq * cos_r + pltpu.roll(xq, RET_DIM // 2, 1) * sin_r).astype(BF16)
        xk = proj[:, RET_W + hd * RET_DIM:RET_W + (hd + 1) * RET_DIM]
        rk_ref[:, sl] = ((xk * cos_r + pltpu.roll(xk, RET_DIM // 2, 1) * sin_r)
                         * ret_scale).astype(BF16)
    rv_ref[...] = proj[:, 2 * RET_W:3 * RET_W].astype(BF16)
    rg_ref[...] = proj[:, 3 * RET_W:4 * RET_W]
    off = 4 * RET_W
    c_q = proj[:, off:off + Q_LORA]
    c_kv = proj[:, off + Q_LORA:off + Q_LORA + KV_LORA]
    kr = proj[:, off + Q_LORA + KV_LORA:]
    cqn = _rms(c_q, gq_ref[...]).astype(BF16)
    ckvn = _rms(c_kv, gkv_ref[...]).astype(BF16)
    qraw = jnp.dot(cqn, wuq_ref[...], preferred_element_type=F32)
    kraw = jnp.dot(ckvn, wk_ref[...], preferred_element_type=F32)
    v_ref[...] = jnp.dot(ckvn, wv_ref[...], preferred_element_type=F32).astype(BF16)

    lane = lax.broadcasted_iota(jnp.int32, (1, LANES), 1)
    in_rope = (lane >= MLA_NOPE) & (lane < MLA_NOPE + MLA_ROPE)
    ss = jnp.sum(jnp.where(in_rope, kr * kr, 0.0), axis=-1, keepdims=True) * (1.0 / MLA_ROPE)
    krn = kr * lax.rsqrt(ss + EPS) * vec_ref[2:3, 0:LANES]
    kro = krn * cos_m + pltpu.roll(krn, SLAB_SHIFT, 1) * sin_m

    gmat = gmat_ref[...]
    for pr in range(MLA_HEADS // 2):
        cols = slice(pr * 2 * LANES, (pr + 1) * 2 * LANES)
        xs = qraw[:, cols]
        y = xs * lax.rsqrt(_group_mean(xs, gmat) + EPS) * vec_ref[0:1, :]
        ks = kraw[:, cols]
        yk = ks * lax.rsqrt(_group_mean(ks, gmat) + EPS) * vec_ref[1:2, :]
        for e in range(2):
            sl = slice(e * LANES, (e + 1) * LANES)
            dst = slice((2 * pr + e) * LANES, (2 * pr + e + 1) * LANES)
            ys = y[:, sl]
            qc_ref[:, dst] = (ys * cos_m + pltpu.roll(ys, SLAB_SHIFT, 1) * sin_m).astype(BF16)
            kc_ref[:, dst] = (yk[:, sl] + kro).astype(BF16)


def _inproj(x, tabs, gmix, win, gq, wuq, gkv, wk, wv, gmat, vecs):
    t = x.shape[0]
    tm = TOK_TILE
    row = lambda w: pl.BlockSpec((tm, w), lambda i: (i, 0))
    slab_w = MLA_HEADS * LANES
    outs = [(RET_W, BF16), (RET_W, BF16), (RET_W, BF16), (RET_W, F32),
            (slab_w, BF16), (slab_w, BF16), (MLA_HEADS * MLA_V, BF16)]
    return pl.pallas_call(
        _inproj_body,
        grid=(t // tm,),
        in_specs=[row(D_MODEL), _resident(gmix.shape), _resident(win.shape),
                  pl.BlockSpec((4, tm, LANES), lambda i: (0, i, 0)),
                  _resident(gq.shape), _resident(wuq.shape), _resident(gkv.shape),
                  _resident(wk.shape), _resident(wv.shape), _resident(gmat.shape),
                  _resident(vecs.shape)],
        out_specs=[row(w) for w, _ in outs],
        out_shape=[jax.ShapeDtypeStruct((t, w), dt) for w, dt in outs],
        compiler_params=pltpu.CompilerParams(
            dimension_semantics=("parallel",), vmem_limit_bytes=VMEM_LIMIT),
        name="inproj",
    )(x, gmix, win, tabs, gq, wuq, gkv, wk, wv, gmat, vecs)


def _retention_body(q_ref, k_ref, v_ref, g_ref, dec_ref, gn_ref, o_ref, state_ref):
    @pl.when(pl.program_id(1) == 0)
    def _():
        state_ref[...] = jnp.zeros_like(state_ref)

    c = RET_CHUNK
    nt = (((1,), (1,)), ((), ()))
    for hd in range(RET_HEADS):
        cols = slice(hd * RET_DIM, (hd + 1) * RET_DIM)
        decay, zeta, xi, cdec = dec_ref[0, hd], dec_ref[1, hd], dec_ref[2, hd], dec_ref[3, hd]
        gn = gn_ref[hd]
        st = state_ref[hd]
        for ci in range(RET_TILE // c):
            rows = slice(ci * c, (ci + 1) * c)
            q = q_ref[0, rows, cols]
            k = k_ref[0, rows, cols]
            v = v_ref[0, rows, cols]
            s = lax.dot_general(q, k, nt, preferred_element_type=F32) * decay
            y = (jnp.dot(s.astype(BF16), v, preferred_element_type=F32)
                 + jnp.dot(q, st.astype(BF16), preferred_element_type=F32) * xi)
            kz = (k.astype(F32) * zeta).astype(BF16)
            st = st * cdec + jnp.dot(kz.T, v, preferred_element_type=F32)
            g = g_ref[0, rows, cols]
            o_ref[0, rows, cols] = (_rms(y, gn) * (g * jax.nn.sigmoid(g))).astype(BF16)
        state_ref[hd] = st


def _retention(rq, rk, rv, rg, dec, gn, batch, seq):
    shp = (batch, seq, RET_W)
    blk = pl.BlockSpec((1, RET_TILE, RET_W), lambda b, s: (b, s, 0))
    return pl.pallas_call(
        _retention_body,
        grid=(batch, seq // RET_TILE),
        in_specs=[blk, blk, blk, blk, _resident(dec.shape), _resident(gn.shape)],
        out_specs=blk,
        out_shape=jax.ShapeDtypeStruct(shp, BF16),
        scratch_shapes=[pltpu.VMEM((RET_HEADS, RET_DIM, RET_DIM), F32)],
        compiler_params=pltpu.CompilerParams(
            dimension_semantics=("parallel", "arbitrary"), vmem_limit_bytes=VMEM_LIMIT),
        name="retention",
    )(rq.reshape(shp), rk.reshape(shp), rv.reshape(shp), rg.reshape(shp), dec, gn)


def _attention_body(q_ref, k_ref, v_ref, o_ref, m_ref, l_ref, acc_ref):
    qi = pl.program_id(2)
    nt = (((1,), (1,)), ((), ()))
    outs = []
    for e in range(2):
        cols = slice(e * LANES, (e + 1) * LANES)
        q = q_ref[0, :, cols]
        m_ref[...] = jnp.full_like(m_ref, -1e30)
        l_ref[...] = jnp.zeros_like(l_ref)
        acc_ref[...] = jnp.zeros_like(acc_ref)

        def step(kb, masked):
            rows = pl.ds(pl.multiple_of(kb * ATT_TK, ATT_TK), ATT_TK)
            k = k_ref[0, rows, cols]
            v = v_ref[0, rows, :]
            s = lax.dot_general(q, k, nt, preferred_element_type=F32)
            if masked:
                r = lax.broadcasted_iota(jnp.int32, s.shape, 0)
                cidx = lax.broadcasted_iota(jnp.int32, s.shape, 1)
                s = jnp.where(r >= cidx, s, -1e30)
            m_old = m_ref[...]
            m_new = jnp.maximum(m_old, jnp.max(s, axis=-1, keepdims=True))
            alpha = jnp.exp(m_old - m_new)
            p = jnp.exp(s - m_new)
            l_ref[...] = alpha * l_ref[...] + jnp.sum(p, axis=-1, keepdims=True)
            acc_ref[...] = alpha * acc_ref[...] + jnp.dot(
                p.astype(BF16), v, preferred_element_type=F32)
            m_ref[...] = m_new

        def loop_body(kb, carry):
            step(kb, False)
            return carry

        lax.fori_loop(0, qi, loop_body, 0)
        step(qi, True)
        outs.append(acc_ref[...] * (1.0 / l_ref[...]))
    lane = lax.broadcasted_iota(jnp.int32, outs[0].shape, 1)
    o_ref[0] = jnp.where(lane < MLA_V, outs[0], outs[1]).astype(BF16)


def _attention(qc, kc, v, batch, seq):
    assert ATT_TQ == ATT_TK
    slab_w = MLA_HEADS * LANES
    pairs = MLA_HEADS // 2
    vw = MLA_HEADS * MLA_V
    return pl.pallas_call(
        _attention_body,
        grid=(batch, pairs, seq // ATT_TQ),
        in_specs=[pl.BlockSpec((1, ATT_TQ, 2 * LANES), lambda b, j, i: (b, i, j)),
                  pl.BlockSpec((1, seq, 2 * LANES), lambda b, j, i: (b, 0, j)),
                  pl.BlockSpec((1, seq, LANES), lambda b, j, i: (b, 0, j))],
        out_specs=pl.BlockSpec((1, ATT_TQ, LANES), lambda b, j, i: (b, i, j)),
        out_shape=jax.ShapeDtypeStruct((batch, seq, vw), BF16),
        scratch_shapes=[pltpu.VMEM((ATT_TQ, 1), F32), pltpu.VMEM((ATT_TQ, 1), F32),
                        pltpu.VMEM((ATT_TQ, LANES), F32)],
        compiler_params=pltpu.CompilerParams(
            dimension_semantics=("parallel", "parallel", "arbitrary"),
            vmem_limit_bytes=VMEM_LIMIT),
        name="latent_attention",
    )(qc.reshape(batch, seq, slab_w), kc.reshape(batch, seq, slab_w), v.reshape(batch, seq, vw))


def _retention_decay():
    c = RET_CHUNK
    lg = jnp.log(1.0 - 2.0 ** (-5.0 - jnp.arange(RET_HEADS, dtype=F32)))
    idx = jnp.arange(c, dtype=F32)
    rel = idx[:, None] - idx[None, :]
    decay = jnp.where(rel >= 0, jnp.exp(jnp.maximum(rel, 0.0)[None] * lg[:, None, None]), 0.0)
    zeta = jnp.exp((c - 1 - idx)[None, :] * lg[:, None])
    xi = jnp.exp((idx + 1.0)[None, :] * lg[:, None])
    cdec = jnp.exp(c * lg)
    full = (RET_HEADS, c, RET_DIM)
    return jnp.stack([decay,
                      jnp.broadcast_to(zeta[:, :, None], full),
                      jnp.broadcast_to(xi[:, :, None], full),
                      jnp.broadcast_to(cdec[:, None, None], full)])


def _swapped_rope_cols(w):
    a, b = w[..., :HALF_ROPE], w[..., HALF_ROPE:]
    return jnp.concatenate([a, b, b, a], axis=-1)


def _layer_params(l, p):
    row = lambda g: g[l].reshape(1, -1)
    w_in = p["w_in"][l]
    base = 4 * RET_W + Q_LORA + KV_LORA
    kr_cols = jnp.concatenate(
        [jnp.zeros((D_MODEL, MLA_NOPE), F32), _swapped_rope_cols(w_in[:, base:])], axis=-1)
    win = jnp.concatenate([w_in[:, :base], kr_cols], axis=-1).astype(BF16)

    dq = MLA_NOPE + MLA_ROPE
    wuq = p["w_uq"][l].reshape(Q_LORA, MLA_HEADS, dq)
    wuq = jnp.concatenate([wuq[..., :MLA_NOPE], _swapped_rope_cols(wuq[..., MLA_NOPE:])], axis=-1)
    wuq = wuq.reshape(Q_LORA, MLA_HEADS * LANES).astype(BF16)
    wukv = p["w_ukv"][l].reshape(KV_LORA, MLA_HEADS, MLA_NOPE + MLA_V)
    wk = jnp.concatenate([wukv[..., :MLA_NOPE], jnp.zeros_like(wukv[..., :MLA_NOPE])], axis=-1)
    wk = wk.reshape(KV_LORA, MLA_HEADS * LANES).astype(BF16)
    wv = wukv[..., MLA_NOPE:].reshape(KV_LORA, MLA_HEADS * MLA_V).astype(BF16)

    scale = dq ** -0.5
    zeros64 = jnp.zeros((MLA_NOPE,), F32)
    gq = jnp.concatenate([p["qn_nope"][l], _swapped_rope_cols(p["qn_rope"][l])]) * scale
    gk = jnp.concatenate([p["kn_nope"][l], zeros64])
    gkr = jnp.concatenate([zeros64, _swapped_rope_cols(p["kn_rope"][l])])
    vecs = jnp.zeros((8, 2 * LANES), F32)
    vecs = vecs.at[0].set(jnp.tile(gq, 2)).at[1].set(jnp.tile(gk, 2)).at[2].set(jnp.tile(gkr, 2))
    return dict(
        ffn1=(row(p["ffn1_norm"]), p["ffn1_w_gate"][l].astype(BF16),
              p["ffn1_w_up"][l].astype(BF16), p["ffn1_w_down"][l].astype(BF16)),
        ffn2=(row(p["ffn2_norm"]), p["ffn2_w_gate"][l].astype(BF16),
              p["ffn2_w_up"][l].astype(BF16), p["ffn2_w_down"][l].astype(BF16)),
        inproj=(row(p["mix_norm"]), win, row(p["q_lat_norm"]), wuq, row(p["kv_lat_norm"]), wk, wv),
        vecs=vecs,
        gn=p["ret_head_norm"][l].reshape(RET_HEADS, 1, RET_DIM),
        wo=p["w_o"][l].astype(BF16),
    )


def _group_matrix():
    g = np.zeros((LANES, LANES), np.float32)
    g[:MLA_NOPE, :MLA_NOPE] = 1.0 / MLA_NOPE
    g[MLA_NOPE:MLA_NOPE + MLA_ROPE, MLA_NOPE:] = 1.0 / MLA_ROPE
    z = np.zeros_like(g)
    return jnp.asarray(np.block([[g, z], [z, g]]), dtype=BF16)


def kernel(x, positions, ffn1_norm, ffn1_w_gate, ffn1_w_up, ffn1_w_down, mix_norm, w_in, ret_head_norm, q_lat_norm, w_uq, kv_lat_norm, w_ukv, qn_nope, qn_rope, kn_nope, kn_rope, w_o, ffn2_norm, ffn2_w_gate, ffn2_w_up, ffn2_w_down):
    p = dict(ffn1_norm=ffn1_norm, ffn1_w_gate=ffn1_w_gate, ffn1_w_up=ffn1_w_up,
             ffn1_w_down=ffn1_w_down, mix_norm=mix_norm, w_in=w_in,
             ret_head_norm=ret_head_norm, q_lat_norm=q_lat_norm, w_uq=w_uq,
             kv_lat_norm=kv_lat_norm, w_ukv=w_ukv, qn_nope=qn_nope, qn_rope=qn_rope,
             kn_nope=kn_nope, kn_rope=kn_rope, w_o=w_o, ffn2_norm=ffn2_norm,
             ffn2_w_gate=ffn2_w_gate, ffn2_w_up=ffn2_w_up, ffn2_w_down=ffn2_w_down)
    batch, seq, d = x.shape
    depth = w_in.shape[0]
    tabs = _rope_tables(positions)
    dec = _retention_decay()
    gmat = _group_matrix()
    xf = x.reshape(batch * seq, d)
    for l in range(depth):
        lp = _layer_params(l, p)
        xf = _ffn(xf, *lp["ffn1"])
        gmix, win, gq, wuq, gkv, wk, wv = lp["inproj"]
        rq, rk, rv, rg, qc, kc, v = _inproj(xf, tabs, gmix, win, gq, wuq, gkv, wk, wv,
                                            gmat, lp["vecs"])
        y_ret = _retention(rq, rk, rv, rg, dec, lp["gn"], batch, seq)
        y_mla = _attention(qc, kc, v, batch, seq)
        xf = _ffn(xf, *lp["ffn2"],
                  proj=(y_ret.reshape(batch * seq, -1), y_mla.reshape(batch * seq, -1), lp["wo"]))
    return xf.reshape(batch, seq, d)
```

```python
import functools

import jax
import jax.numpy as jnp
import numpy as np
from jax import lax
from jax.experimental import pallas as pl
from jax.experimental.pallas import tpu as pltpu

D_MODEL = 1024
D_FF = 2816
RET_HEADS = 4
RET_DIM = 128
RET_CHUNK = 128
MLA_HEADS = 8
MLA_NOPE = 64
MLA_ROPE = 32
MLA_V = 64
Q_LORA = 256
KV_LORA = 128
ROPE_THETA = 10000.0
EPS = 1e-6

LANES = 128
RET_W = RET_HEADS * RET_DIM
IN_W = 4 * RET_W + Q_LORA + KV_LORA + LANES
HALF_ROPE = MLA_ROPE // 2
SLAB_SHIFT = LANES - MLA_ROPE

TOK_TILE = 512
FF_TILE = 256
ATT_TQ = 512
ATT_TK = 512
RET_TILE = 1024
VMEM_LIMIT = 48 * 1024 * 1024

F32 = jnp.float32
BF16 = jnp.bfloat16


def _rms(x, gain):
    return x * lax.rsqrt(jnp.mean(x * x, axis=-1, keepdims=True) + EPS) * gain


def _resident(shape):
    nd = len(shape)
    return pl.BlockSpec(shape, lambda *_: (0,) * nd)


def _tables_body(pos_ref, c_ref, o_ref):
    p = pos_ref[...].astype(F32)
    ang_r = p * c_ref[0:1, :]
    o_ref[0] = jnp.cos(ang_r)
    o_ref[1] = jnp.sin(ang_r) * c_ref[1:2, :]
    ang_m = p * c_ref[2:3, :]
    o_ref[2] = c_ref[3:4, :] + c_ref[4:5, :] * jnp.cos(ang_m)
    o_ref[3] = jnp.sin(ang_m) * c_ref[5:6, :]


def _rope_tables(positions):
    t = positions.size
    pos = positions.reshape(t, 1)
    inv_r = 1.0 / (ROPE_THETA ** (jnp.arange(0, RET_DIM, 2, dtype=F32) / RET_DIM))
    inv_m = 1.0 / (ROPE_THETA ** (jnp.arange(0, MLA_ROPE, 2, dtype=F32) / MLA_ROPE))
    half = RET_DIM // 2
    z = lambda n: jnp.zeros((n,), F32)
    o = lambda n: jnp.ones((n,), F32)
    rows = jnp.stack([
        jnp.concatenate([inv_r, inv_r]),
        jnp.concatenate([-o(half), o(half)]),
        jnp.concatenate([z(MLA_NOPE), inv_m, inv_m, z(MLA_ROPE)]),
        jnp.concatenate([o(MLA_NOPE), z(LANES - MLA_NOPE)]),
        jnp.concatenate([z(MLA_NOPE), o(MLA_ROPE), z(MLA_ROPE)]),
        jnp.concatenate([z(MLA_NOPE), -o(HALF_ROPE), o(HALF_ROPE), z(MLA_ROPE)]),
        z(LANES), z(LANES)])
    tm = 1024
    return pl.pallas_call(
        _tables_body,
        grid=(t // tm,),
        in_specs=[pl.BlockSpec((tm, 1), lambda i: (i, 0)), _resident((8, LANES))],
        out_specs=pl.BlockSpec((4, tm, LANES), lambda i: (0, i, 0)),
        out_shape=jax.ShapeDtypeStruct((4, t, LANES), F32),
        compiler_params=pltpu.CompilerParams(dimension_semantics=("parallel",)),
        name="rope_tables",
    )(pos, rows)


def _ffn_body(*refs, with_proj):
    if with_proj:
        (x_ref, yr_ref, ym_ref, wo_ref, g_ref, wg_ref, wu_ref, wd_ref, o_ref, a_ref) = refs
        x = (x_ref[...]
             + jnp.dot(yr_ref[...], wo_ref[0:RET_W, :], preferred_element_type=F32)
             + jnp.dot(ym_ref[...], wo_ref[RET_W:, :], preferred_element_type=F32))
    else:
        (x_ref, g_ref, wg_ref, wu_ref, wd_ref, o_ref, a_ref) = refs
        x = x_ref[...]
    h = _rms(x, g_ref[...]).astype(BF16)
    for c in range(D_FF // FF_TILE):
        cols = slice(c * FF_TILE, (c + 1) * FF_TILE)
        g = jnp.dot(h, wg_ref[:, cols], preferred_element_type=F32)
        u = jnp.dot(h, wu_ref[:, cols], preferred_element_type=F32)
        a_ref[:, cols] = (g * jax.nn.sigmoid(g) * u).astype(BF16)
    y = jnp.dot(a_ref[...], wd_ref[...], preferred_element_type=F32)
    o_ref[...] = x + 0.5 * y


def _ffn(x, gain, wg, wu, wd, proj=None):
    t = x.shape[0]
    tm = TOK_TILE
    row = lambda w: pl.BlockSpec((tm, w), lambda i: (i, 0))
    ins, specs = [x], [row(D_MODEL)]
    if proj is not None:
        yr, ym, wo = proj
        ins += [yr, ym, wo]
        specs += [row(RET_W), row(MLA_HEADS * MLA_V), _resident(wo.shape)]
    ins += [gain, wg, wu, wd]
    specs += [_resident(gain.shape), _resident(wg.shape), _resident(wu.shape), _resident(wd.shape)]
    return pl.pallas_call(
        functools.partial(_ffn_body, with_proj=proj is not None),
        grid=(t // tm,),
        in_specs=specs,
        out_specs=row(D_MODEL),
        out_shape=jax.ShapeDtypeStruct((t, D_MODEL), F32),
        scratch_shapes=[pltpu.VMEM((tm, D_FF), BF16)],
        compiler_params=pltpu.CompilerParams(
            dimension_semantics=("parallel",), vmem_limit_bytes=VMEM_LIMIT),
        name="ffn_proj" if proj is not None else "ffn",
    )(*ins)


def _group_mean(x, gmat):
    x2 = x * x
    hi = x2.astype(BF16)
    lo = (x2 - hi.astype(F32)).astype(BF16)
    return (jnp.dot(hi, gmat, preferred_element_type=F32)
            + jnp.dot(lo, gmat, preferred_element_type=F32))


def _inproj_body(x_ref, gmix_ref, win_ref, tab_ref, gq_ref, wuq_ref, gkv_ref, wk_ref, wv_ref,
                 gmat_ref, vec_ref,
                 rq_ref, rk_ref, rv_ref, rg_ref, qc_ref, kc_ref, v_ref):
    h = _rms(x_ref[...], gmix_ref[...]).astype(BF16)
    proj = jnp.dot(h, win_ref[...], preferred_element_type=F32)
    cos_r, sin_r, cos_m, sin_m = tab_ref[0], tab_ref[1], tab_ref[2], tab_ref[3]
    ret_scale = RET_DIM ** -0.5
    for hd in range(RET_HEADS):
        sl = slice(hd * RET_DIM, (hd + 1) * RET_DIM)
        xq = proj[:, sl]
        rq_ref[:, sl] = (xq * cos_r + pltpu.roll(xq, RET_DIM // 2, 1) * sin_r).astype(BF16)
        xk = proj[:, RET_W + hd * RET_DIM:RET_W + (hd + 1) * RET_DIM]
        rk_ref[:, sl] = ((xk * cos_r + pltpu.roll(xk, RET_DIM // 2, 1) * sin_r)
                         * ret_scale).astype(BF16)
    rv_ref[...] = proj[:, 2 * RET_W:3 * RET_W].astype(BF16)
    rg_ref[...] = proj[:, 3 * RET_W:4 * RET_W]
    off = 4 * RET_W
    c_q = proj[:, off:off + Q_LORA]
    c_kv = proj[:, off + Q_LORA:off + Q_LORA + KV_LORA]
    kr = proj[:, off + Q_LORA + KV_LORA:]
    cqn = _rms(c_q, gq_ref[...]).astype(BF16)
    ckvn = _rms(c_kv, gkv_ref[...]).astype(BF16)
    qraw = jnp.dot(cqn, wuq_ref[...], preferred_element_type=F32)
    kraw = jnp.dot(ckvn, wk_ref[...], preferred_element_type=F32)
    v_ref[...] = (jnp.dot(ckvn, wv_ref[...], preferred_element_type=F32)
                  + vec_ref[3:4, :]).astype(BF16)

    lane = lax.broadcasted_iota(jnp.int32, (1, LANES), 1)
    in_rope = (lane >= MLA_NOPE) & (lane < MLA_NOPE + MLA_ROPE)
    ss = jnp.sum(jnp.where(in_rope, kr * kr, 0.0), axis=-1, keepdims=True) * (1.0 / MLA_ROPE)
    krn = kr * lax.rsqrt(ss + EPS) * vec_ref[2:3, 0:LANES]
    kro = krn * cos_m + pltpu.roll(krn, SLAB_SHIFT, 1) * sin_m

    gmat = gmat_ref[...]
    for pr in range(MLA_HEADS // 2):
        cols = slice(pr * 2 * LANES, (pr + 1) * 2 * LANES)
        xs = qraw[:, cols]
        y = xs * lax.rsqrt(_group_mean(xs, gmat) + EPS) * vec_ref[0:1, 0:2 * LANES]
        ks = kraw[:, cols]
        yk = ks * lax.rsqrt(_group_mean(ks, gmat) + EPS) * vec_ref[1:2, 0:2 * LANES]
        for e in range(2):
            sl = slice(e * LANES, (e + 1) * LANES)
            dst = slice((2 * pr + e) * LANES, (2 * pr + e + 1) * LANES)
            ys = y[:, sl]
            qc_ref[:, dst] = (ys * cos_m + pltpu.roll(ys, SLAB_SHIFT, 1) * sin_m).astype(BF16)
            kc_ref[:, dst] = (yk[:, sl] + kro).astype(BF16)


def _inproj(x, tabs, gmix, win, gq, wuq, gkv, wk, wv, gmat, vecs):
    t = x.shape[0]
    tm = TOK_TILE
    row = lambda w: pl.BlockSpec((tm, w), lambda i: (i, 0))
    slab_w = MLA_HEADS * LANES
    outs = [(RET_W, BF16), (RET_W, BF16), (RET_W, BF16), (RET_W, F32),
            (slab_w, BF16), (slab_w, BF16), (slab_w, BF16)]
    return pl.pallas_call(
        _inproj_body,
        grid=(t // tm,),
        in_specs=[row(D_MODEL), _resident(gmix.shape), _resident(win.shape),
                  pl.BlockSpec((4, tm, LANES), lambda i: (0, i, 0)),
                  _resident(gq.shape), _resident(wuq.shape), _resident(gkv.shape),
                  _resident(wk.shape), _resident(wv.shape), _resident(gmat.shape),
                  _resident(vecs.shape)],
        out_specs=[row(w) for w, _ in outs],
        out_shape=[jax.ShapeDtypeStruct((t, w), dt) for w, dt in outs],
        compiler_params=pltpu.CompilerParams(
            dimension_semantics=("parallel",), vmem_limit_bytes=VMEM_LIMIT),
        name="inproj",
    )(x, gmix, win, tabs, gq, wuq, gkv, wk, wv, gmat, vecs)


def _retention_body(q_ref, k_ref, v_ref, g_ref, dec_ref, gn_ref, o_ref, state_ref):
    @pl.when(pl.program_id(1) == 0)
    def _():
        state_ref[...] = jnp.zeros_like(state_ref)

    c = RET_CHUNK
    nt = (((1,), (1,)), ((), ()))
    for hd in range(RET_HEADS):
        cols = slice(hd * RET_DIM, (hd + 1) * RET_DIM)
        decay, zeta, xi, cdec = dec_ref[0, hd], dec_ref[1, hd], dec_ref[2, hd], dec_ref[3, hd]
        gn = gn_ref[hd]
        st = state_ref[hd]
        for ci in range(RET_TILE // c):
            rows = slice(ci * c, (ci + 1) * c)
            q = q_ref[0, rows, cols]
            k = k_ref[0, rows, cols]
            v = v_ref[0, rows, cols]
            s = lax.dot_general(q, k, nt, preferred_element_type=F32) * decay
            y = (jnp.dot(s.astype(BF16), v, preferred_element_type=F32)
                 + jnp.dot(q, st.astype(BF16), preferred_element_type=F32) * xi)
            kz = (k.astype(F32) * zeta).astype(BF16)
            st = st * cdec + jnp.dot(kz.T, v, preferred_element_type=F32)
            g = g_ref[0, rows, cols]
            o_ref[0, rows, cols] = (_rms(y, gn) * (g * jax.nn.sigmoid(g))).astype(BF16)
        state_ref[hd] = st


def _retention(rq, rk, rv, rg, dec, gn, batch, seq):
    shp = (batch, seq, RET_W)
    blk = pl.BlockSpec((1, RET_TILE, RET_W), lambda b, s: (b, s, 0))
    return pl.pallas_call(
        _retention_body,
        grid=(batch, seq // RET_TILE),
        in_specs=[blk, blk, blk, blk, _resident(dec.shape), _resident(gn.shape)],
        out_specs=blk,
        out_shape=jax.ShapeDtypeStruct(shp, BF16),
        scratch_shapes=[pltpu.VMEM((RET_HEADS, RET_DIM, RET_DIM), F32)],
        compiler_params=pltpu.CompilerParams(
            dimension_semantics=("parallel", "arbitrary"), vmem_limit_bytes=VMEM_LIMIT),
        name="retention",
    )(rq.reshape(shp), rk.reshape(shp), rv.reshape(shp), rg.reshape(shp), dec, gn)


def _attention_body(q_ref, k_ref, v_ref, o_ref, m_ref, acc_ref):
    qi = pl.program_id(2)
    nt = (((1,), (1,)), ((), ()))
    m_ref[...] = jnp.full_like(m_ref, -1e30)
    acc_ref[...] = jnp.zeros_like(acc_ref)

    def step(e, kb, masked):
        cols = slice(e * LANES, (e + 1) * LANES)
        rows = pl.ds(pl.multiple_of(kb * ATT_TK, ATT_TK), ATT_TK)
        s = lax.dot_general(q_ref[0, :, cols], k_ref[0, rows, cols], nt,
                            preferred_element_type=F32)
        if masked:
            r = lax.broadcasted_iota(jnp.int32, s.shape, 0)
            cidx = lax.broadcasted_iota(jnp.int32, s.shape, 1)
            s = jnp.where(r >= cidx, s, -1e30)
        chunks = [s[:, c * LANES:(c + 1) * LANES] for c in range(ATT_TK // LANES)]
        m_old = m_ref[e]
        m_new = jnp.maximum(
            m_old, jnp.max(functools.reduce(jnp.maximum, chunks), axis=-1, keepdims=True))
        alpha = jnp.exp(m_old - m_new)
        p = jnp.concatenate([jnp.exp(ch - m_new).astype(BF16) for ch in chunks], axis=1)
        acc_ref[e] = alpha * acc_ref[e] + jnp.dot(p, v_ref[0, rows, cols],
                                                  preferred_element_type=F32)
        m_ref[e] = m_new

    def loop_body(kb, carry):
        for e in range(2):
            step(e, kb, False)
        return carry

    lax.fori_loop(0, qi, loop_body, 0)
    outs = []
    for e in range(2):
        step(e, qi, True)
        acc = acc_ref[e]
        outs.append(acc * (1.0 / acc[:, MLA_V:MLA_V + 1]))
    lane = lax.broadcasted_iota(jnp.int32, outs[0].shape, 1)
    o_ref[0] = jnp.where(lane < MLA_V, outs[0], pltpu.roll(outs[1], MLA_V, 1)).astype(BF16)


def _attention(qc, kc, vc, batch, seq):
    assert ATT_TQ == ATT_TK
    slab_w = MLA_HEADS * LANES
    pairs = MLA_HEADS // 2
    seq_blk = pl.BlockSpec((1, seq, 2 * LANES), lambda b, j, i: (b, 0, j))
    return pl.pallas_call(
        _attention_body,
        grid=(batch, pairs, seq // ATT_TQ),
        in_specs=[pl.BlockSpec((1, ATT_TQ, 2 * LANES), lambda b, j, i: (b, i, j)),
                  seq_blk, seq_blk],
        out_specs=pl.BlockSpec((1, ATT_TQ, LANES), lambda b, j, i: (b, i, j)),
        out_shape=jax.ShapeDtypeStruct((batch, seq, MLA_HEADS * MLA_V), BF16),
        scratch_shapes=[pltpu.VMEM((2, ATT_TQ, LANES), F32),
                        pltpu.VMEM((2, ATT_TQ, LANES), F32)],
        compiler_params=pltpu.CompilerParams(
            dimension_semantics=("parallel", "parallel", "arbitrary"),
            vmem_limit_bytes=VMEM_LIMIT),
        name="latent_attention",
    )(qc.reshape(batch, seq, slab_w), kc.reshape(batch, seq, slab_w),
      vc.reshape(batch, seq, slab_w))


def _retention_decay():
    c = RET_CHUNK
    lg = jnp.log(1.0 - 2.0 ** (-5.0 - jnp.arange(RET_HEADS, dtype=F32)))
    idx = jnp.arange(c, dtype=F32)
    rel = idx[:, None] - idx[None, :]
    decay = jnp.where(rel >= 0, jnp.exp(jnp.maximum(rel, 0.0)[None] * lg[:, None, None]), 0.0)
    zeta = jnp.exp((c - 1 - idx)[None, :] * lg[:, None])
    xi = jnp.exp((idx + 1.0)[None, :] * lg[:, None])
    cdec = jnp.exp(c * lg)
    full = (RET_HEADS, c, RET_DIM)
    return jnp.stack([decay,
                      jnp.broadcast_to(zeta[:, :, None], full),
                      jnp.broadcast_to(xi[:, :, None], full),
                      jnp.broadcast_to(cdec[:, None, None], full)])


def _swapped_rope_cols(w):
    a, b = w[..., :HALF_ROPE], w[..., HALF_ROPE:]
    return jnp.concatenate([a, b, b, a], axis=-1)


def _layer_params(l, p):
    row = lambda g: g[l].reshape(1, -1)
    w_in = p["w_in"][l]
    base = 4 * RET_W + Q_LORA + KV_LORA
    kr_cols = jnp.concatenate(
        [jnp.zeros((D_MODEL, MLA_NOPE), F32), _swapped_rope_cols(w_in[:, base:])], axis=-1)
    win = jnp.concatenate([w_in[:, :base], kr_cols], axis=-1).astype(BF16)

    dq = MLA_NOPE + MLA_ROPE
    wuq = p["w_uq"][l].reshape(Q_LORA, MLA_HEADS, dq)
    wuq = jnp.concatenate([wuq[..., :MLA_NOPE], _swapped_rope_cols(wuq[..., MLA_NOPE:])], axis=-1)
    wuq = wuq.reshape(Q_LORA, MLA_HEADS * LANES).astype(BF16)
    wukv = p["w_ukv"][l].reshape(KV_LORA, MLA_HEADS, MLA_NOPE + MLA_V)
    wk = jnp.concatenate([wukv[..., :MLA_NOPE], jnp.zeros_like(wukv[..., :MLA_NOPE])], axis=-1)
    wk = wk.reshape(KV_LORA, MLA_HEADS * LANES).astype(BF16)
    wv = jnp.concatenate([wukv[..., MLA_NOPE:], jnp.zeros_like(wukv[..., :LANES - MLA_V])], axis=-1)
    wv = wv.reshape(KV_LORA, MLA_HEADS * LANES).astype(BF16)

    scale = dq ** -0.5
    zeros64 = jnp.zeros((MLA_NOPE,), F32)
    gq = jnp.concatenate([p["qn_nope"][l], _swapped_rope_cols(p["qn_rope"][l])]) * scale
    gk = jnp.concatenate([p["kn_nope"][l], zeros64])
    gkr = jnp.concatenate([zeros64, _swapped_rope_cols(p["kn_rope"][l])])
    one_lane = jnp.zeros((LANES,), F32).at[MLA_V].set(1.0)
    vecs = jnp.stack([jnp.tile(r, MLA_HEADS) for r in (gq, gk, gkr, one_lane)]
                     + [jnp.zeros((MLA_HEADS * LANES,), F32)] * 4)
    return dict(
        ffn1=(row(p["ffn1_norm"]), p["ffn1_w_gate"][l].astype(BF16),
              p["ffn1_w_up"][l].astype(BF16), p["ffn1_w_down"][l].astype(BF16)),
        ffn2=(row(p["ffn2_norm"]), p["ffn2_w_gate"][l].astype(BF16),
              p["ffn2_w_up"][l].astype(BF16), p["ffn2_w_down"][l].astype(BF16)),
        inproj=(row(p["mix_norm"]), win, row(p["q_lat_norm"]), wuq, row(p["kv_lat_norm"]), wk, wv),
        vecs=vecs,
        gn=p["ret_head_norm"][l].reshape(RET_HEADS, 1, RET_DIM),
        wo=p["w_o"][l].astype(BF16),
    )


def _group_matrix():
    g = np.zeros((LANES, LANES), np.float32)
    g[:MLA_NOPE, :MLA_NOPE] = 1.0 / MLA_NOPE
    g[MLA_NOPE:MLA_NOPE + MLA_ROPE, MLA_NOPE:] = 1.0 / MLA_ROPE
    z = np.zeros_like(g)
    return jnp.asarray(np.block([[g, z], [z, g]]), dtype=BF16)


def kernel(x, positions, ffn1_norm, ffn1_w_gate, ffn1_w_up, ffn1_w_down, mix_norm, w_in, ret_head_norm, q_lat_norm, w_uq, kv_lat_norm, w_ukv, qn_nope, qn_rope, kn_nope, kn_rope, w_o, ffn2_norm, ffn2_w_gate, ffn2_w_up, ffn2_w_down):
    p = dict(ffn1_norm=ffn1_norm, ffn1_w_gate=ffn1_w_gate, ffn1_w_up=ffn1_w_up,
             ffn1_w_down=ffn1_w_down, mix_norm=mix_norm, w_in=w_in,
             ret_head_norm=ret_head_norm, q_lat_norm=q_lat_norm, w_uq=w_uq,
             kv_lat_norm=kv_lat_norm, w_ukv=w_ukv, qn_nope=qn_nope, qn_rope=qn_rope,
             kn_nope=kn_nope, kn_rope=kn_rope, w_o=w_o, ffn2_norm=ffn2_norm,
             ffn2_w_gate=ffn2_w_gate, ffn2_w_up=ffn2_w_up, ffn2_w_down=ffn2_w_down)
    batch, seq, d = x.shape
    depth = w_in.shape[0]
    tabs = _rope_tables(positions)
    dec = _retention_decay()
    gmat = _group_matrix()
    xf = x.reshape(batch * seq, d)
    for l in range(depth):
        lp = _layer_params(l, p)
        xf = _ffn(xf, *lp["ffn1"])
        gmix, win, gq, wuq, gkv, wk, wv = lp["inproj"]
        rq, rk, rv, rg, qc, kc, v = _inproj(xf, tabs, gmix, win, gq, wuq, gkv, wk, wv,
                                            gmat, lp["vecs"])
        y_ret = _retention(rq, rk, rv, rg, dec, lp["gn"], batch, seq)
        y_mla = _attention(qc, kc, v, batch, seq)
        xf = _ffn(xf, *lp["ffn2"],
                  proj=(y_ret.reshape(batch * seq, -1), y_mla.reshape(batch * seq, -1), lp["wo"]))
    return xf.reshape(batch, seq, d)
```

```python
import functools

import jax
import jax.numpy as jnp
import numpy as np
from jax import lax
from jax.experimental import pallas as pl
from jax.experimental.pallas import tpu as pltpu

D_MODEL = 1024
D_FF = 2816
RET_HEADS = 4
RET_DIM = 128
RET_CHUNK = 128
MLA_HEADS = 8
MLA_NOPE = 64
MLA_ROPE = 32
MLA_V = 64
Q_LORA = 256
KV_LORA = 128
ROPE_THETA = 10000.0
EPS = 1e-6

LANES = 128
RET_W = RET_HEADS * RET_DIM
IN_W = 4 * RET_W + Q_LORA + KV_LORA + LANES
HALF_ROPE = MLA_ROPE // 2
SLAB_SHIFT = LANES - MLA_ROPE

TOK_TILE = 512
FF_TILE = 256
ATT_TQ = 1024
RET_TILE = 1024
VMEM_LIMIT = 48 * 1024 * 1024

F32 = jnp.float32
BF16 = jnp.bfloat16


def _rms(x, gain):
    return x * lax.rsqrt(jnp.mean(x * x, axis=-1, keepdims=True) + EPS) * gain


def _resident(shape):
    nd = len(shape)
    return pl.BlockSpec(shape, lambda *_: (0,) * nd)


def _tables_body(pos_ref, c_ref, o_ref):
    ang = pos_ref[...].astype(F32) * c_ref[0:1, :]
    c, s = jnp.cos(ang), jnp.sin(ang)
    low = lax.broadcasted_iota(jnp.int32, c.shape, 1) < RET_DIM // 2
    o_ref[0] = jnp.where(low, c, pltpu.roll(c, RET_DIM // 2, 1))
    o_ref[1] = jnp.where(low, s, pltpu.roll(s, RET_DIM // 2, 1)) * c_ref[1:2, :]
    o_ref[2] = c_ref[2:3, :] + c_ref[3:4, :] * c
    o_ref[3] = s * c_ref[4:5, :]


def _rope_tables(positions):
    t = positions.size
    pos = positions.reshape(t, 1)
    inv_r = 1.0 / (ROPE_THETA ** (jnp.arange(0, RET_DIM, 2, dtype=F32) / RET_DIM))
    inv_m = 1.0 / (ROPE_THETA ** (jnp.arange(0, MLA_ROPE, 2, dtype=F32) / MLA_ROPE))
    half = RET_DIM // 2
    z = lambda n: jnp.zeros((n,), F32)
    o = lambda n: jnp.ones((n,), F32)
    rows = jnp.stack([
        jnp.concatenate([inv_r, inv_m, inv_m, z(MLA_ROPE)]),
        jnp.concatenate([-o(half), o(half)]),
        jnp.concatenate([o(MLA_NOPE), z(LANES - MLA_NOPE)]),
        jnp.concatenate([z(MLA_NOPE), o(MLA_ROPE), z(MLA_ROPE)]),
        jnp.concatenate([z(MLA_NOPE), -o(HALF_ROPE), o(HALF_ROPE), z(MLA_ROPE)]),
        z(LANES), z(LANES), z(LANES)])
    tm = 1024
    return pl.pallas_call(
        _tables_body,
        grid=(t // tm,),
        in_specs=[pl.BlockSpec((tm, 1), lambda i: (i, 0)), _resident((8, LANES))],
        out_specs=pl.BlockSpec((4, tm, LANES), lambda i: (0, i, 0)),
        out_shape=jax.ShapeDtypeStruct((4, t, LANES), F32),
        compiler_params=pltpu.CompilerParams(dimension_semantics=("parallel",)),
        name="rope_tables",
    )(pos, rows)


def _ffn_body(*refs, with_proj):
    if with_proj:
        (x_ref, yr_ref, ym_ref, wo_ref, g_ref, wg_ref, wu_ref, wd_ref, o_ref, a_ref) = refs
        x = (x_ref[...]
             + jnp.dot(yr_ref[...], wo_ref[0:RET_W, :], preferred_element_type=F32)
             + jnp.dot(ym_ref[...], wo_ref[RET_W:, :], preferred_element_type=F32))
    else:
        (x_ref, g_ref, wg_ref, wu_ref, wd_ref, o_ref, a_ref) = refs
        x = x_ref[...]
    h = _rms(x, g_ref[...]).astype(BF16)
    for c in range(D_FF // FF_TILE):
        cols = slice(c * FF_TILE, (c + 1) * FF_TILE)
        g = jnp.dot(h, wg_ref[:, cols], preferred_element_type=F32)
        u = jnp.dot(h, wu_ref[:, cols], preferred_element_type=F32)
        a_ref[:, cols] = (g * jax.nn.sigmoid(g) * u).astype(BF16)
    y = jnp.dot(a_ref[...], wd_ref[...], preferred_element_type=F32)
    o_ref[...] = x + 0.5 * y


def _ffn(x, gain, wg, wu, wd, proj=None):
    t = x.shape[0]
    tm = TOK_TILE
    row = lambda w: pl.BlockSpec((tm, w), lambda i: (i, 0))
    ins, specs = [x], [row(D_MODEL)]
    if proj is not None:
        yr, ym, wo = proj
        ins += [yr, ym, wo]
        specs += [row(RET_W), row(MLA_HEADS * MLA_V), _resident(wo.shape)]
    ins += [gain, wg, wu, wd]
    specs += [_resident(gain.shape), _resident(wg.shape), _resident(wu.shape), _resident(wd.shape)]
    return pl.pallas_call(
        functools.partial(_ffn_body, with_proj=proj is not None),
        grid=(t // tm,),
        in_specs=specs,
        out_specs=row(D_MODEL),
        out_shape=jax.ShapeDtypeStruct((t, D_MODEL), F32),
        scratch_shapes=[pltpu.VMEM((tm, D_FF), BF16)],
        compiler_params=pltpu.CompilerParams(
            dimension_semantics=("parallel",), vmem_limit_bytes=VMEM_LIMIT),
        name="ffn_proj" if proj is not None else "ffn",
    )(*ins)


def _group_mean(x, gmat):
    x2 = x * x
    hi = x2.astype(BF16)
    lo = (x2 - hi.astype(F32)).astype(BF16)
    return (jnp.dot(hi, gmat, preferred_element_type=F32)
            + jnp.dot(lo, gmat, preferred_element_type=F32))


def _inproj_body(x_ref, gmix_ref, win_ref, tab_ref, gq_ref, wuq_ref, gkv_ref, wk_ref, wv_ref,
                 gmat_ref, vec_ref,
                 rq_ref, rk_ref, rv_ref, rg_ref, qc_ref, kc_ref, v_ref):
    h = _rms(x_ref[...], gmix_ref[...]).astype(BF16)
    proj = jnp.dot(h, win_ref[...], preferred_element_type=F32)
    cos_r, sin_r, cos_m, sin_m = tab_ref[0], tab_ref[1], tab_ref[2], tab_ref[3]
    ret_scale = RET_DIM ** -0.5
    for hd in range(RET_HEADS):
        sl = slice(hd * RET_DIM, (hd + 1) * RET_DIM)
        xq = proj[:, sl]
        rq_ref[:, sl] = (xq * cos_r + pltpu.roll(xq, RET_DIM // 2, 1) * sin_r).astype(BF16)
        xk = proj[:, RET_W + hd * RET_DIM:RET_W + (hd + 1) * RET_DIM]
        rk_ref[:, sl] = ((xk * cos_r + pltpu.roll(xk, RET_DIM // 2, 1) * sin_r)
                         * ret_scale).astype(BF16)
    rv_ref[...] = proj[:, 2 * RET_W:3 * RET_W].astype(BF16)
    rg_ref[...] = proj[:, 3 * RET_W:4 * RET_W]
    off = 4 * RET_W
    c_q = proj[:, off:off + Q_LORA]
    c_kv = proj[:, off + Q_LORA:off + Q_LORA + KV_LORA]
    kr = proj[:, off + Q_LORA + KV_LORA:]
    cqn = _rms(c_q, gq_ref[...]).astype(BF16)
    ckvn = _rms(c_kv, gkv_ref[...]).astype(BF16)
    qraw = jnp.dot(cqn, wuq_ref[...], preferred_element_type=F32)
    kraw = jnp.dot(ckvn, wk_ref[...], preferred_element_type=F32)
    v_ref[...] = (jnp.dot(ckvn, wv_ref[...], preferred_element_type=F32)
                  + vec_ref[3:4, :]).astype(BF16)

    lane = lax.broadcasted_iota(jnp.int32, (1, LANES), 1)
    in_rope = (lane >= MLA_NOPE) & (lane < MLA_NOPE + MLA_ROPE)
    ss = jnp.sum(jnp.where(in_rope, kr * kr, 0.0), axis=-1, keepdims=True) * (1.0 / MLA_ROPE)
    krn = kr * lax.rsqrt(ss + EPS) * vec_ref[2:3, 0:LANES]
    kro = krn * cos_m + pltpu.roll(krn, SLAB_SHIFT, 1) * sin_m

    gmat = gmat_ref[...]
    for pr in range(MLA_HEADS // 2):
        cols = slice(pr * 2 * LANES, (pr + 1) * 2 * LANES)
        xs = qraw[:, cols]
        y = xs * lax.rsqrt(_group_mean(xs, gmat) + EPS) * vec_ref[0:1, 0:2 * LANES]
        for e in range(2):
            sl = slice(e * LANES, (e + 1) * LANES)
            dst = slice((2 * pr + e) * LANES, (2 * pr + e + 1) * LANES)
            ys = y[:, sl]
            qc_ref[:, dst] = (ys * cos_m + pltpu.roll(ys, SLAB_SHIFT, 1) * sin_m).astype(BF16)
            ks = kraw[:, dst]
            ms = jnp.sum(ks * ks, axis=-1, keepdims=True) * (1.0 / MLA_NOPE)
            kc_ref[:, dst] = (ks * lax.rsqrt(ms + EPS) * vec_ref[1:2, 0:LANES] + kro).astype(BF16)


def _inproj(x, tabs, gmix, win, gq, wuq, gkv, wk, wv, gmat, vecs):
    t = x.shape[0]
    tm = TOK_TILE
    row = lambda w: pl.BlockSpec((tm, w), lambda i: (i, 0))
    slab_w = MLA_HEADS * LANES
    outs = [(RET_W, BF16), (RET_W, BF16), (RET_W, BF16), (RET_W, F32),
            (slab_w, BF16), (slab_w, BF16), (slab_w, BF16)]
    return pl.pallas_call(
        _inproj_body,
        grid=(t // tm,),
        in_specs=[row(D_MODEL), _resident(gmix.shape), _resident(win.shape),
                  pl.BlockSpec((4, tm, LANES), lambda i: (0, i, 0)),
                  _resident(gq.shape), _resident(wuq.shape), _resident(gkv.shape),
                  _resident(wk.shape), _resident(wv.shape), _resident(gmat.shape),
                  _resident(vecs.shape)],
        out_specs=[row(w) for w, _ in outs],
        out_shape=[jax.ShapeDtypeStruct((t, w), dt) for w, dt in outs],
        compiler_params=pltpu.CompilerParams(
            dimension_semantics=("parallel",), vmem_limit_bytes=VMEM_LIMIT),
        name="inproj",
    )(x, gmix, win, tabs, gq, wuq, gkv, wk, wv, gmat, vecs)


def _retention_body(q_ref, k_ref, v_ref, g_ref, dec_ref, gn_ref, o_ref, state_ref):
    @pl.when(pl.program_id(1) == 0)
    def _():
        state_ref[...] = jnp.zeros_like(state_ref)

    c = RET_CHUNK
    nt = (((1,), (1,)), ((), ()))
    states = [state_ref[hd] for hd in range(RET_HEADS)]
    for ci in range(RET_TILE // c):
        rows = slice(ci * c, (ci + 1) * c)
        for hd in range(RET_HEADS):
            cols = slice(hd * RET_DIM, (hd + 1) * RET_DIM)
            q = q_ref[0, rows, cols]
            k = k_ref[0, rows, cols]
            v = v_ref[0, rows, cols]
            st = states[hd]
            s = lax.dot_general(q, k, nt, preferred_element_type=F32) * dec_ref[0, hd]
            y = (jnp.dot(s.astype(BF16), v, preferred_element_type=F32)
                 + jnp.dot(q, st.astype(BF16), preferred_element_type=F32) * dec_ref[2, hd])
            kz = (k.astype(F32) * dec_ref[1, hd]).astype(BF16)
            states[hd] = st * dec_ref[3, hd] + jnp.dot(kz.T, v, preferred_element_type=F32)
            g = g_ref[0, rows, cols]
            o_ref[0, rows, cols] = (_rms(y, gn_ref[hd]) * (g * jax.nn.sigmoid(g))).astype(BF16)
    for hd in range(RET_HEADS):
        state_ref[hd] = states[hd]


def _retention(rq, rk, rv, rg, dec, gn, batch, seq):
    shp = (batch, seq, RET_W)
    blk = pl.BlockSpec((1, RET_TILE, RET_W), lambda b, s: (b, s, 0))
    return pl.pallas_call(
        _retention_body,
        grid=(batch, seq // RET_TILE),
        in_specs=[blk, blk, blk, blk, _resident(dec.shape), _resident(gn.shape)],
        out_specs=blk,
        out_shape=jax.ShapeDtypeStruct(shp, BF16),
        scratch_shapes=[pltpu.VMEM((RET_HEADS, RET_DIM, RET_DIM), F32)],
        compiler_params=pltpu.CompilerParams(
            dimension_semantics=("parallel", "arbitrary"), vmem_limit_bytes=VMEM_LIMIT),
        name="retention",
    )(rq.reshape(shp), rk.reshape(shp), rv.reshape(shp), rg.reshape(shp), dec, gn)


def _attention_body(q_ref, k_ref, v_ref, o_ref, m_ref, acc_ref):
    qi = pl.program_id(2)
    nt = (((1,), (1,)), ((), ()))
    half = ATT_TQ // 2
    m_ref[...] = jnp.full_like(m_ref, -1e30)
    acc_ref[...] = jnp.zeros_like(acc_ref)

    def step(e, q_rows, start, width, mask_from):
        cols = slice(e * LANES, (e + 1) * LANES)
        keys = pl.ds(pl.multiple_of(start, half), width)
        s = lax.dot_general(q_ref[0, q_rows, cols], k_ref[0, keys, cols], nt,
                            preferred_element_type=F32)
        chunks = [s[:, c * LANES:(c + 1) * LANES] for c in range(width // LANES)]
        if mask_from is not None:
            r = lax.broadcasted_iota(jnp.int32, chunks[0].shape, 0)
            lane = lax.broadcasted_iota(jnp.int32, chunks[0].shape, 1)
            for c in range(mask_from, len(chunks)):
                chunks[c] = jnp.where(r >= lane + (c - mask_from) * LANES, chunks[c], -1e30)
        m_old = m_ref[e, q_rows]
        m_new = jnp.maximum(
            m_old, jnp.max(functools.reduce(jnp.maximum, chunks), axis=-1, keepdims=True))
        alpha = jnp.exp(m_old - m_new)
        p = jnp.concatenate([jnp.exp(ch - m_new).astype(BF16) for ch in chunks], axis=1)
        acc_ref[e, q_rows] = alpha * acc_ref[e, q_rows] + jnp.dot(
            p, v_ref[0, keys, cols], preferred_element_type=F32)
        m_ref[e, q_rows] = m_new

    def loop_body(kb, carry):
        for e in range(2):
            step(e, slice(0, ATT_TQ), kb * ATT_TQ, ATT_TQ, None)
        return carry

    lax.fori_loop(0, qi, loop_body, 0)
    for e in range(2):
        step(e, slice(0, half), qi * ATT_TQ, half, 0)
        step(e, slice(half, ATT_TQ), qi * ATT_TQ, ATT_TQ, half // LANES)
    outs = []
    for e in range(2):
        acc = acc_ref[e]
        outs.append(acc * (1.0 / acc[:, MLA_V:MLA_V + 1]))
    lane = lax.broadcasted_iota(jnp.int32, outs[0].shape, 1)
    o_ref[0] = jnp.where(lane < MLA_V, outs[0], pltpu.roll(outs[1], MLA_V, 1)).astype(BF16)


def _attention(qc, kc, vc, batch, seq):
    slab_w = MLA_HEADS * LANES
    pairs = MLA_HEADS // 2
    seq_blk = pl.BlockSpec((1, seq, 2 * LANES), lambda b, j, i: (b, 0, j))
    return pl.pallas_call(
        _attention_body,
        grid=(batch, pairs, seq // ATT_TQ),
        in_specs=[pl.BlockSpec((1, ATT_TQ, 2 * LANES), lambda b, j, i: (b, i, j)),
                  seq_blk, seq_blk],
        out_specs=pl.BlockSpec((1, ATT_TQ, LANES), lambda b, j, i: (b, i, j)),
        out_shape=jax.ShapeDtypeStruct((batch, seq, MLA_HEADS * MLA_V), BF16),
        scratch_shapes=[pltpu.VMEM((2, ATT_TQ, LANES), F32),
                        pltpu.VMEM((2, ATT_TQ, LANES), F32)],
        compiler_params=pltpu.CompilerParams(
            dimension_semantics=("parallel", "parallel", "arbitrary"),
            vmem_limit_bytes=VMEM_LIMIT),
        name="latent_attention",
    )(qc.reshape(batch, seq, slab_w), kc.reshape(batch, seq, slab_w),
      vc.reshape(batch, seq, slab_w))


def _retention_decay():
    c = RET_CHUNK
    lg = jnp.log(1.0 - 2.0 ** (-5.0 - jnp.arange(RET_HEADS, dtype=F32)))
    idx = jnp.arange(c, dtype=F32)
    rel = idx[:, None] - idx[None, :]
    decay = jnp.where(rel >= 0, jnp.exp(jnp.maximum(rel, 0.0)[None] * lg[:, None, None]), 0.0)
    zeta = jnp.exp((c - 1 - idx)[None, :] * lg[:, None])
    xi = jnp.exp((idx + 1.0)[None, :] * lg[:, None])
    cdec = jnp.exp(c * lg)
    full = (RET_HEADS, c, RET_DIM)
    return jnp.stack([decay,
                      jnp.broadcast_to(zeta[:, :, None], full),
                      jnp.broadcast_to(xi[:, :, None], full),
                      jnp.broadcast_to(cdec[:, None, None], full)])


def _swapped_rope_cols(w):
    a, b = w[..., :HALF_ROPE], w[..., HALF_ROPE:]
    return jnp.concatenate([a, b, b, a], axis=-1)


def _layer_params(l, p):
    row = lambda g: g[l].reshape(1, -1)
    w_in = p["w_in"][l]
    base = 4 * RET_W + Q_LORA + KV_LORA
    kr_cols = jnp.concatenate(
        [jnp.zeros((D_MODEL, MLA_NOPE), F32), _swapped_rope_cols(w_in[:, base:])], axis=-1)
    win = jnp.concatenate([w_in[:, :base], kr_cols], axis=-1).astype(BF16)

    dq = MLA_NOPE + MLA_ROPE
    wuq = p["w_uq"][l].reshape(Q_LORA, MLA_HEADS, dq)
    wuq = jnp.concatenate([wuq[..., :MLA_NOPE], _swapped_rope_cols(wuq[..., MLA_NOPE:])], axis=-1)
    wuq = wuq.reshape(Q_LORA, MLA_HEADS * LANES).astype(BF16)
    wukv = p["w_ukv"][l].reshape(KV_LORA, MLA_HEADS, MLA_NOPE + MLA_V)
    wk = jnp.concatenate([wukv[..., :MLA_NOPE], jnp.zeros_like(wukv[..., :MLA_NOPE])], axis=-1)
    wk = wk.reshape(KV_LORA, MLA_HEADS * LANES).astype(BF16)
    wv = jnp.concatenate([wukv[..., MLA_NOPE:], jnp.zeros_like(wukv[..., :LANES - MLA_V])], axis=-1)
    wv = wv.reshape(KV_LORA, MLA_HEADS * LANES).astype(BF16)

    scale = dq ** -0.5
    zeros64 = jnp.zeros((MLA_NOPE,), F32)
    gq = jnp.concatenate([p["qn_nope"][l], _swapped_rope_cols(p["qn_rope"][l])]) * scale
    gk = jnp.concatenate([p["kn_nope"][l], zeros64])
    gkr = jnp.concatenate([zeros64, _swapped_rope_cols(p["kn_rope"][l])])
    one_lane = jnp.zeros((LANES,), F32).at[MLA_V].set(1.0)
    vecs = jnp.stack([jnp.tile(r, MLA_HEADS) for r in (gq, gk, gkr, one_lane)]
                     + [jnp.zeros((MLA_HEADS * LANES,), F32)] * 4)
    return dict(
        ffn1=(row(p["ffn1_norm"]), p["ffn1_w_gate"][l].astype(BF16),
              p["ffn1_w_up"][l].astype(BF16), p["ffn1_w_down"][l].astype(BF16)),
        ffn2=(row(p["ffn2_norm"]), p["ffn2_w_gate"][l].astype(BF16),
              p["ffn2_w_up"][l].astype(BF16), p["ffn2_w_down"][l].astype(BF16)),
        inproj=(row(p["mix_norm"]), win, row(p["q_lat_norm"]), wuq, row(p["kv_lat_norm"]), wk, wv),
        vecs=vecs,
        gn=p["ret_head_norm"][l].reshape(RET_HEADS, 1, RET_DIM),
        wo=p["w_o"][l].astype(BF16),
    )


def _group_matrix():
    g = np.zeros((LANES, LANES), np.float32)
    g[:MLA_NOPE, :MLA_NOPE] = 1.0 / MLA_NOPE
    g[MLA_NOPE:MLA_NOPE + MLA_ROPE, MLA_NOPE:] = 1.0 / MLA_ROPE
    z = np.zeros_like(g)
    return jnp.asarray(np.block([[g, z], [z, g]]), dtype=BF16)


def kernel(x, positions, ffn1_norm, ffn1_w_gate, ffn1_w_up, ffn1_w_down, mix_norm, w_in, ret_head_norm, q_lat_norm, w_uq, kv_lat_norm, w_ukv, qn_nope, qn_rope, kn_nope, kn_rope, w_o, ffn2_norm, ffn2_w_gate, ffn2_w_up, ffn2_w_down):
    p = dict(ffn1_norm=ffn1_norm, ffn1_w_gate=ffn1_w_gate, ffn1_w_up=ffn1_w_up,
             ffn1_w_down=ffn1_w_down, mix_norm=mix_norm, w_in=w_in,
             ret_head_norm=ret_head_norm, q_lat_norm=q_lat_norm, w_uq=w_uq,
             kv_lat_norm=kv_lat_norm, w_ukv=w_ukv, qn_nope=qn_nope, qn_rope=qn_rope,
             kn_nope=kn_nope, kn_rope=kn_rope, w_o=w_o, ffn2_norm=ffn2_norm,
             ffn2_w_gate=ffn2_w_gate, ffn2_w_up=ffn2_w_up, ffn2_w_down=ffn2_w_down)
    batch, seq, d = x.shape
    depth = w_in.shape[0]
    tabs = _rope_tables(positions)
    dec = _retention_decay()
    gmat = _group_matrix()
    xf = x.reshape(batch * seq, d)
    for l in range(depth):
        lp = _layer_params(l, p)
        xf = _ffn(xf, *lp["ffn1"])
        gmix, win, gq, wuq, gkv, wk, wv = lp["inproj"]
        rq, rk, rv, rg, qc, kc, v = _inproj(xf, tabs, gmix, win, gq, wuq, gkv, wk, wv,
                                            gmat, lp["vecs"])
        y_ret = _retention(rq, rk, rv, rg, dec, lp["gn"], batch, seq)
        y_mla = _attention(qc, kc, v, batch, seq)
        xf = _ffn(xf, *lp["ffn2"],
                  proj=(y_ret.reshape(batch * seq, -1), y_mla.reshape(batch * seq, -1), lp["wo"]))
    return xf.reshape(batch, seq, d)
```

```python
import functools

import jax
import jax.numpy as jnp
import numpy as np
from jax import lax
from jax.experimental import pallas as pl
from jax.experimental.pallas import tpu as pltpu

D_MODEL = 1024
D_FF = 2816
RET_HEADS = 4
RET_DIM = 128
RET_CHUNK = 128
MLA_HEADS = 8
MLA_NOPE = 64
MLA_ROPE = 32
MLA_V = 64
Q_LORA = 256
KV_LORA = 128
ROPE_THETA = 10000.0
EPS = 1e-6

LANES = 128
RET_W = RET_HEADS * RET_DIM
IN_W = 4 * RET_W + Q_LORA + KV_LORA + LANES
HALF_ROPE = MLA_ROPE // 2
SLAB_SHIFT = LANES - MLA_ROPE

TOK_TILE = 512
FF_TILE = 256
CAST_ROWS = 256
ATT_TQ = 1024
RET_TILE = 1024
VMEM_LIMIT = 48 * 1024 * 1024

F32 = jnp.float32
BF16 = jnp.bfloat16


def _rms(x, gain):
    return x * lax.rsqrt(jnp.mean(x * x, axis=-1, keepdims=True) + EPS) * gain


def _resident(shape):
    nd = len(shape)
    return pl.BlockSpec(shape, lambda *_: (0,) * nd)


def _tables_body(pos_ref, c_ref, o_ref):
    ang = pos_ref[...].astype(F32) * c_ref[0:1, :]
    c, s = jnp.cos(ang), jnp.sin(ang)
    low = lax.broadcasted_iota(jnp.int32, c.shape, 1) < RET_DIM // 2
    o_ref[0] = jnp.where(low, c, pltpu.roll(c, RET_DIM // 2, 1))
    o_ref[1] = jnp.where(low, s, pltpu.roll(s, RET_DIM // 2, 1)) * c_ref[1:2, :]
    o_ref[2] = c_ref[2:3, :] + c_ref[3:4, :] * c
    o_ref[3] = s * c_ref[4:5, :]


def _rope_tables(positions):
    t = positions.size
    pos = positions.reshape(t, 1)
    inv_r = 1.0 / (ROPE_THETA ** (jnp.arange(0, RET_DIM, 2, dtype=F32) / RET_DIM))
    inv_m = 1.0 / (ROPE_THETA ** (jnp.arange(0, MLA_ROPE, 2, dtype=F32) / MLA_ROPE))
    half = RET_DIM // 2
    z = lambda n: jnp.zeros((n,), F32)
    o = lambda n: jnp.ones((n,), F32)
    rows = jnp.stack([
        jnp.concatenate([inv_r, inv_m, inv_m, z(MLA_ROPE)]),
        jnp.concatenate([-o(half), o(half)]),
        jnp.concatenate([o(MLA_NOPE), z(LANES - MLA_NOPE)]),
        jnp.concatenate([z(MLA_NOPE), o(MLA_ROPE), z(MLA_ROPE)]),
        jnp.concatenate([z(MLA_NOPE), -o(HALF_ROPE), o(HALF_ROPE), z(MLA_ROPE)]),
        z(LANES), z(LANES), z(LANES)])
    tm = 1024
    return pl.pallas_call(
        _tables_body,
        grid=(t // tm,),
        in_specs=[pl.BlockSpec((tm, 1), lambda i: (i, 0)), _resident((8, LANES))],
        out_specs=pl.BlockSpec((4, tm, LANES), lambda i: (0, i, 0)),
        out_shape=jax.ShapeDtypeStruct((4, t, LANES), F32),
        compiler_params=pltpu.CompilerParams(dimension_semantics=("parallel",)),
        name="rope_tables",
    )(pos, rows)


def _ffn_body(*refs, with_proj):
    if with_proj:
        (x_ref, yr_ref, ym_ref, wo_ref, g_ref, wg_ref, wu_ref, wd_ref, o_ref, a_ref) = refs
        x = (x_ref[...]
             + jnp.dot(yr_ref[...], wo_ref[0:RET_W, :], preferred_element_type=F32)
             + jnp.dot(ym_ref[...], wo_ref[RET_W:, :], preferred_element_type=F32))
    else:
        (x_ref, g_ref, wg_ref, wu_ref, wd_ref, o_ref, a_ref) = refs
        x = x_ref[...]
    h = _rms(x, g_ref[...]).astype(BF16)
    for c in range(D_FF // FF_TILE):
        cols = slice(c * FF_TILE, (c + 1) * FF_TILE)
        g = jnp.dot(h, wg_ref[:, cols], preferred_element_type=F32)
        u = jnp.dot(h, wu_ref[:, cols], preferred_element_type=F32)
        a_ref[:, cols] = (g * jax.nn.sigmoid(g) * u).astype(BF16)
    y = jnp.dot(a_ref[...], wd_ref[...], preferred_element_type=F32)
    o_ref[...] = x + 0.5 * y


def _cast_body(*refs):
    n = len(refs) // 2
    for src, dst in zip(refs[:n], refs[n:]):
        dst[...] = src[...].astype(BF16)


def _cast_stacks(ws):
    depth, r, c = ws[0].shape
    tr = CAST_ROWS
    blk = pl.BlockSpec((1, tr, c), lambda l, i: (l, i, 0))
    return pl.pallas_call(
        _cast_body,
        grid=(depth, r // tr),
        in_specs=[blk] * len(ws),
        out_specs=[blk] * len(ws),
        out_shape=[jax.ShapeDtypeStruct(w.shape, BF16) for w in ws],
        compiler_params=pltpu.CompilerParams(
            dimension_semantics=("parallel", "parallel"), vmem_limit_bytes=VMEM_LIMIT),
        name="cast_weights",
    )(*ws)


def _layer_block(w, l):
    return pl.BlockSpec((None,) + w.shape[1:], lambda *_: (l, 0, 0))


def _ffn(x, gain, wg, wu, wd, l, proj=None):
    t = x.shape[0]
    tm = TOK_TILE
    row = lambda w: pl.BlockSpec((tm, w), lambda i: (i, 0))
    ins, specs = [x], [row(D_MODEL)]
    if proj is not None:
        yr, ym, wo = proj
        ins += [yr, ym, wo]
        specs += [row(RET_W), row(MLA_HEADS * MLA_V), _resident(wo.shape)]
    ins += [gain, wg, wu, wd]
    specs += [_resident(gain.shape), _layer_block(wg, l), _layer_block(wu, l), _layer_block(wd, l)]
    return pl.pallas_call(
        functools.partial(_ffn_body, with_proj=proj is not None),
        grid=(t // tm,),
        in_specs=specs,
        out_specs=row(D_MODEL),
        out_shape=jax.ShapeDtypeStruct((t, D_MODEL), F32),
        scratch_shapes=[pltpu.VMEM((tm, D_FF), BF16)],
        compiler_params=pltpu.CompilerParams(
            dimension_semantics=("parallel",), vmem_limit_bytes=VMEM_LIMIT),
        name="ffn_proj" if proj is not None else "ffn",
    )(*ins)


def _group_mean(x, gmat):
    x2 = x * x
    hi = x2.astype(BF16)
    lo = (x2 - hi.astype(F32)).astype(BF16)
    return (jnp.dot(hi, gmat, preferred_element_type=F32)
            + jnp.dot(lo, gmat, preferred_element_type=F32))


def _inproj_body(x_ref, gmix_ref, win_ref, tab_ref, gq_ref, wuq_ref, gkv_ref, wk_ref, wv_ref,
                 gmat_ref, vec_ref,
                 rq_ref, rk_ref, rv_ref, rg_ref, qc_ref, kc_ref, v_ref):
    h = _rms(x_ref[...], gmix_ref[...]).astype(BF16)
    proj = jnp.dot(h, win_ref[...], preferred_element_type=F32)
    cos_r, sin_r, cos_m, sin_m = tab_ref[0], tab_ref[1], tab_ref[2], tab_ref[3]
    ret_scale = RET_DIM ** -0.5
    for hd in range(RET_HEADS):
        sl = slice(hd * RET_DIM, (hd + 1) * RET_DIM)
        xq = proj[:, sl]
        rq_ref[:, sl] = (xq * cos_r + pltpu.roll(xq, RET_DIM // 2, 1) * sin_r).astype(BF16)
        xk = proj[:, RET_W + hd * RET_DIM:RET_W + (hd + 1) * RET_DIM]
        rk_ref[:, sl] = ((xk * cos_r + pltpu.roll(xk, RET_DIM // 2, 1) * sin_r)
                         * ret_scale).astype(BF16)
    rv_ref[...] = proj[:, 2 * RET_W:3 * RET_W].astype(BF16)
    rg_ref[...] = proj[:, 3 * RET_W:4 * RET_W]
    off = 4 * RET_W
    c_q = proj[:, off:off + Q_LORA]
    c_kv = proj[:, off + Q_LORA:off + Q_LORA + KV_LORA]
    kr = proj[:, off + Q_LORA + KV_LORA:]
    cqn = _rms(c_q, gq_ref[...]).astype(BF16)
    ckvn = _rms(c_kv, gkv_ref[...]).astype(BF16)
    qraw = jnp.dot(cqn, wuq_ref[...], preferred_element_type=F32)
    kraw = jnp.dot(ckvn, wk_ref[...], preferred_element_type=F32)
    v_ref[...] = (jnp.dot(ckvn, wv_ref[...], preferred_element_type=F32)
                  + vec_ref[3:4, :]).astype(BF16)

    lane = lax.broadcasted_iota(jnp.int32, (1, LANES), 1)
    in_rope = (lane >= MLA_NOPE) & (lane < MLA_NOPE + MLA_ROPE)
    ss = jnp.sum(jnp.where(in_rope, kr * kr, 0.0), axis=-1, keepdims=True) * (1.0 / MLA_ROPE)
    krn = kr * lax.rsqrt(ss + EPS) * vec_ref[2:3, 0:LANES]
    kro = krn * cos_m + pltpu.roll(krn, SLAB_SHIFT, 1) * sin_m

    gmat = gmat_ref[...]
    for pr in range(MLA_HEADS // 2):
        cols = slice(pr * 2 * LANES, (pr + 1) * 2 * LANES)
        xs = qraw[:, cols]
        y = xs * lax.rsqrt(_group_mean(xs, gmat) + EPS) * vec_ref[0:1, 0:2 * LANES]
        for e in range(2):
            sl = slice(e * LANES, (e + 1) * LANES)
            dst = slice((2 * pr + e) * LANES, (2 * pr + e + 1) * LANES)
            ys = y[:, sl]
            qc_ref[:, dst] = (ys * cos_m + pltpu.roll(ys, SLAB_SHIFT, 1) * sin_m).astype(BF16)
            ks = kraw[:, dst]
            ms = jnp.sum(ks * ks, axis=-1, keepdims=True) * (1.0 / MLA_NOPE)
            kc_ref[:, dst] = (ks * lax.rsqrt(ms + EPS) * vec_ref[1:2, 0:LANES] + kro).astype(BF16)


def _inproj(x, tabs, gmix, win, gq, wuq, gkv, wk, wv, gmat, vecs):
    t = x.shape[0]
    tm = TOK_TILE
    row = lambda w: pl.BlockSpec((tm, w), lambda i: (i, 0))
    slab_w = MLA_HEADS * LANES
    outs = [(RET_W, BF16), (RET_W, BF16), (RET_W, BF16), (RET_W, F32),
            (slab_w, BF16), (slab_w, BF16), (slab_w, BF16)]
    return pl.pallas_call(
        _inproj_body,
        grid=(t // tm,),
        in_specs=[row(D_MODEL), _resident(gmix.shape), _resident(win.shape),
                  pl.BlockSpec((4, tm, LANES), lambda i: (0, i, 0)),
                  _resident(gq.shape), _resident(wuq.shape), _resident(gkv.shape),
                  _resident(wk.shape), _resident(wv.shape), _resident(gmat.shape),
                  _resident(vecs.shape)],
        out_specs=[row(w) for w, _ in outs],
        out_shape=[jax.ShapeDtypeStruct((t, w), dt) for w, dt in outs],
        compiler_params=pltpu.CompilerParams(
            dimension_semantics=("parallel",), vmem_limit_bytes=VMEM_LIMIT),
        name="inproj",
    )(x, gmix, win, tabs, gq, wuq, gkv, wk, wv, gmat, vecs)


def _retention_body(q_ref, k_ref, v_ref, g_ref, dec_ref, gn_ref, o_ref,
                    state_ref, s_ref, kv_ref, prev_ref):
    @pl.when(pl.program_id(1) == 0)
    def _():
        state_ref[...] = jnp.zeros_like(state_ref)

    c = RET_CHUNK
    nt = (((1,), (1,)), ((), ()))
    n = RET_TILE // c
    blocks = [(hd, ci) for ci in range(n) for hd in range(RET_HEADS)]
    tile = lambda hd, ci: (slice(ci * c, (ci + 1) * c), slice(hd * RET_DIM, (hd + 1) * RET_DIM))

    for hd, ci in blocks:
        rows, cols = tile(hd, ci)
        k = k_ref[0, rows, cols]
        v = v_ref[0, rows, cols]
        s = lax.dot_general(q_ref[0, rows, cols], k, nt, preferred_element_type=F32)
        s_ref[hd * n + ci] = (s * dec_ref[0, hd]).astype(BF16)
        kz = (k.astype(F32) * dec_ref[1, hd]).astype(BF16)
        kv_ref[hd * n + ci] = jnp.dot(kz.T, v, preferred_element_type=F32)
    for hd in range(RET_HEADS):
        st = state_ref[hd]
        for ci in range(n):
            prev_ref[hd * n + ci] = st.astype(BF16)
            st = st * dec_ref[3, hd] + kv_ref[hd * n + ci]
        state_ref[hd] = st
    for hd, ci in blocks:
        rows, cols = tile(hd, ci)
        q = q_ref[0, rows, cols]
        y = (jnp.dot(s_ref[hd * n + ci], v_ref[0, rows, cols], preferred_element_type=F32)
             + jnp.dot(q, prev_ref[hd * n + ci], preferred_element_type=F32) * dec_ref[2, hd])
        g = g_ref[0, rows, cols]
        o_ref[0, rows, cols] = (_rms(y, gn_ref[hd]) * (g * jax.nn.sigmoid(g))).astype(BF16)


def _retention(rq, rk, rv, rg, dec, gn, batch, seq):
    shp = (batch, seq, RET_W)
    nblk = RET_HEADS * (RET_TILE // RET_CHUNK)
    blk = pl.BlockSpec((1, RET_TILE, RET_W), lambda b, s: (b, s, 0))
    return pl.pallas_call(
        _retention_body,
        grid=(batch, seq // RET_TILE),
        in_specs=[blk, blk, blk, blk, _resident(dec.shape), _resident(gn.shape)],
        out_specs=blk,
        out_shape=jax.ShapeDtypeStruct(shp, BF16),
        scratch_shapes=[pltpu.VMEM((RET_HEADS, RET_DIM, RET_DIM), F32),
                        pltpu.VMEM((nblk, RET_CHUNK, RET_CHUNK), BF16),
                        pltpu.VMEM((nblk, RET_DIM, RET_DIM), F32),
                        pltpu.VMEM((nblk, RET_DIM, RET_DIM), BF16)],
        compiler_params=pltpu.CompilerParams(
            dimension_semantics=("parallel", "arbitrary"), vmem_limit_bytes=VMEM_LIMIT),
        name="retention",
    )(rq.reshape(shp), rk.reshape(shp), rv.reshape(shp), rg.reshape(shp), dec, gn)


def _attention_body(q_ref, k_ref, v_ref, o_ref, m_ref, acc_ref):
    qi = pl.program_id(2)
    nt = (((1,), (1,)), ((), ()))
    half = ATT_TQ // 2
    m_ref[...] = jnp.full_like(m_ref, -1e30)
    acc_ref[...] = jnp.zeros_like(acc_ref)

    def steps(specs):
        cols = lambda e: slice(e * LANES, (e + 1) * LANES)
        keys = lambda start, width: pl.ds(pl.multiple_of(start, half), width)
        scores = [lax.dot_general(q_ref[0, q_rows, cols(e)], k_ref[0, keys(start, width), cols(e)],
                                  nt, preferred_element_type=F32)
                  for e, q_rows, start, width, _ in specs]
        probs = []
        for (e, q_rows, _, width, mask_from), s in zip(specs, scores):
            chunks = [s[:, c * LANES:(c + 1) * LANES] for c in range(width // LANES)]
            if mask_from is not None:
                r = lax.broadcasted_iota(jnp.int32, chunks[0].shape, 0)
                lane = lax.broadcasted_iota(jnp.int32, chunks[0].shape, 1)
                for c in range(mask_from, len(chunks)):
                    chunks[c] = jnp.where(r >= lane + (c - mask_from) * LANES, chunks[c], -1e30)
            m_old = m_ref[e, q_rows]
            m_new = jnp.maximum(
                m_old, jnp.max(functools.reduce(jnp.maximum, chunks), axis=-1, keepdims=True))
            m_ref[e, q_rows] = m_new
            probs.append((jnp.exp(m_old - m_new), jnp.concatenate(
                [jnp.exp(ch - m_new).astype(BF16) for ch in chunks], axis=1)))
        for (e, q_rows, start, width, _), (alpha, p) in zip(specs, probs):
            acc_ref[e, q_rows] = alpha * acc_ref[e, q_rows] + jnp.dot(
                p, v_ref[0, keys(start, width), cols(e)], preferred_element_type=F32)

    def loop_body(kb, carry):
        for e in range(2):
            steps([(e, slice(0, ATT_TQ), kb * ATT_TQ, ATT_TQ, None)])
        return carry

    lax.fori_loop(0, qi, loop_body, 0)
    diag = qi * ATT_TQ
    steps([spec for e in range(2) for spec in (
        (e, slice(0, half), diag, half, 0),
        (e, slice(half, ATT_TQ), diag, ATT_TQ, half // LANES))])
    outs = []
    for e in range(2):
        acc = acc_ref[e]
        outs.append(acc * (1.0 / acc[:, MLA_V:MLA_V + 1]))
    lane = lax.broadcasted_iota(jnp.int32, outs[0].shape, 1)
    o_ref[0] = jnp.where(lane < MLA_V, outs[0], pltpu.roll(outs[1], MLA_V, 1)).astype(BF16)


def _attention(qc, kc, vc, batch, seq):
    slab_w = MLA_HEADS * LANES
    pairs = MLA_HEADS // 2
    seq_blk = pl.BlockSpec((1, seq, 2 * LANES), lambda b, j, i: (b, 0, j))
    return pl.pallas_call(
        _attention_body,
        grid=(batch, pairs, seq // ATT_TQ),
        in_specs=[pl.BlockSpec((1, ATT_TQ, 2 * LANES), lambda b, j, i: (b, i, j)),
                  seq_blk, seq_blk],
        out_specs=pl.BlockSpec((1, ATT_TQ, LANES), lambda b, j, i: (b, i, j)),
        out_shape=jax.ShapeDtypeStruct((batch, seq, MLA_HEADS * MLA_V), BF16),
        scratch_shapes=[pltpu.VMEM((2, ATT_TQ, LANES), F32),
                        pltpu.VMEM((2, ATT_TQ, LANES), F32)],
        compiler_params=pltpu.CompilerParams(
            dimension_semantics=("parallel", "parallel", "arbitrary"),
            vmem_limit_bytes=VMEM_LIMIT),
        name="latent_attention",
    )(qc.reshape(batch, seq, slab_w), kc.reshape(batch, seq, slab_w),
      vc.reshape(batch, seq, slab_w))


def _retention_decay():
    c = RET_CHUNK
    lg = jnp.log(1.0 - 2.0 ** (-5.0 - jnp.arange(RET_HEADS, dtype=F32)))
    idx = jnp.arange(c, dtype=F32)
    rel = idx[:, None] - idx[None, :]
    decay = jnp.where(rel >= 0, jnp.exp(jnp.maximum(rel, 0.0)[None] * lg[:, None, None]), 0.0)
    zeta = jnp.exp((c - 1 - idx)[None, :] * lg[:, None])
    xi = jnp.exp((idx + 1.0)[None, :] * lg[:, None])
    cdec = jnp.exp(c * lg)
    full = (RET_HEADS, c, RET_DIM)
    return jnp.stack([decay,
                      jnp.broadcast_to(zeta[:, :, None], full),
                      jnp.broadcast_to(xi[:, :, None], full),
                      jnp.broadcast_to(cdec[:, None, None], full)])


def _swapped_rope_cols(w):
    a, b = w[..., :HALF_ROPE], w[..., HALF_ROPE:]
    return jnp.concatenate([a, b, b, a], axis=-1)


def _layer_params(l, p):
    row = lambda g: g[l].reshape(1, -1)
    w_in = p["w_in"][l]
    base = 4 * RET_W + Q_LORA + KV_LORA
    kr_cols = jnp.concatenate(
        [jnp.zeros((D_MODEL, MLA_NOPE), F32), _swapped_rope_cols(w_in[:, base:])], axis=-1)
    win = jnp.concatenate([w_in[:, :base], kr_cols], axis=-1).astype(BF16)

    dq = MLA_NOPE + MLA_ROPE
    wuq = p["w_uq"][l].reshape(Q_LORA, MLA_HEADS, dq)
    wuq = jnp.concatenate([wuq[..., :MLA_NOPE], _swapped_rope_cols(wuq[..., MLA_NOPE:])], axis=-1)
    wuq = wuq.reshape(Q_LORA, MLA_HEADS * LANES).astype(BF16)
    wukv = p["w_ukv"][l].reshape(KV_LORA, MLA_HEADS, MLA_NOPE + MLA_V)
    wk = jnp.concatenate([wukv[..., :MLA_NOPE], jnp.zeros_like(wukv[..., :MLA_NOPE])], axis=-1)
    wk = wk.reshape(KV_LORA, MLA_HEADS * LANES).astype(BF16)
    wv = jnp.concatenate([wukv[..., MLA_NOPE:], jnp.zeros_like(wukv[..., :LANES - MLA_V])], axis=-1)
    wv = wv.reshape(KV_LORA, MLA_HEADS * LANES).astype(BF16)

    scale = dq ** -0.5
    zeros64 = jnp.zeros((MLA_NOPE,), F32)
    gq = jnp.concatenate([p["qn_nope"][l], _swapped_rope_cols(p["qn_rope"][l])]) * scale
    gk = jnp.concatenate([p["kn_nope"][l], zeros64])
    gkr = jnp.concatenate([zeros64, _swapped_rope_cols(p["kn_rope"][l])])
    one_lane = jnp.zeros((LANES,), F32).at[MLA_V].set(1.0)
    vecs = jnp.stack([jnp.tile(r, MLA_HEADS) for r in (gq, gk, gkr, one_lane)]
                     + [jnp.zeros((MLA_HEADS * LANES,), F32)] * 4)
    return dict(
        ffn1_gain=row(p["ffn1_norm"]),
        ffn2_gain=row(p["ffn2_norm"]),
        inproj=(row(p["mix_norm"]), win, row(p["q_lat_norm"]), wuq, row(p["kv_lat_norm"]), wk, wv),
        vecs=vecs,
        gn=p["ret_head_norm"][l].reshape(RET_HEADS, 1, RET_DIM),
        wo=p["w_o"][l].astype(BF16),
    )


def _group_matrix():
    g = np.zeros((LANES, LANES), np.float32)
    g[:MLA_NOPE, :MLA_NOPE] = 1.0 / MLA_NOPE
    g[MLA_NOPE:MLA_NOPE + MLA_ROPE, MLA_NOPE:] = 1.0 / MLA_ROPE
    z = np.zeros_like(g)
    return jnp.asarray(np.block([[g, z], [z, g]]), dtype=BF16)


def kernel(x, positions, ffn1_norm, ffn1_w_gate, ffn1_w_up, ffn1_w_down, mix_norm, w_in, ret_head_norm, q_lat_norm, w_uq, kv_lat_norm, w_ukv, qn_nope, qn_rope, kn_nope, kn_rope, w_o, ffn2_norm, ffn2_w_gate, ffn2_w_up, ffn2_w_down):
    p = dict(ffn1_norm=ffn1_norm, ffn1_w_gate=ffn1_w_gate, ffn1_w_up=ffn1_w_up,
             ffn1_w_down=ffn1_w_down, mix_norm=mix_norm, w_in=w_in,
             ret_head_norm=ret_head_norm, q_lat_norm=q_lat_norm, w_uq=w_uq,
             kv_lat_norm=kv_lat_norm, w_ukv=w_ukv, qn_nope=qn_nope, qn_rope=qn_rope,
             kn_nope=kn_nope, kn_rope=kn_rope, w_o=w_o, ffn2_norm=ffn2_norm,
             ffn2_w_gate=ffn2_w_gate, ffn2_w_up=ffn2_w_up, ffn2_w_down=ffn2_w_down)
    batch, seq, d = x.shape
    depth = w_in.shape[0]
    tabs = _rope_tables(positions)
    dec = _retention_decay()
    gmat = _group_matrix()
    g1, u1, g2, u2 = _cast_stacks([ffn1_w_gate, ffn1_w_up, ffn2_w_gate, ffn2_w_up])
    d1, d2 = _cast_stacks([ffn1_w_down, ffn2_w_down])
    xf = x.reshape(batch * seq, d)
    for l in range(depth):
        lp = _layer_params(l, p)
        xf = _ffn(xf, lp["ffn1_gain"], g1, u1, d1, l)
        gmix, win, gq, wuq, gkv, wk, wv = lp["inproj"]
        rq, rk, rv, rg, qc, kc, v = _inproj(xf, tabs, gmix, win, gq, wuq, gkv, wk, wv,
                                            gmat, lp["vecs"])
        y_ret = _retention(rq, rk, rv, rg, dec, lp["gn"], batch, seq)
        y_mla = _attention(qc, kc, v, batch, seq)
        xf = _ffn(xf, lp["ffn2_gain"], g2, u2, d2, l,
                  proj=(y_ret.reshape(batch * seq, -1), y_mla.reshape(batch * seq, -1), lp["wo"]))
    return xf.reshape(batch, seq, d)
```

```python
import functools

import jax
import jax.numpy as jnp
import numpy as np
from jax import lax
from jax.experimental import pallas as pl
from jax.experimental.pallas import tpu as pltpu

D_MODEL = 1024
D_FF = 2816
RET_HEADS = 4
RET_DIM = 128
RET_CHUNK = 128
MLA_HEADS = 8
MLA_NOPE = 64
MLA_ROPE = 32
MLA_V = 64
Q_LORA = 256
KV_LORA = 128
ROPE_THETA = 10000.0
EPS = 1e-6

LANES = 128
BF16_ROWS = 16
RET_W = RET_HEADS * RET_DIM
IN_W = 4 * RET_W + Q_LORA + KV_LORA + LANES
HALF_ROPE = MLA_ROPE // 2
SLAB_SHIFT = LANES - MLA_ROPE

TOK_TILE = 512
FF_TILE = 256
CAST_ROWS = 256
ATT_TQ = 1024
RET_TILE = 1024
VMEM_LIMIT = 48 * 1024 * 1024

F32 = jnp.float32
BF16 = jnp.bfloat16


def _rms(x, gain):
    return x * lax.rsqrt(jnp.mean(x * x, axis=-1, keepdims=True) + EPS) * gain


def _resident(shape):
    nd = len(shape)
    return pl.BlockSpec(shape, lambda *_: (0,) * nd)


def _tables_rows(pos_ref, c_ref, o_ref, rows):
    ang = pos_ref[rows, :].astype(F32) * c_ref[0:1, :]
    c, s = jnp.cos(ang), jnp.sin(ang)
    low = lax.broadcasted_iota(jnp.int32, c.shape, 1) < RET_DIM // 2
    o_ref[0, rows] = jnp.where(low, c, pltpu.roll(c, RET_DIM // 2, 1))
    o_ref[1, rows] = jnp.where(low, s, pltpu.roll(s, RET_DIM // 2, 1)) * c_ref[1:2, :]
    o_ref[2, rows] = c_ref[2:3, :] + c_ref[3:4, :] * c
    o_ref[3, rows] = s * c_ref[4:5, :]


def _table_consts():
    inv_r = 1.0 / (ROPE_THETA ** (jnp.arange(0, RET_DIM, 2, dtype=F32) / RET_DIM))
    inv_m = 1.0 / (ROPE_THETA ** (jnp.arange(0, MLA_ROPE, 2, dtype=F32) / MLA_ROPE))
    half = RET_DIM // 2
    z = lambda n: jnp.zeros((n,), F32)
    o = lambda n: jnp.ones((n,), F32)
    return jnp.stack([
        jnp.concatenate([inv_r, inv_m, inv_m, z(MLA_ROPE)]),
        jnp.concatenate([-o(half), o(half)]),
        jnp.concatenate([o(MLA_NOPE), z(LANES - MLA_NOPE)]),
        jnp.concatenate([z(MLA_NOPE), o(MLA_ROPE), z(MLA_ROPE)]),
        jnp.concatenate([z(MLA_NOPE), -o(HALF_ROPE), o(HALF_ROPE), z(MLA_ROPE)]),
        z(LANES), z(LANES), z(LANES)])


def _ffn_body(*refs, with_proj, with_tables, n_side):
    refs = list(refs)
    take = lambda n: [refs.pop(0) for _ in range(n)]
    (x_ref,) = take(1)
    if with_proj:
        yr_ref, ym_ref, wo_ref = take(3)
    g_ref, wg_ref, wu_ref, wd_ref = take(4)
    if with_tables:
        pos_ref, c_ref = take(2)
    side_in = take(n_side)
    (o_ref,) = take(1)
    if with_tables:
        (tab_ref,) = take(1)
    side_out = take(n_side)
    (a_ref,) = take(1)

    x = x_ref[...]
    if with_proj:
        x = (x + jnp.dot(yr_ref[...], wo_ref[0:RET_W, :], preferred_element_type=F32)
             + jnp.dot(ym_ref[...], wo_ref[RET_W:, :], preferred_element_type=F32))
    h = _rms(x, g_ref[...]).astype(BF16)
    n_chunks = D_FF // FF_TILE
    tab_rows = x.shape[0] // n_chunks // 8 * 8
    for c in range(n_chunks):
        cols = slice(c * FF_TILE, (c + 1) * FF_TILE)
        g = jnp.dot(h, wg_ref[:, cols], preferred_element_type=F32)
        u = jnp.dot(h, wu_ref[:, cols], preferred_element_type=F32)
        a_ref[:, cols] = (g * jax.nn.sigmoid(g) * u).astype(BF16)
        if with_tables:
            stop = x.shape[0] if c == n_chunks - 1 else (c + 1) * tab_rows
            _tables_rows(pos_ref, c_ref, tab_ref, slice(c * tab_rows, stop))
    for src, dst in zip(side_in, side_out):
        dst[...] = src[...].astype(BF16)
    y = jnp.dot(a_ref[...], wd_ref[...], preferred_element_type=F32)
    o_ref[...] = x + 0.5 * y


def _cast_body(*refs):
    n = len(refs) // 2
    for src, dst in zip(refs[:n], refs[n:]):
        dst[...] = src[...].astype(BF16)


def _cast_stacks(ws, layers):
    _, r, c = ws[0].shape
    tr = CAST_ROWS
    blk = pl.BlockSpec((1, tr, c), lambda l, i: (l, i, 0))
    return pl.pallas_call(
        _cast_body,
        grid=(layers, r // tr),
        in_specs=[blk] * len(ws),
        out_specs=[blk] * len(ws),
        out_shape=[jax.ShapeDtypeStruct((layers, r, c), BF16) for w in ws],
        compiler_params=pltpu.CompilerParams(
            dimension_semantics=("parallel", "parallel"), vmem_limit_bytes=VMEM_LIMIT),
        name="cast_weights",
    )(*ws)


def _weight_block(w):
    if w.ndim == 2:
        return _resident(w.shape)
    return pl.BlockSpec((None,) + w.shape[1:], lambda *_: (0, 0, 0))


def _side_cast_specs(w, l, steps):
    _, r, c = w.shape
    hold = 1
    while (r * hold) % (steps * BF16_ROWS) or steps % hold:
        hold += 1
    rows = r * hold // steps
    return (pl.BlockSpec((None, rows, c), lambda i: (l, i // hold, 0)),
            pl.BlockSpec((rows, c), lambda i: (i // hold, 0)),
            jax.ShapeDtypeStruct((r, c), BF16))


def _ffn(x, gain, wg, wu, wd, proj=None, positions=None, side=None):
    t = x.shape[0]
    tm = TOK_TILE
    steps = t // tm
    row = lambda w: pl.BlockSpec((tm, w), lambda i: (i, 0))
    ins, specs = [x], [row(D_MODEL)]
    if proj is not None:
        yr, ym, wo = proj
        ins += [yr, ym, wo]
        specs += [row(RET_W), row(MLA_HEADS * MLA_V), _resident(wo.shape)]
    ins += [gain, wg, wu, wd]
    specs += [_resident(gain.shape), _weight_block(wg), _weight_block(wu), _weight_block(wd)]
    out_specs = [row(D_MODEL)]
    out_shape = [jax.ShapeDtypeStruct((t, D_MODEL), F32)]
    if positions is not None:
        ins += [positions.reshape(t, 1), _table_consts()]
        specs += [pl.BlockSpec((tm, 1), lambda i: (i, 0)), _resident((8, LANES))]
        out_specs.append(pl.BlockSpec((4, tm, LANES), lambda i: (0, i, 0)))
        out_shape.append(jax.ShapeDtypeStruct((4, t, LANES), F32))
    side_ws, side_layer = side if side is not None else ([], 0)
    for w in side_ws:
        in_spec, out_spec, shape = _side_cast_specs(w, side_layer, steps)
        ins.append(w)
        specs.append(in_spec)
        out_specs.append(out_spec)
        out_shape.append(shape)
    outs = pl.pallas_call(
        functools.partial(_ffn_body, with_proj=proj is not None,
                          with_tables=positions is not None, n_side=len(side_ws)),
        grid=(steps,),
        in_specs=specs,
        out_specs=out_specs,
        out_shape=out_shape,
        scratch_shapes=[pltpu.VMEM((tm, D_FF), BF16)],
        compiler_params=pltpu.CompilerParams(
            dimension_semantics=("arbitrary",), vmem_limit_bytes=VMEM_LIMIT),
        name="ffn_proj" if proj is not None else "ffn",
    )(*ins)
    tabs = outs[1] if positions is not None else None
    return outs[0], tabs, list(outs[len(outs) - len(side_ws):])


def _group_mean(x, gmat):
    x2 = x * x
    hi = x2.astype(BF16)
    lo = (x2 - hi.astype(F32)).astype(BF16)
    return (jnp.dot(hi, gmat, preferred_element_type=F32)
            + jnp.dot(lo, gmat, preferred_element_type=F32))


def _inproj_body(x_ref, gmix_ref, win_ref, tab_ref, gq_ref, wuq_ref, gkv_ref, wk_ref, wv_ref,
                 gmat_ref, vec_ref,
                 rq_ref, rk_ref, rv_ref, rg_ref, qc_ref, kc_ref, v_ref):
    h = _rms(x_ref[...], gmix_ref[...]).astype(BF16)
    proj = jnp.dot(h, win_ref[...], preferred_element_type=F32)
    cos_r, sin_r, cos_m, sin_m = tab_ref[0], tab_ref[1], tab_ref[2], tab_ref[3]
    ret_scale = RET_DIM ** -0.5
    for hd in range(RET_HEADS):
        sl = slice(hd * RET_DIM, (hd + 1) * RET_DIM)
        xq = proj[:, sl]
        rq_ref[:, sl] = (xq * cos_r + pltpu.roll(xq, RET_DIM // 2, 1) * sin_r).astype(BF16)
        xk = proj[:, RET_W + hd * RET_DIM:RET_W + (hd + 1) * RET_DIM]
        rk_ref[:, sl] = ((xk * cos_r + pltpu.roll(xk, RET_DIM // 2, 1) * sin_r)
                         * ret_scale).astype(BF16)
    rv_ref[...] = proj[:, 2 * RET_W:3 * RET_W].astype(BF16)
    rg_ref[...] = proj[:, 3 * RET_W:4 * RET_W]
    off = 4 * RET_W
    c_q = proj[:, off:off + Q_LORA]
    c_kv = proj[:, off + Q_LORA:off + Q_LORA + KV_LORA]
    kr = proj[:, off + Q_LORA + KV_LORA:]
    cqn = _rms(c_q, gq_ref[...]).astype(BF16)
    ckvn = _rms(c_kv, gkv_ref[...]).astype(BF16)
    qraw = jnp.dot(cqn, wuq_ref[...], preferred_element_type=F32)
    kraw = jnp.dot(ckvn, wk_ref[...], preferred_element_type=F32)
    v_ref[...] = (jnp.dot(ckvn, wv_ref[...], preferred_element_type=F32)
                  + vec_ref[3:4, :]).astype(BF16)

    lane = lax.broadcasted_iota(jnp.int32, (1, LANES), 1)
    in_rope = (lane >= MLA_NOPE) & (lane < MLA_NOPE + MLA_ROPE)
    ss = jnp.sum(jnp.where(in_rope, kr * kr, 0.0), axis=-1, keepdims=True) * (1.0 / MLA_ROPE)
    krn = kr * lax.rsqrt(ss + EPS) * vec_ref[2:3, 0:LANES]
    kro = krn * cos_m + pltpu.roll(krn, SLAB_SHIFT, 1) * sin_m

    gmat = gmat_ref[...]
    for pr in range(MLA_HEADS // 2):
        cols = slice(pr * 2 * LANES, (pr + 1) * 2 * LANES)
        xs = qraw[:, cols]
        y = xs * lax.rsqrt(_group_mean(xs, gmat) + EPS) * vec_ref[0:1, 0:2 * LANES]
        for e in range(2):
            sl = slice(e * LANES, (e + 1) * LANES)
            dst = slice((2 * pr + e) * LANES, (2 * pr + e + 1) * LANES)
            ys = y[:, sl]
            qc_ref[:, dst] = (ys * cos_m + pltpu.roll(ys, SLAB_SHIFT, 1) * sin_m).astype(BF16)
            ks = kraw[:, dst]
            ms = jnp.sum(ks * ks, axis=-1, keepdims=True) * (1.0 / MLA_NOPE)
            kc_ref[:, dst] = (ks * lax.rsqrt(ms + EPS) * vec_ref[1:2, 0:LANES] + kro).astype(BF16)


def _inproj(x, tabs, gmix, win, gq, wuq, gkv, wk, wv, gmat, vecs):
    t = x.shape[0]
    tm = TOK_TILE
    row = lambda w: pl.BlockSpec((tm, w), lambda i: (i, 0))
    slab_w = MLA_HEADS * LANES
    outs = [(RET_W, BF16), (RET_W, BF16), (RET_W, BF16), (RET_W, F32),
            (slab_w, BF16), (slab_w, BF16), (slab_w, BF16)]
    return pl.pallas_call(
        _inproj_body,
        grid=(t // tm,),
        in_specs=[row(D_MODEL), _resident(gmix.shape), _resident(win.shape),
                  pl.BlockSpec((4, tm, LANES), lambda i: (0, i, 0)),
                  _resident(gq.shape), _resident(wuq.shape), _resident(gkv.shape),
                  _resident(wk.shape), _resident(wv.shape), _resident(gmat.shape),
                  _resident(vecs.shape)],
        out_specs=[row(w) for w, _ in outs],
        out_shape=[jax.ShapeDtypeStruct((t, w), dt) for w, dt in outs],
        compiler_params=pltpu.CompilerParams(
            dimension_semantics=("parallel",), vmem_limit_bytes=VMEM_LIMIT),
        name="inproj",
    )(x, gmix, win, tabs, gq, wuq, gkv, wk, wv, gmat, vecs)


def _retention_body(q_ref, k_ref, v_ref, g_ref, dec_ref, gn_ref, o_ref,
                    state_ref, s_ref, kv_ref, prev_ref):
    @pl.when(pl.program_id(1) == 0)
    def _():
        state_ref[...] = jnp.zeros_like(state_ref)

    c = RET_CHUNK
    nt = (((1,), (1,)), ((), ()))
    n = RET_TILE // c
    blocks = [(hd, ci) for ci in range(n) for hd in range(RET_HEADS)]
    tile = lambda hd, ci: (slice(ci * c, (ci + 1) * c), slice(hd * RET_DIM, (hd + 1) * RET_DIM))

    for hd, ci in blocks:
        rows, cols = tile(hd, ci)
        k = k_ref[0, rows, cols]
        v = v_ref[0, rows, cols]
        s = lax.dot_general(q_ref[0, rows, cols], k, nt, preferred_element_type=F32)
        s_ref[hd * n + ci] = (s * dec_ref[0, hd]).astype(BF16)
        kz = (k.astype(F32) * dec_ref[1, hd]).astype(BF16)
        kv_ref[hd * n + ci] = jnp.dot(kz.T, v, preferred_element_type=F32)
    for hd in range(RET_HEADS):
        st = state_ref[hd]
        for ci in range(n):
            prev_ref[hd * n + ci] = st.astype(BF16)
            st = st * dec_ref[3, hd] + kv_ref[hd * n + ci]
        state_ref[hd] = st
    for hd, ci in blocks:
        rows, cols = tile(hd, ci)
        q = q_ref[0, rows, cols]
        y = (jnp.dot(s_ref[hd * n + ci], v_ref[0, rows, cols], preferred_element_type=F32)
             + jnp.dot(q, prev_ref[hd * n + ci], preferred_element_type=F32) * dec_ref[2, hd])
        g = g_ref[0, rows, cols]
        o_ref[0, rows, cols] = (_rms(y, gn_ref[hd]) * (g * jax.nn.sigmoid(g))).astype(BF16)


def _retention(rq, rk, rv, rg, dec, gn, batch, seq):
    shp = (batch, seq, RET_W)
    nblk = RET_HEADS * (RET_TILE // RET_CHUNK)
    blk = pl.BlockSpec((1, RET_TILE, RET_W), lambda b, s: (b, s, 0))
    return pl.pallas_call(
        _retention_body,
        grid=(batch, seq // RET_TILE),
        in_specs=[blk, blk, blk, blk, _resident(dec.shape), _resident(gn.shape)],
        out_specs=blk,
        out_shape=jax.ShapeDtypeStruct(shp, BF16),
        scratch_shapes=[pltpu.VMEM((RET_HEADS, RET_DIM, RET_DIM), F32),
                        pltpu.VMEM((nblk, RET_CHUNK, RET_CHUNK), BF16),
                        pltpu.VMEM((nblk, RET_DIM, RET_DIM), F32),
                        pltpu.VMEM((nblk, RET_DIM, RET_DIM), BF16)],
        compiler_params=pltpu.CompilerParams(
            dimension_semantics=("parallel", "arbitrary"), vmem_limit_bytes=VMEM_LIMIT),
        name="retention",
    )(rq.reshape(shp), rk.reshape(shp), rv.reshape(shp), rg.reshape(shp), dec, gn)


def _attention_body(q_ref, k_ref, v_ref, o_ref, m_ref, acc_ref):
    qi = pl.program_id(2)
    nt = (((1,), (1,)), ((), ()))
    half = ATT_TQ // 2
    m_ref[...] = jnp.full_like(m_ref, -1e30)
    acc_ref[...] = jnp.zeros_like(acc_ref)

    def steps(specs):
        cols = lambda e: slice(e * LANES, (e + 1) * LANES)
        keys = lambda start, width: pl.ds(pl.multiple_of(start, half), width)
        scores = [lax.dot_general(q_ref[0, q_rows, cols(e)], k_ref[0, keys(start, width), cols(e)],
                                  nt, preferred_element_type=F32)
                  for e, q_rows, start, width, _ in specs]
        probs = []
        for (e, q_rows, _, width, mask_from), s in zip(specs, scores):
            chunks = [s[:, c * LANES:(c + 1) * LANES] for c in range(width // LANES)]
            if mask_from is not None:
                r = lax.broadcasted_iota(jnp.int32, chunks[0].shape, 0)
                lane = lax.broadcasted_iota(jnp.int32, chunks[0].shape, 1)
                for c in range(mask_from, len(chunks)):
                    chunks[c] = jnp.where(r >= lane + (c - mask_from) * LANES, chunks[c], -1e30)
            m_old = m_ref[e, q_rows]
            m_new = jnp.maximum(
                m_old, jnp.max(functools.reduce(jnp.maximum, chunks), axis=-1, keepdims=True))
            m_ref[e, q_rows] = m_new
            probs.append((jnp.exp(m_old - m_new), jnp.concatenate(
                [jnp.exp(ch - m_new).astype(BF16) for ch in chunks], axis=1)))
        for (e, q_rows, start, width, _), (alpha, p) in zip(specs, probs):
            acc_ref[e, q_rows] = alpha * acc_ref[e, q_rows] + jnp.dot(
                p, v_ref[0, keys(start, width), cols(e)], preferred_element_type=F32)

    def loop_body(kb, carry):
        for e in range(2):
            steps([(e, slice(0, ATT_TQ), kb * ATT_TQ, ATT_TQ, None)])
        return carry

    lax.fori_loop(0, qi, loop_body, 0)
    diag = qi * ATT_TQ
    steps([spec for e in range(2) for spec in (
        (e, slice(0, half), diag, half, 0),
        (e, slice(half, ATT_TQ), diag, ATT_TQ, half // LANES))])
    outs = []
    for e in range(2):
        acc = acc_ref[e]
        outs.append(acc * (1.0 / acc[:, MLA_V:MLA_V + 1]))
    lane = lax.broadcasted_iota(jnp.int32, outs[0].shape, 1)
    o_ref[0] = jnp.where(lane < MLA_V, outs[0], pltpu.roll(outs[1], MLA_V, 1)).astype(BF16)


def _attention(qc, kc, vc, batch, seq):
    slab_w = MLA_HEADS * LANES
    pairs = MLA_HEADS // 2
    seq_blk = pl.BlockSpec((1, seq, 2 * LANES), lambda b, j, i: (b, 0, j))
    return pl.pallas_call(
        _attention_body,
        grid=(batch, pairs, seq // ATT_TQ),
        in_specs=[pl.BlockSpec((1, ATT_TQ, 2 * LANES), lambda b, j, i: (b, i, j)),
                  seq_blk, seq_blk],
        out_specs=pl.BlockSpec((1, ATT_TQ, LANES), lambda b, j, i: (b, i, j)),
        out_shape=jax.ShapeDtypeStruct((batch, seq, MLA_HEADS * MLA_V), BF16),
        scratch_shapes=[pltpu.VMEM((2, ATT_TQ, LANES), F32),
                        pltpu.VMEM((2, ATT_TQ, LANES), F32)],
        compiler_params=pltpu.CompilerParams(
            dimension_semantics=("parallel", "parallel", "arbitrary"),
            vmem_limit_bytes=VMEM_LIMIT),
        name="latent_attention",
    )(qc.reshape(batch, seq, slab_w), kc.reshape(batch, seq, slab_w),
      vc.reshape(batch, seq, slab_w))


def _retention_decay():
    c = RET_CHUNK
    lg = jnp.log(1.0 - 2.0 ** (-5.0 - jnp.arange(RET_HEADS, dtype=F32)))
    idx = jnp.arange(c, dtype=F32)
    rel = idx[:, None] - idx[None, :]
    decay = jnp.where(rel >= 0, jnp.exp(jnp.maximum(rel, 0.0)[None] * lg[:, None, None]), 0.0)
    zeta = jnp.exp((c - 1 - idx)[None, :] * lg[:, None])
    xi = jnp.exp((idx + 1.0)[None, :] * lg[:, None])
    cdec = jnp.exp(c * lg)
    full = (RET_HEADS, c, RET_DIM)
    return jnp.stack([decay,
                      jnp.broadcast_to(zeta[:, :, None], full),
                      jnp.broadcast_to(xi[:, :, None], full),
                      jnp.broadcast_to(cdec[:, None, None], full)])


def _swapped_rope_cols(w):
    a, b = w[..., :HALF_ROPE], w[..., HALF_ROPE:]
    return jnp.concatenate([a, b, b, a], axis=-1)


def _layer_params(l, p):
    row = lambda g: g[l].reshape(1, -1)
    w_in = p["w_in"][l]
    base = 4 * RET_W + Q_LORA + KV_LORA
    kr_cols = jnp.concatenate(
        [jnp.zeros((D_MODEL, MLA_NOPE), F32), _swapped_rope_cols(w_in[:, base:])], axis=-1)
    win = jnp.concatenate([w_in[:, :base], kr_cols], axis=-1).astype(BF16)

    dq = MLA_NOPE + MLA_ROPE
    wuq = p["w_uq"][l].reshape(Q_LORA, MLA_HEADS, dq)
    wuq = jnp.concatenate([wuq[..., :MLA_NOPE], _swapped_rope_cols(wuq[..., MLA_NOPE:])], axis=-1)
    wuq = wuq.reshape(Q_LORA, MLA_HEADS * LANES).astype(BF16)
    wukv = p["w_ukv"][l].reshape(KV_LORA, MLA_HEADS, MLA_NOPE + MLA_V)
    wk = jnp.concatenate([wukv[..., :MLA_NOPE], jnp.zeros_like(wukv[..., :MLA_NOPE])], axis=-1)
    wk = wk.reshape(KV_LORA, MLA_HEADS * LANES).astype(BF16)
    wv = jnp.concatenate([wukv[..., MLA_NOPE:], jnp.zeros_like(wukv[..., :LANES - MLA_V])], axis=-1)
    wv = wv.reshape(KV_LORA, MLA_HEADS * LANES).astype(BF16)

    scale = dq ** -0.5
    zeros64 = jnp.zeros((MLA_NOPE,), F32)
    gq = jnp.concatenate([p["qn_nope"][l], _swapped_rope_cols(p["qn_rope"][l])]) * scale
    gk = jnp.concatenate([p["kn_nope"][l], zeros64])
    gkr = jnp.concatenate([zeros64, _swapped_rope_cols(p["kn_rope"][l])])
    one_lane = jnp.zeros((LANES,), F32).at[MLA_V].set(1.0)
    vecs = jnp.stack([jnp.tile(r, MLA_HEADS) for r in (gq, gk, gkr, one_lane)]
                     + [jnp.zeros((MLA_HEADS * LANES,), F32)] * 4)
    return dict(
        ffn1_gain=row(p["ffn1_norm"]),
        ffn2_gain=row(p["ffn2_norm"]),
        inproj=(row(p["mix_norm"]), win, row(p["q_lat_norm"]), wuq, row(p["kv_lat_norm"]), wk, wv),
        vecs=vecs,
        gn=p["ret_head_norm"][l].reshape(RET_HEADS, 1, RET_DIM),
        wo=p["w_o"][l].astype(BF16),
    )


def _group_matrix():
    g = np.zeros((LANES, LANES), np.float32)
    g[:MLA_NOPE, :MLA_NOPE] = 1.0 / MLA_NOPE
    g[MLA_NOPE:MLA_NOPE + MLA_ROPE, MLA_NOPE:] = 1.0 / MLA_ROPE
    z = np.zeros_like(g)
    return jnp.asarray(np.block([[g, z], [z, g]]), dtype=BF16)


def kernel(x, positions, ffn1_norm, ffn1_w_gate, ffn1_w_up, ffn1_w_down, mix_norm, w_in, ret_head_norm, q_lat_norm, w_uq, kv_lat_norm, w_ukv, qn_nope, qn_rope, kn_nope, kn_rope, w_o, ffn2_norm, ffn2_w_gate, ffn2_w_up, ffn2_w_down):
    p = dict(ffn1_norm=ffn1_norm, ffn1_w_gate=ffn1_w_gate, ffn1_w_up=ffn1_w_up,
             ffn1_w_down=ffn1_w_down, mix_norm=mix_norm, w_in=w_in,
             ret_head_norm=ret_head_norm, q_lat_norm=q_lat_norm, w_uq=w_uq,
             kv_lat_norm=kv_lat_norm, w_ukv=w_ukv, qn_nope=qn_nope, qn_rope=qn_rope,
             kn_nope=kn_nope, kn_rope=kn_rope, w_o=w_o, ffn2_norm=ffn2_norm,
             ffn2_w_gate=ffn2_w_gate, ffn2_w_up=ffn2_w_up, ffn2_w_down=ffn2_w_down)
    batch, seq, d = x.shape
    depth = w_in.shape[0]
    dec = _retention_decay()
    gmat = _group_matrix()
    ffn1_stacks = [ffn1_w_gate, ffn1_w_up, ffn1_w_down]
    ffn2_stacks = [ffn2_w_gate, ffn2_w_up, ffn2_w_down]
    w_ffn1 = _cast_stacks(ffn1_stacks[:2], 1) + _cast_stacks(ffn1_stacks[2:], 1)
    tabs = None
    xf = x.reshape(batch * seq, d)
    for l in range(depth):
        lp = _layer_params(l, p)
        xf, new_tabs, w_ffn2 = _ffn(xf, lp["ffn1_gain"], *w_ffn1, side=(ffn2_stacks, l),
                                    positions=positions if l == 0 else None)
        tabs = new_tabs if l == 0 else tabs
        gmix, win, gq, wuq, gkv, wk, wv = lp["inproj"]
        rq, rk, rv, rg, qc, kc, v = _inproj(xf, tabs, gmix, win, gq, wuq, gkv, wk, wv,
                                            gmat, lp["vecs"])
        y_ret = _retention(rq, rk, rv, rg, dec, lp["gn"], batch, seq)
        y_mla = _attention(qc, kc, v, batch, seq)
        xf, _, w_ffn1 = _ffn(
            xf, lp["ffn2_gain"], *w_ffn2,
            proj=(y_ret.reshape(batch * seq, -1), y_mla.reshape(batch * seq, -1), lp["wo"]),
            side=(ffn1_stacks, l + 1) if l + 1 < depth else None)
    return xf.reshape(batch, seq, d)
```

```python
import functools

import jax
import jax.numpy as jnp
from jax import lax
from jax.experimental import pallas as pl
from jax.experimental.pallas import tpu as pltpu

D_MODEL = 1024
D_FF = 2816
RET_HEADS = 4
RET_DIM = 128
RET_CHUNK = 128
MLA_HEADS = 8
MLA_NOPE = 64
MLA_ROPE = 32
MLA_V = 64
Q_LORA = 256
KV_LORA = 128
ROPE_THETA = 10000.0
EPS = 1e-6

LANES = 128
BF16_ROWS = 16
RET_W = RET_HEADS * RET_DIM
IN_BASE = 4 * RET_W + Q_LORA + KV_LORA
IN_W = IN_BASE + LANES
HALF_ROPE = MLA_ROPE // 2
SLAB_SHIFT = LANES - MLA_ROPE

TOK_TILE = 512
FF_TILE = 256
CAST_ROWS = 256
ATT_TQ = 1024
ATT_HEADS = 4
RET_TILE = 1024
VMEM_LIMIT = 48 * 1024 * 1024

F32 = jnp.float32
BF16 = jnp.bfloat16


def _rms(x, gain):
    return x * lax.rsqrt(jnp.mean(x * x, axis=-1, keepdims=True) + EPS) * gain


def _resident(shape):
    nd = len(shape)
    return pl.BlockSpec(shape, lambda *_: (0,) * nd)


def _tables_rows(pos_ref, c_ref, o_ref, rows):
    ang = pos_ref[rows, :].astype(F32) * c_ref[0:1, :]
    c, s = jnp.cos(ang), jnp.sin(ang)
    low = lax.broadcasted_iota(jnp.int32, c.shape, 1) < RET_DIM // 2
    o_ref[0, rows] = jnp.where(low, c, pltpu.roll(c, RET_DIM // 2, 1))
    o_ref[1, rows] = jnp.where(low, s, pltpu.roll(s, RET_DIM // 2, 1)) * c_ref[1:2, :]
    o_ref[2, rows] = c_ref[2:3, :] + c_ref[3:4, :] * c
    o_ref[3, rows] = s * c_ref[4:5, :]


def _table_consts():
    inv_r = 1.0 / (ROPE_THETA ** (jnp.arange(0, RET_DIM, 2, dtype=F32) / RET_DIM))
    inv_m = 1.0 / (ROPE_THETA ** (jnp.arange(0, MLA_ROPE, 2, dtype=F32) / MLA_ROPE))
    half = RET_DIM // 2
    z = lambda n: jnp.zeros((n,), F32)
    o = lambda n: jnp.ones((n,), F32)
    return jnp.stack([
        jnp.concatenate([inv_r, inv_m, inv_m, z(MLA_ROPE)]),
        jnp.concatenate([-o(half), o(half)]),
        jnp.concatenate([o(MLA_NOPE), z(LANES - MLA_NOPE)]),
        jnp.concatenate([z(MLA_NOPE), o(MLA_ROPE), z(MLA_ROPE)]),
        jnp.concatenate([z(MLA_NOPE), -o(HALF_ROPE), o(HALF_ROPE), z(MLA_ROPE)]),
        z(LANES), z(LANES), z(LANES)])


def _ffn_body(*refs, with_proj, with_tables, n_side):
    refs = list(refs)
    take = lambda n: [refs.pop(0) for _ in range(n)]
    (x_ref,) = take(1)
    if with_proj:
        yr_ref, ym_ref, wo_ref = take(3)
    g_ref, wg_ref, wu_ref, wd_ref = take(4)
    if with_tables:
        pos_ref, c_ref = take(2)
    side_in = take(n_side)
    (o_ref,) = take(1)
    if with_tables:
        (tab_ref,) = take(1)
    side_out = take(n_side)
    (a_ref,) = take(1)

    x = x_ref[...]
    if with_proj:
        x = (x + jnp.dot(yr_ref[...], wo_ref[0:RET_W, :], preferred_element_type=F32)
             + jnp.dot(ym_ref[...], wo_ref[RET_W:, :], preferred_element_type=F32))
    h = _rms(x, g_ref[...]).astype(BF16)
    n_chunks = D_FF // FF_TILE
    tab_rows = x.shape[0] // n_chunks // 8 * 8
    for c in range(n_chunks):
        cols = slice(c * FF_TILE, (c + 1) * FF_TILE)
        g = jnp.dot(h, wg_ref[:, cols], preferred_element_type=F32)
        u = jnp.dot(h, wu_ref[:, cols], preferred_element_type=F32)
        a_ref[:, cols] = (g * jax.nn.sigmoid(g) * u).astype(BF16)
        if with_tables:
            stop = x.shape[0] if c == n_chunks - 1 else (c + 1) * tab_rows
            _tables_rows(pos_ref, c_ref, tab_ref, slice(c * tab_rows, stop))
    for src, dst in zip(side_in, side_out):
        dst[...] = src[...].astype(BF16)
    y = jnp.dot(a_ref[...], wd_ref[...], preferred_element_type=F32)
    o_ref[...] = x + 0.5 * y


def _cast_body(*refs):
    n = len(refs) // 2
    for src, dst in zip(refs[:n], refs[n:]):
        dst[...] = src[...].astype(BF16)


def _cast_stacks(ws, layers):
    _, r, c = ws[0].shape
    tr = CAST_ROWS
    blk = pl.BlockSpec((1, tr, c), lambda l, i: (l, i, 0))
    return pl.pallas_call(
        _cast_body,
        grid=(layers, r // tr),
        in_specs=[blk] * len(ws),
        out_specs=[blk] * len(ws),
        out_shape=[jax.ShapeDtypeStruct((layers, r, c), BF16) for w in ws],
        compiler_params=pltpu.CompilerParams(
            dimension_semantics=("parallel", "parallel"), vmem_limit_bytes=VMEM_LIMIT),
        name="cast_weights",
    )(*ws)


def _weight_block(w):
    if w.ndim == 2:
        return _resident(w.shape)
    return pl.BlockSpec((None,) + w.shape[1:], lambda *_: (0, 0, 0))


def _side_cast_specs(w, l, steps):
    _, r, c = w.shape
    hold = 1
    while (r * hold) % (steps * BF16_ROWS) or steps % hold:
        hold += 1
    rows = r * hold // steps
    return (pl.BlockSpec((None, rows, c), lambda i: (l, i // hold, 0)),
            pl.BlockSpec((rows, c), lambda i: (i // hold, 0)),
            jax.ShapeDtypeStruct((r, c), BF16))


def _ffn(x, gain, wg, wu, wd, proj=None, positions=None, side=None):
    t = x.shape[0]
    tm = TOK_TILE
    steps = t // tm
    row = lambda w: pl.BlockSpec((tm, w), lambda i: (i, 0))
    ins, specs = [x], [row(D_MODEL)]
    if proj is not None:
        yr, ym, wo = proj
        ins += [yr, ym, wo]
        specs += [row(RET_W), row(MLA_HEADS * MLA_V), _resident(wo.shape)]
    ins += [gain, wg, wu, wd]
    specs += [_resident(gain.shape), _weight_block(wg), _weight_block(wu), _weight_block(wd)]
    out_specs = [row(D_MODEL)]
    out_shape = [jax.ShapeDtypeStruct((t, D_MODEL), F32)]
    if positions is not None:
        ins += [positions.reshape(t, 1), _table_consts()]
        specs += [pl.BlockSpec((tm, 1), lambda i: (i, 0)), _resident((8, LANES))]
        out_specs.append(pl.BlockSpec((4, tm, LANES), lambda i: (0, i, 0)))
        out_shape.append(jax.ShapeDtypeStruct((4, t, LANES), F32))
    side_ws, side_layer = side if side is not None else ([], 0)
    for w in side_ws:
        in_spec, out_spec, shape = _side_cast_specs(w, side_layer, steps)
        ins.append(w)
        specs.append(in_spec)
        out_specs.append(out_spec)
        out_shape.append(shape)
    outs = pl.pallas_call(
        functools.partial(_ffn_body, with_proj=proj is not None,
                          with_tables=positions is not None, n_side=len(side_ws)),
        grid=(steps,),
        in_specs=specs,
        out_specs=out_specs,
        out_shape=out_shape,
        scratch_shapes=[pltpu.VMEM((tm, D_FF), BF16)],
        compiler_params=pltpu.CompilerParams(
            dimension_semantics=("arbitrary",), vmem_limit_bytes=VMEM_LIMIT),
        name="ffn_proj" if proj is not None else "ffn",
    )(*ins)
    tabs = outs[1] if positions is not None else None
    return outs[0], tabs, list(outs[len(outs) - len(side_ws):])


def _group_mean(x, gmat):
    x2 = x * x
    hi = x2.astype(BF16)
    lo = (x2 - hi.astype(F32)).astype(BF16)
    return (jnp.dot(hi, gmat, preferred_element_type=F32)
            + jnp.dot(lo, gmat, preferred_element_type=F32))


def _inproj_body(x_ref, gmix_ref, win_ref, tab_ref, gq_ref, wuq_ref, gkv_ref, wk_ref, wv_ref,
                 gmat_ref, vec_ref,
                 rq_ref, rk_ref, rv_ref, rg_ref, qc_ref, kc_ref, v_ref):
    h = _rms(x_ref[...], gmix_ref[...]).astype(BF16)
    proj = jnp.dot(h, win_ref[...], preferred_element_type=F32)
    cos_r, sin_r, cos_m, sin_m = tab_ref[0], tab_ref[1], tab_ref[2], tab_ref[3]
    ret_scale = RET_DIM ** -0.5
    for hd in range(RET_HEADS):
        sl = slice(hd * RET_DIM, (hd + 1) * RET_DIM)
        xq = proj[:, sl]
        rq_ref[:, sl] = (xq * cos_r + pltpu.roll(xq, RET_DIM // 2, 1) * sin_r).astype(BF16)
        xk = proj[:, RET_W + hd * RET_DIM:RET_W + (hd + 1) * RET_DIM]
        rk_ref[:, sl] = ((xk * cos_r + pltpu.roll(xk, RET_DIM // 2, 1) * sin_r)
                         * ret_scale).astype(BF16)
    rv_ref[...] = proj[:, 2 * RET_W:3 * RET_W].astype(BF16)
    rg_ref[...] = proj[:, 3 * RET_W:4 * RET_W]
    off = 4 * RET_W
    c_q = proj[:, off:off + Q_LORA]
    c_kv = proj[:, off + Q_LORA:off + Q_LORA + KV_LORA]
    kr = proj[:, off + Q_LORA + KV_LORA:]
    cqn = _rms(c_q, gq_ref[...]).astype(BF16)
    ckvn = _rms(c_kv, gkv_ref[...]).astype(BF16)
    qraw = jnp.dot(cqn, wuq_ref[...], preferred_element_type=F32)
    kraw = jnp.dot(ckvn, wk_ref[...], preferred_element_type=F32)
    v_ref[...] = (jnp.dot(ckvn, wv_ref[...], preferred_element_type=F32)
                  + vec_ref[3:4, :]).astype(BF16)

    lane = lax.broadcasted_iota(jnp.int32, (1, LANES), 1)
    in_rope = (lane >= MLA_NOPE) & (lane < MLA_NOPE + MLA_ROPE)
    ss = jnp.sum(jnp.where(in_rope, kr * kr, 0.0), axis=-1, keepdims=True) * (1.0 / MLA_ROPE)
    krn = kr * lax.rsqrt(ss + EPS) * vec_ref[2:3, 0:LANES]
    kro = krn * cos_m + pltpu.roll(krn, SLAB_SHIFT, 1) * sin_m

    gmat = gmat_ref[...]
    for pr in range(MLA_HEADS // 2):
        cols = slice(pr * 2 * LANES, (pr + 1) * 2 * LANES)
        xs = qraw[:, cols]
        y = xs * lax.rsqrt(_group_mean(xs, gmat) + EPS) * vec_ref[0:1, 0:2 * LANES]
        for e in range(2):
            sl = slice(e * LANES, (e + 1) * LANES)
            dst = slice((2 * pr + e) * LANES, (2 * pr + e + 1) * LANES)
            ys = y[:, sl]
            qc_ref[:, dst] = (ys * cos_m + pltpu.roll(ys, SLAB_SHIFT, 1) * sin_m).astype(BF16)
            ks = kraw[:, dst]
            ms = jnp.sum(ks * ks, axis=-1, keepdims=True) * (1.0 / MLA_NOPE)
            kc_ref[:, dst] = (ks * lax.rsqrt(ms + EPS) * vec_ref[1:2, 0:LANES] + kro).astype(BF16)


def _inproj(x, tabs, gmix, win, gq, wuq, gkv, wk, wv, gmat, vecs):
    t = x.shape[0]
    tm = TOK_TILE
    row = lambda w: pl.BlockSpec((tm, w), lambda i: (i, 0))
    slab_w = MLA_HEADS * LANES
    outs = [(RET_W, BF16), (RET_W, BF16), (RET_W, BF16), (RET_W, F32),
            (slab_w, BF16), (slab_w, BF16), (slab_w, BF16)]
    return pl.pallas_call(
        _inproj_body,
        grid=(t // tm,),
        in_specs=[row(D_MODEL), _resident(gmix.shape), _resident(win.shape),
                  pl.BlockSpec((4, tm, LANES), lambda i: (0, i, 0)),
                  _resident(gq.shape), _resident(wuq.shape), _resident(gkv.shape),
                  _resident(wk.shape), _resident(wv.shape), _resident(gmat.shape),
                  _resident(vecs.shape)],
        out_specs=[row(w) for w, _ in outs],
        out_shape=[jax.ShapeDtypeStruct((t, w), dt) for w, dt in outs],
        compiler_params=pltpu.CompilerParams(
            dimension_semantics=("parallel",), vmem_limit_bytes=VMEM_LIMIT),
        name="inproj",
    )(x, gmix, win, tabs, gq, wuq, gkv, wk, wv, gmat, vecs)


def _retention_body(q_ref, k_ref, v_ref, g_ref, dec_ref, gn_ref, o_ref,
                    state_ref, s_ref, kv_ref, prev_ref):
    @pl.when(pl.program_id(1) == 0)
    def _():
        state_ref[...] = jnp.zeros_like(state_ref)

    c = RET_CHUNK
    nt = (((1,), (1,)), ((), ()))
    n = RET_TILE // c
    blocks = [(hd, ci) for ci in range(n) for hd in range(RET_HEADS)]
    tile = lambda hd, ci: (slice(ci * c, (ci + 1) * c), slice(hd * RET_DIM, (hd + 1) * RET_DIM))

    for hd, ci in blocks:
        rows, cols = tile(hd, ci)
        k = k_ref[0, rows, cols]
        v = v_ref[0, rows, cols]
        s = lax.dot_general(q_ref[0, rows, cols], k, nt, preferred_element_type=F32)
        s_ref[hd * n + ci] = (s * dec_ref[0, hd]).astype(BF16)
        kz = (k.astype(F32) * dec_ref[1, hd]).astype(BF16)
        kv_ref[hd * n + ci] = jnp.dot(kz.T, v, preferred_element_type=F32)
    for hd in range(RET_HEADS):
        st = state_ref[hd]
        for ci in range(n):
            prev_ref[hd * n + ci] = st.astype(BF16)
            st = st * dec_ref[3, hd] + kv_ref[hd * n + ci]
        state_ref[hd] = st
    for hd, ci in blocks:
        rows, cols = tile(hd, ci)
        q = q_ref[0, rows, cols]
        y = (jnp.dot(s_ref[hd * n + ci], v_ref[0, rows, cols], preferred_element_type=F32)
             + jnp.dot(q, prev_ref[hd * n + ci], preferred_element_type=F32) * dec_ref[2, hd])
        g = g_ref[0, rows, cols]
        o_ref[0, rows, cols] = (_rms(y, gn_ref[hd]) * (g * jax.nn.sigmoid(g))).astype(BF16)


def _retention(rq, rk, rv, rg, dec, gn, batch, seq):
    shp = (batch, seq, RET_W)
    nblk = RET_HEADS * (RET_TILE // RET_CHUNK)
    blk = pl.BlockSpec((1, RET_TILE, RET_W), lambda b, s: (b, s, 0))
    return pl.pallas_call(
        _retention_body,
        grid=(batch, seq // RET_TILE),
        in_specs=[blk, blk, blk, blk, _resident(dec.shape), _resident(gn.shape)],
        out_specs=blk,
        out_shape=jax.ShapeDtypeStruct(shp, BF16),
        scratch_shapes=[pltpu.VMEM((RET_HEADS, RET_DIM, RET_DIM), F32),
                        pltpu.VMEM((nblk, RET_CHUNK, RET_CHUNK), BF16),
                        pltpu.VMEM((nblk, RET_DIM, RET_DIM), F32),
                        pltpu.VMEM((nblk, RET_DIM, RET_DIM), BF16)],
        compiler_params=pltpu.CompilerParams(
            dimension_semantics=("parallel", "arbitrary"), vmem_limit_bytes=VMEM_LIMIT),
        name="retention",
    )(rq.reshape(shp), rk.reshape(shp), rv.reshape(shp), rg.reshape(shp), dec, gn)


def _attention_body(q_ref, k_ref, v_ref, o_ref, m_ref, acc_ref):
    qi = pl.program_id(2)
    nt = (((1,), (1,)), ((), ()))
    half = ATT_TQ // 2
    m_ref[...] = jnp.full_like(m_ref, -1e30)
    acc_ref[...] = jnp.zeros_like(acc_ref)

    def steps(specs):
        cols = lambda e: slice(e * LANES, (e + 1) * LANES)
        keys = lambda start, width: pl.ds(pl.multiple_of(start, half), width)
        scores = [lax.dot_general(q_ref[0, q_rows, cols(e)], k_ref[0, keys(start, width), cols(e)],
                                  nt, preferred_element_type=F32)
                  for e, q_rows, start, width, _ in specs]
        probs = []
        for (e, q_rows, _, width, mask_from), s in zip(specs, scores):
            chunks = [s[:, c * LANES:(c + 1) * LANES] for c in range(width // LANES)]
            if mask_from is not None:
                r = lax.broadcasted_iota(jnp.int32, chunks[0].shape, 0)
                lane = lax.broadcasted_iota(jnp.int32, chunks[0].shape, 1)
                for c in range(mask_from, len(chunks)):
                    chunks[c] = jnp.where(r >= lane + (c - mask_from) * LANES, chunks[c], -1e30)
            m_old = m_ref[e, q_rows]
            m_new = jnp.maximum(
                m_old, jnp.max(functools.reduce(jnp.maximum, chunks), axis=-1, keepdims=True))
            m_ref[e, q_rows] = m_new
            probs.append((jnp.exp(m_old - m_new), jnp.concatenate(
                [jnp.exp(ch - m_new).astype(BF16) for ch in chunks], axis=1)))
        for (e, q_rows, start, width, _), (alpha, p) in zip(specs, probs):
            acc_ref[e, q_rows] = alpha * acc_ref[e, q_rows] + jnp.dot(
                p, v_ref[0, keys(start, width), cols(e)], preferred_element_type=F32)

    def loop_body(kb, carry):
        for e in range(ATT_HEADS):
            steps([(e, slice(0, ATT_TQ), kb * ATT_TQ, ATT_TQ, None)])
        return carry

    lax.fori_loop(0, qi, loop_body, 0)
    diag = qi * ATT_TQ
    steps([spec for e in range(ATT_HEADS) for spec in (
        (e, slice(0, half), diag, half, 0),
        (e, slice(half, ATT_TQ), diag, ATT_TQ, half // LANES))])
    lane = lax.broadcasted_iota(jnp.int32, (ATT_TQ, LANES), 1)
    for pr in range(ATT_HEADS // 2):
        o0, o1 = [acc * (1.0 / acc[:, MLA_V:MLA_V + 1])
                  for acc in (acc_ref[2 * pr], acc_ref[2 * pr + 1])]
        o_ref[0, :, pr * LANES:(pr + 1) * LANES] = jnp.where(
            lane < MLA_V, o0, pltpu.roll(o1, MLA_V, 1)).astype(BF16)


def _attention(qc, kc, vc, batch, seq):
    slab_w = MLA_HEADS * LANES
    grp_w = ATT_HEADS * LANES
    out_w = ATT_HEADS * MLA_V
    seq_blk = pl.BlockSpec((1, seq, grp_w), lambda b, j, i: (b, 0, j))
    return pl.pallas_call(
        _attention_body,
        grid=(batch, MLA_HEADS // ATT_HEADS, seq // ATT_TQ),
        in_specs=[pl.BlockSpec((1, ATT_TQ, grp_w), lambda b, j, i: (b, i, j)),
                  seq_blk, seq_blk],
        out_specs=pl.BlockSpec((1, ATT_TQ, out_w), lambda b, j, i: (b, i, j)),
        out_shape=jax.ShapeDtypeStruct((batch, seq, MLA_HEADS * MLA_V), BF16),
        scratch_shapes=[pltpu.VMEM((ATT_HEADS, ATT_TQ, LANES), F32),
                        pltpu.VMEM((ATT_HEADS, ATT_TQ, LANES), F32)],
        compiler_params=pltpu.CompilerParams(
            dimension_semantics=("parallel", "parallel", "arbitrary"),
            vmem_limit_bytes=VMEM_LIMIT),
        name="latent_attention",
    )(qc.reshape(batch, seq, slab_w), kc.reshape(batch, seq, slab_w),
      vc.reshape(batch, seq, slab_w))


def _retention_decay():
    c = RET_CHUNK
    lg = jnp.log(1.0 - 2.0 ** (-5.0 - jnp.arange(RET_HEADS, dtype=F32)))
    idx = jnp.arange(c, dtype=F32)
    rel = idx[:, None] - idx[None, :]
    decay = jnp.where(rel >= 0, jnp.exp(jnp.maximum(rel, 0.0)[None] * lg[:, None, None]), 0.0)
    zeta = jnp.exp((c - 1 - idx)[None, :] * lg[:, None])
    xi = jnp.exp((idx + 1.0)[None, :] * lg[:, None])
    cdec = jnp.exp(c * lg)
    full = (RET_HEADS, c, RET_DIM)
    return jnp.stack([decay,
                      jnp.broadcast_to(zeta[:, :, None], full),
                      jnp.broadcast_to(xi[:, :, None], full),
                      jnp.broadcast_to(cdec[:, None, None], full)])


def _swapped_rope_cols(w):
    a, b = w[..., :HALF_ROPE], w[..., HALF_ROPE:]
    return jnp.concatenate([a, b, b, a], axis=-1)


def _layer_params(l, p):
    row = lambda g: g[l].reshape(1, -1)
    kr_cols = jnp.concatenate(
        [jnp.zeros((D_MODEL, MLA_NOPE), F32),
         _swapped_rope_cols(p["w_in"][l][:, IN_BASE:])], axis=-1).astype(BF16)

    dq = MLA_NOPE + MLA_ROPE
    wuq = p["w_uq"][l].reshape(Q_LORA, MLA_HEADS, dq)
    wuq = jnp.concatenate([wuq[..., :MLA_NOPE], _swapped_rope_cols(wuq[..., MLA_NOPE:])], axis=-1)
    wuq = wuq.reshape(Q_LORA, MLA_HEADS * LANES).astype(BF16)
    wukv = p["w_ukv"][l].reshape(KV_LORA, MLA_HEADS, MLA_NOPE + MLA_V)
    wk = jnp.concatenate([wukv[..., :MLA_NOPE], jnp.zeros_like(wukv[..., :MLA_NOPE])], axis=-1)
    wk = wk.reshape(KV_LORA, MLA_HEADS * LANES).astype(BF16)
    wv = jnp.concatenate([wukv[..., MLA_NOPE:], jnp.zeros_like(wukv[..., :LANES - MLA_V])], axis=-1)
    wv = wv.reshape(KV_LORA, MLA_HEADS * LANES).astype(BF16)

    scale = dq ** -0.5
    zeros64 = jnp.zeros((MLA_NOPE,), F32)
    gq = jnp.concatenate([p["qn_nope"][l], _swapped_rope_cols(p["qn_rope"][l])]) * scale
    gk = jnp.concatenate([p["kn_nope"][l], zeros64])
    gkr = jnp.concatenate([zeros64, _swapped_rope_cols(p["kn_rope"][l])])
    one_lane = jnp.zeros((LANES,), F32).at[MLA_V].set(1.0)
    vecs = jnp.stack([jnp.tile(r, MLA_HEADS) for r in (gq, gk, gkr, one_lane)]
                     + [jnp.zeros((MLA_HEADS * LANES,), F32)] * 4)
    return dict(
        ffn1_gain=row(p["ffn1_norm"]),
        ffn2_gain=row(p["ffn2_norm"]),
        inproj=(row(p["mix_norm"]), kr_cols, row(p["q_lat_norm"]), wuq, row(p["kv_lat_norm"]),
                wk, wv),
        vecs=vecs,
        gn=p["ret_head_norm"][l].reshape(RET_HEADS, 1, RET_DIM),
    )


def _group_matrix():
    idx = jnp.arange(2 * LANES)
    slab, lane = idx // LANES, idx % LANES
    same = slab[:, None] == slab[None, :]
    nope = (lane[:, None] < MLA_NOPE) & (lane[None, :] < MLA_NOPE)
    rope = ((lane[:, None] >= MLA_NOPE) & (lane[:, None] < MLA_NOPE + MLA_ROPE)
            & (lane[None, :] >= MLA_NOPE))
    g = jnp.where(nope, 1.0 / MLA_NOPE, jnp.where(rope, 1.0 / MLA_ROPE, 0.0))
    return jnp.where(same, g, 0.0).astype(BF16)


def kernel(x, positions, ffn1_norm, ffn1_w_gate, ffn1_w_up, ffn1_w_down, mix_norm, w_in, ret_head_norm, q_lat_norm, w_uq, kv_lat_norm, w_ukv, qn_nope, qn_rope, kn_nope, kn_rope, w_o, ffn2_norm, ffn2_w_gate, ffn2_w_up, ffn2_w_down):
    p = dict(ffn1_norm=ffn1_norm, ffn1_w_gate=ffn1_w_gate, ffn1_w_up=ffn1_w_up,
             ffn1_w_down=ffn1_w_down, mix_norm=mix_norm, w_in=w_in,
             ret_head_norm=ret_head_norm, q_lat_norm=q_lat_norm, w_uq=w_uq,
             kv_lat_norm=kv_lat_norm, w_ukv=w_ukv, qn_nope=qn_nope, qn_rope=qn_rope,
             kn_nope=kn_nope, kn_rope=kn_rope, w_o=w_o, ffn2_norm=ffn2_norm,
             ffn2_w_gate=ffn2_w_gate, ffn2_w_up=ffn2_w_up, ffn2_w_down=ffn2_w_down)
    batch, seq, d = x.shape
    depth = w_in.shape[0]
    dec = _retention_decay()
    gmat = _group_matrix()
    ffn1_stacks = [ffn1_w_gate, ffn1_w_up, ffn1_w_down]
    ffn2_stacks = [ffn2_w_gate, ffn2_w_up, ffn2_w_down]
    w_ffn1 = _cast_stacks(ffn1_stacks[:2], 1) + _cast_stacks(ffn1_stacks[2:], 1)
    tabs = None
    xf = x.reshape(batch * seq, d)
    for l in range(depth):
        lp = _layer_params(l, p)
        xf, new_tabs, side_w = _ffn(xf, lp["ffn1_gain"], *w_ffn1,
                                    side=(ffn2_stacks + [w_in, w_o], l),
                                    positions=positions if l == 0 else None)
        tabs = new_tabs if l == 0 else tabs
        w_ffn2, (win_main, wo) = side_w[:3], side_w[3:]
        gmix, kr_cols, gq, wuq, gkv, wk, wv = lp["inproj"]
        win = jnp.concatenate([win_main[:, :IN_BASE], kr_cols], axis=-1)
        rq, rk, rv, rg, qc, kc, v = _inproj(xf, tabs, gmix, win, gq, wuq, gkv, wk, wv,
                                            gmat, lp["vecs"])
        y_ret = _retention(rq, rk, rv, rg, dec, lp["gn"], batch, seq)
        y_mla = _attention(qc, kc, v, batch, seq)
        xf, _, w_ffn1 = _ffn(
            xf, lp["ffn2_gain"], *w_ffn2,
            proj=(y_ret.reshape(batch * seq, -1), y_mla.reshape(batch * seq, -1), wo),
            side=(ffn1_stacks, l + 1) if l + 1 < depth else None)
    return xf.reshape(batch, seq, d)
```

```python
import functools

import jax
import jax.numpy as jnp
from jax import lax
from jax.experimental import pallas as pl
from jax.experimental.pallas import tpu as pltpu

D_MODEL = 1024
D_FF = 2816
RET_HEADS = 4
RET_DIM = 128
RET_CHUNK = 128
MLA_HEADS = 8
MLA_NOPE = 64
MLA_ROPE = 32
MLA_V = 64
Q_LORA = 256
KV_LORA = 128
ROPE_THETA = 10000.0
EPS = 1e-6

LANES = 128
BF16_ROWS = 16
RET_W = RET_HEADS * RET_DIM
IN_BASE = 4 * RET_W + Q_LORA + KV_LORA
IN_W = IN_BASE + LANES
HALF_ROPE = MLA_ROPE // 2
SLAB_SHIFT = LANES - MLA_ROPE

TOK_TILE = 512
FF_TILE = 256
CAST_ROWS = 256
ATT_TQ = 1024
ATT_HEADS = 4
RET_TILE = 1024
VMEM_LIMIT = 48 * 1024 * 1024

F32 = jnp.float32
BF16 = jnp.bfloat16


def _rms(x, gain):
    return x * lax.rsqrt(jnp.mean(x * x, axis=-1, keepdims=True) + EPS) * gain


def _resident(shape):
    nd = len(shape)
    return pl.BlockSpec(shape, lambda *_: (0,) * nd)


def _tables_rows(pos_ref, c_ref, o_ref, rows):
    ang = pos_ref[rows, :].astype(F32) * c_ref[0:1, :]
    c, s = jnp.cos(ang), jnp.sin(ang)
    low = lax.broadcasted_iota(jnp.int32, c.shape, 1) < RET_DIM // 2
    o_ref[0, rows] = jnp.where(low, c, pltpu.roll(c, RET_DIM // 2, 1))
    o_ref[1, rows] = jnp.where(low, s, pltpu.roll(s, RET_DIM // 2, 1)) * c_ref[1:2, :]
    o_ref[2, rows] = c_ref[2:3, :] + c_ref[3:4, :] * c
    o_ref[3, rows] = s * c_ref[4:5, :]


def _table_consts():
    inv_r = 1.0 / (ROPE_THETA ** (jnp.arange(0, RET_DIM, 2, dtype=F32) / RET_DIM))
    inv_m = 1.0 / (ROPE_THETA ** (jnp.arange(0, MLA_ROPE, 2, dtype=F32) / MLA_ROPE))
    half = RET_DIM // 2
    z = lambda n: jnp.zeros((n,), F32)
    o = lambda n: jnp.ones((n,), F32)
    return jnp.stack([
        jnp.concatenate([inv_r, inv_m, inv_m, z(MLA_ROPE)]),
        jnp.concatenate([-o(half), o(half)]),
        jnp.concatenate([o(MLA_NOPE), z(LANES - MLA_NOPE)]),
        jnp.concatenate([z(MLA_NOPE), o(MLA_ROPE), z(MLA_ROPE)]),
        jnp.concatenate([z(MLA_NOPE), -o(HALF_ROPE), o(HALF_ROPE), z(MLA_ROPE)]),
        z(LANES), z(LANES), z(LANES)])


def _ffn_body(*refs, with_proj, with_tables, n_side):
    refs = list(refs)
    take = lambda n: [refs.pop(0) for _ in range(n)]
    (x_ref,) = take(1)
    if with_proj:
        yr_ref, ym_ref, wo_ref = take(3)
    g_ref, wg_ref, wu_ref, wd_ref = take(4)
    if with_tables:
        pos_ref, c_ref = take(2)
    side_in = take(n_side)
    (o_ref,) = take(1)
    if with_tables:
        (tab_ref,) = take(1)
    side_out = take(n_side)
    (a_ref,) = take(1)

    x = x_ref[...]
    if with_proj:
        x = (x + jnp.dot(yr_ref[...], wo_ref[0:RET_W, :], preferred_element_type=F32)
             + jnp.dot(ym_ref[...], wo_ref[RET_W:, :], preferred_element_type=F32))
    h = _rms(x, g_ref[...]).astype(BF16)
    n_chunks = D_FF // FF_TILE
    tab_rows = x.shape[0] // n_chunks // 8 * 8
    for c in range(n_chunks):
        cols = slice(c * FF_TILE, (c + 1) * FF_TILE)
        g = jnp.dot(h, wg_ref[:, cols], preferred_element_type=F32)
        u = jnp.dot(h, wu_ref[:, cols], preferred_element_type=F32)
        a_ref[:, cols] = (g * jax.nn.sigmoid(g) * u).astype(BF16)
        if with_tables:
            stop = x.shape[0] if c == n_chunks - 1 else (c + 1) * tab_rows
            _tables_rows(pos_ref, c_ref, tab_ref, slice(c * tab_rows, stop))
    for src, dst in zip(side_in, side_out):
        dst[...] = src[...].astype(BF16)
    y = jnp.dot(a_ref[...], wd_ref[...], preferred_element_type=F32)
    o_ref[...] = x + 0.5 * y


def _cast_body(*refs):
    n = len(refs) // 2
    for src, dst in zip(refs[:n], refs[n:]):
        dst[...] = src[...].astype(BF16)


def _cast_stacks(ws, layers):
    _, r, c = ws[0].shape
    tr = CAST_ROWS
    blk = pl.BlockSpec((1, tr, c), lambda l, i: (l, i, 0))
    return pl.pallas_call(
        _cast_body,
        grid=(layers, r // tr),
        in_specs=[blk] * len(ws),
        out_specs=[blk] * len(ws),
        out_shape=[jax.ShapeDtypeStruct((layers, r, c), BF16) for w in ws],
        compiler_params=pltpu.CompilerParams(
            dimension_semantics=("parallel", "parallel"), vmem_limit_bytes=VMEM_LIMIT),
        name="cast_weights",
    )(*ws)


def _weight_block(w):
    if w.ndim == 2:
        return _resident(w.shape)
    return pl.BlockSpec((None,) + w.shape[1:], lambda *_: (0, 0, 0))


def _side_cast_specs(w, l, steps):
    _, r, c = w.shape
    hold = 1
    while (r * hold) % (steps * BF16_ROWS) or steps % hold:
        hold += 1
    rows = r * hold // steps
    return (pl.BlockSpec((None, rows, c), lambda i: (l, i // hold, 0)),
            pl.BlockSpec((rows, c), lambda i: (i // hold, 0)),
            jax.ShapeDtypeStruct((r, c), BF16))


def _ffn(x, gain, wg, wu, wd, proj=None, positions=None, side=None):
    t = x.shape[0]
    tm = TOK_TILE
    steps = t // tm
    row = lambda w: pl.BlockSpec((tm, w), lambda i: (i, 0))
    ins, specs = [x], [row(D_MODEL)]
    if proj is not None:
        yr, ym, wo = proj
        ins += [yr, ym, wo]
        specs += [row(RET_W), row(MLA_HEADS * MLA_V), _resident(wo.shape)]
    ins += [gain, wg, wu, wd]
    specs += [_resident(gain.shape), _weight_block(wg), _weight_block(wu), _weight_block(wd)]
    out_specs = [row(D_MODEL)]
    out_shape = [jax.ShapeDtypeStruct((t, D_MODEL), F32)]
    if positions is not None:
        ins += [positions.reshape(t, 1), _table_consts()]
        specs += [pl.BlockSpec((tm, 1), lambda i: (i, 0)), _resident((8, LANES))]
        out_specs.append(pl.BlockSpec((4, tm, LANES), lambda i: (0, i, 0)))
        out_shape.append(jax.ShapeDtypeStruct((4, t, LANES), F32))
    side_ws, side_layer = side if side is not None else ([], 0)
    for w in side_ws:
        in_spec, out_spec, shape = _side_cast_specs(w, side_layer, steps)
        ins.append(w)
        specs.append(in_spec)
        out_specs.append(out_spec)
        out_shape.append(shape)
    outs = pl.pallas_call(
        functools.partial(_ffn_body, with_proj=proj is not None,
                          with_tables=positions is not None, n_side=len(side_ws)),
        grid=(steps,),
        in_specs=specs,
        out_specs=out_specs,
        out_shape=out_shape,
        scratch_shapes=[pltpu.VMEM((tm, D_FF), BF16)],
        compiler_params=pltpu.CompilerParams(
            dimension_semantics=("arbitrary",), vmem_limit_bytes=VMEM_LIMIT),
        name="ffn_proj" if proj is not None else "ffn",
    )(*ins)
    tabs = outs[1] if positions is not None else None
    return outs[0], tabs, list(outs[len(outs) - len(side_ws):])


def _group_mean(x, gmat):
    x2 = x * x
    hi = x2.astype(BF16)
    lo = (x2 - hi.astype(F32)).astype(BF16)
    return (jnp.dot(hi, gmat, preferred_element_type=F32)
            + jnp.dot(lo, gmat, preferred_element_type=F32))


def _inproj_body(x_ref, gmix_ref, win_ref, krc_ref, tab_ref, gq_ref, wuq_ref, gkv_ref, wk_ref,
                 wv_ref, gmat_ref, vec_ref,
                 rq_ref, rk_ref, rv_ref, rg_ref, qc_ref, kc_ref, v_ref, wfull_ref):
    @pl.when(pl.program_id(0) == 0)
    def _():
        wfull_ref[:, 0:IN_BASE] = win_ref[...]
        wfull_ref[:, IN_BASE:] = krc_ref[...]

    h = _rms(x_ref[...], gmix_ref[...]).astype(BF16)
    proj = jnp.dot(h, wfull_ref[...], preferred_element_type=F32)
    cos_r, sin_r, cos_m, sin_m = tab_ref[0], tab_ref[1], tab_ref[2], tab_ref[3]
    ret_scale = RET_DIM ** -0.5
    for hd in range(RET_HEADS):
        sl = slice(hd * RET_DIM, (hd + 1) * RET_DIM)
        xq = proj[:, sl]
        rq_ref[:, sl] = (xq * cos_r + pltpu.roll(xq, RET_DIM // 2, 1) * sin_r).astype(BF16)
        xk = proj[:, RET_W + hd * RET_DIM:RET_W + (hd + 1) * RET_DIM]
        rk_ref[:, sl] = ((xk * cos_r + pltpu.roll(xk, RET_DIM // 2, 1) * sin_r)
                         * ret_scale).astype(BF16)
    rv_ref[...] = proj[:, 2 * RET_W:3 * RET_W].astype(BF16)
    rg_ref[...] = proj[:, 3 * RET_W:4 * RET_W]
    off = 4 * RET_W
    c_q = proj[:, off:off + Q_LORA]
    c_kv = proj[:, off + Q_LORA:off + Q_LORA + KV_LORA]
    kr = proj[:, off + Q_LORA + KV_LORA:]
    cqn = _rms(c_q, gq_ref[...]).astype(BF16)
    ckvn = _rms(c_kv, gkv_ref[...]).astype(BF16)
    qraw = jnp.dot(cqn, wuq_ref[...], preferred_element_type=F32)
    kraw = jnp.dot(ckvn, wk_ref[...], preferred_element_type=F32)
    v_ref[...] = (jnp.dot(ckvn, wv_ref[...], preferred_element_type=F32)
                  + vec_ref[3:4, :]).astype(BF16)

    lane = lax.broadcasted_iota(jnp.int32, (1, LANES), 1)
    in_rope = (lane >= MLA_NOPE) & (lane < MLA_NOPE + MLA_ROPE)
    ss = jnp.sum(jnp.where(in_rope, kr * kr, 0.0), axis=-1, keepdims=True) * (1.0 / MLA_ROPE)
    krn = kr * lax.rsqrt(ss + EPS) * vec_ref[2:3, 0:LANES]
    kro = krn * cos_m + pltpu.roll(krn, SLAB_SHIFT, 1) * sin_m

    gmat = gmat_ref[...]
    for pr in range(MLA_HEADS // 2):
        cols = slice(pr * 2 * LANES, (pr + 1) * 2 * LANES)
        xs = qraw[:, cols]
        y = xs * lax.rsqrt(_group_mean(xs, gmat) + EPS) * vec_ref[0:1, 0:2 * LANES]
        for e in range(2):
            sl = slice(e * LANES, (e + 1) * LANES)
            dst = slice((2 * pr + e) * LANES, (2 * pr + e + 1) * LANES)
            ys = y[:, sl]
            qc_ref[:, dst] = (ys * cos_m + pltpu.roll(ys, SLAB_SHIFT, 1) * sin_m).astype(BF16)
            ks = kraw[:, dst]
            ms = jnp.sum(ks * ks, axis=-1, keepdims=True) * (1.0 / MLA_NOPE)
            kc_ref[:, dst] = (ks * lax.rsqrt(ms + EPS) * vec_ref[1:2, 0:LANES] + kro).astype(BF16)


def _inproj(x, tabs, gmix, win, l, krc, gq, wuq, gkv, wk, wv, gmat, vecs):
    t = x.shape[0]
    tm = TOK_TILE
    row = lambda w: pl.BlockSpec((tm, w), lambda i: (i, 0))
    slab_w = MLA_HEADS * LANES
    outs = [(RET_W, BF16), (RET_W, BF16), (RET_W, BF16), (RET_W, F32),
            (slab_w, BF16), (slab_w, BF16), (slab_w, BF16)]
    return pl.pallas_call(
        _inproj_body,
        grid=(t // tm,),
        in_specs=[row(D_MODEL), _resident(gmix.shape),
                  pl.BlockSpec((None,) + win.shape[1:], lambda i: (l, 0, 0)),
                  _resident(krc.shape),
                  pl.BlockSpec((4, tm, LANES), lambda i: (0, i, 0)),
                  _resident(gq.shape), _resident(wuq.shape), _resident(gkv.shape),
                  _resident(wk.shape), _resident(wv.shape), _resident(gmat.shape),
                  _resident(vecs.shape)],
        out_specs=[row(w) for w, _ in outs],
        out_shape=[jax.ShapeDtypeStruct((t, w), dt) for w, dt in outs],
        scratch_shapes=[pltpu.VMEM((D_MODEL, IN_W), BF16)],
        compiler_params=pltpu.CompilerParams(
            dimension_semantics=("arbitrary",), vmem_limit_bytes=VMEM_LIMIT),
        name="inproj",
    )(x, gmix, win, krc, tabs, gq, wuq, gkv, wk, wv, gmat, vecs)


def _retention_body(q_ref, k_ref, v_ref, g_ref, dec_ref, gn_ref, o_ref,
                    state_ref, s_ref, kv_ref, prev_ref):
    @pl.when(pl.program_id(1) == 0)
    def _():
        state_ref[...] = jnp.zeros_like(state_ref)

    c = RET_CHUNK
    nt = (((1,), (1,)), ((), ()))
    n = RET_TILE // c
    blocks = [(hd, ci) for ci in range(n) for hd in range(RET_HEADS)]
    tile = lambda hd, ci: (slice(ci * c, (ci + 1) * c), slice(hd * RET_DIM, (hd + 1) * RET_DIM))

    for hd, ci in blocks:
        rows, cols = tile(hd, ci)
        k = k_ref[0, rows, cols]
        v = v_ref[0, rows, cols]
        s = lax.dot_general(q_ref[0, rows, cols], k, nt, preferred_element_type=F32)
        s_ref[hd * n + ci] = (s * dec_ref[0, hd]).astype(BF16)
        kz = (k.astype(F32) * dec_ref[1, hd]).astype(BF16)
        kv_ref[hd * n + ci] = jnp.dot(kz.T, v, preferred_element_type=F32)
    for hd in range(RET_HEADS):
        st = state_ref[hd]
        for ci in range(n):
            prev_ref[hd * n + ci] = st.astype(BF16)
            st = st * dec_ref[3, hd] + kv_ref[hd * n + ci]
        state_ref[hd] = st
    for hd, ci in blocks:
        rows, cols = tile(hd, ci)
        q = q_ref[0, rows, cols]
        y = (jnp.dot(s_ref[hd * n + ci], v_ref[0, rows, cols], preferred_element_type=F32)
             + jnp.dot(q, prev_ref[hd * n + ci], preferred_element_type=F32) * dec_ref[2, hd])
        g = g_ref[0, rows, cols]
        o_ref[0, rows, cols] = (_rms(y, gn_ref[hd]) * (g * jax.nn.sigmoid(g))).astype(BF16)


def _retention(rq, rk, rv, rg, dec, gn, batch, seq):
    shp = (batch, seq, RET_W)
    nblk = RET_HEADS * (RET_TILE // RET_CHUNK)
    blk = pl.BlockSpec((1, RET_TILE, RET_W), lambda b, s: (b, s, 0))
    return pl.pallas_call(
        _retention_body,
        grid=(batch, seq // RET_TILE),
        in_specs=[blk, blk, blk, blk, _resident(dec.shape), _resident(gn.shape)],
        out_specs=blk,
        out_shape=jax.ShapeDtypeStruct(shp, BF16),
        scratch_shapes=[pltpu.VMEM((RET_HEADS, RET_DIM, RET_DIM), F32),
                        pltpu.VMEM((nblk, RET_CHUNK, RET_CHUNK), BF16),
                        pltpu.VMEM((nblk, RET_DIM, RET_DIM), F32),
                        pltpu.VMEM((nblk, RET_DIM, RET_DIM), BF16)],
        compiler_params=pltpu.CompilerParams(
            dimension_semantics=("parallel", "arbitrary"), vmem_limit_bytes=VMEM_LIMIT),
        name="retention",
    )(rq.reshape(shp), rk.reshape(shp), rv.reshape(shp), rg.reshape(shp), dec, gn)


def _attention_body(q_ref, k_ref, v_ref, o_ref, m_ref, acc_ref):
    qi = pl.program_id(2)
    nt = (((1,), (1,)), ((), ()))
    half = ATT_TQ // 2
    m_ref[...] = jnp.full_like(m_ref, -1e30)
    acc_ref[...] = jnp.zeros_like(acc_ref)

    def steps(specs):
        cols = lambda e: slice(e * LANES, (e + 1) * LANES)
        keys = lambda start, width: pl.ds(pl.multiple_of(start, half), width)
        scores = [lax.dot_general(q_ref[0, q_rows, cols(e)], k_ref[0, keys(start, width), cols(e)],
                                  nt, preferred_element_type=F32)
                  for e, q_rows, start, width, _ in specs]
        probs = []
        for (e, q_rows, _, width, mask_from), s in zip(specs, scores):
            chunks = [s[:, c * LANES:(c + 1) * LANES] for c in range(width // LANES)]
            if mask_from is not None:
                r = lax.broadcasted_iota(jnp.int32, chunks[0].shape, 0)
                lane = lax.broadcasted_iota(jnp.int32, chunks[0].shape, 1)
                for c in range(mask_from, len(chunks)):
                    chunks[c] = jnp.where(r >= lane + (c - mask_from) * LANES, chunks[c], -1e30)
            m_old = m_ref[e, q_rows]
            m_new = jnp.maximum(
                m_old, jnp.max(functools.reduce(jnp.maximum, chunks), axis=-1, keepdims=True))
            m_ref[e, q_rows] = m_new
            probs.append((jnp.exp(m_old - m_new), jnp.concatenate(
                [jnp.exp(ch - m_new).astype(BF16) for ch in chunks], axis=1)))
        for (e, q_rows, start, width, _), (alpha, p) in zip(specs, probs):
            acc_ref[e, q_rows] = alpha * acc_ref[e, q_rows] + jnp.dot(
                p, v_ref[0, keys(start, width), cols(e)], preferred_element_type=F32)

    def loop_body(kb, carry):
        for e in range(ATT_HEADS):
            steps([(e, slice(0, ATT_TQ), kb * ATT_TQ, ATT_TQ, None)])
        return carry

    lax.fori_loop(0, qi, loop_body, 0)
    diag = qi * ATT_TQ
    steps([spec for e in range(ATT_HEADS) for spec in (
        (e, slice(0, half), diag, half, 0),
        (e, slice(half, ATT_TQ), diag, ATT_TQ, half // LANES))])
    lane = lax.broadcasted_iota(jnp.int32, (ATT_TQ, LANES), 1)
    for pr in range(ATT_HEADS // 2):
        o0, o1 = [acc * (1.0 / acc[:, MLA_V:MLA_V + 1])
                  for acc in (acc_ref[2 * pr], acc_ref[2 * pr + 1])]
        o_ref[0, :, pr * LANES:(pr + 1) * LANES] = jnp.where(
            lane < MLA_V, o0, pltpu.roll(o1, MLA_V, 1)).astype(BF16)


def _attention(qc, kc, vc, batch, seq):
    slab_w = MLA_HEADS * LANES
    grp_w = ATT_HEADS * LANES
    out_w = ATT_HEADS * MLA_V
    seq_blk = pl.BlockSpec((1, seq, grp_w), lambda b, j, i: (b, 0, j))
    return pl.pallas_call(
        _attention_body,
        grid=(batch, MLA_HEADS // ATT_HEADS, seq // ATT_TQ),
        in_specs=[pl.BlockSpec((1, ATT_TQ, grp_w), lambda b, j, i: (b, i, j)),
                  seq_blk, seq_blk],
        out_specs=pl.BlockSpec((1, ATT_TQ, out_w), lambda b, j, i: (b, i, j)),
        out_shape=jax.ShapeDtypeStruct((batch, seq, MLA_HEADS * MLA_V), BF16),
        scratch_shapes=[pltpu.VMEM((ATT_HEADS, ATT_TQ, LANES), F32),
                        pltpu.VMEM((ATT_HEADS, ATT_TQ, LANES), F32)],
        compiler_params=pltpu.CompilerParams(
            dimension_semantics=("parallel", "parallel", "arbitrary"),
            vmem_limit_bytes=VMEM_LIMIT),
        name="latent_attention",
    )(qc.reshape(batch, seq, slab_w), kc.reshape(batch, seq, slab_w),
      vc.reshape(batch, seq, slab_w))


def _retention_decay():
    c = RET_CHUNK
    lg = jnp.log(1.0 - 2.0 ** (-5.0 - jnp.arange(RET_HEADS, dtype=F32)))
    idx = jnp.arange(c, dtype=F32)
    rel = idx[:, None] - idx[None, :]
    decay = jnp.where(rel >= 0, jnp.exp(jnp.maximum(rel, 0.0)[None] * lg[:, None, None]), 0.0)
    zeta = jnp.exp((c - 1 - idx)[None, :] * lg[:, None])
    xi = jnp.exp((idx + 1.0)[None, :] * lg[:, None])
    cdec = jnp.exp(c * lg)
    full = (RET_HEADS, c, RET_DIM)
    return jnp.stack([decay,
                      jnp.broadcast_to(zeta[:, :, None], full),
                      jnp.broadcast_to(xi[:, :, None], full),
                      jnp.broadcast_to(cdec[:, None, None], full)])


def _swapped_rope_cols(w):
    a, b = w[..., :HALF_ROPE], w[..., HALF_ROPE:]
    return jnp.concatenate([a, b, b, a], axis=-1)


def _layer_params(l, p):
    row = lambda g: g[l].reshape(1, -1)
    kr_cols = jnp.concatenate(
        [jnp.zeros((D_MODEL, MLA_NOPE), F32),
         _swapped_rope_cols(p["w_in"][l][:, IN_BASE:])], axis=-1).astype(BF16)

    dq = MLA_NOPE + MLA_ROPE
    wuq = p["w_uq"][l].reshape(Q_LORA, MLA_HEADS, dq)
    wuq = jnp.concatenate([wuq[..., :MLA_NOPE], _swapped_rope_cols(wuq[..., MLA_NOPE:])], axis=-1)
    wuq = wuq.reshape(Q_LORA, MLA_HEADS * LANES).astype(BF16)
    wukv = p["w_ukv"][l].reshape(KV_LORA, MLA_HEADS, MLA_NOPE + MLA_V)
    wk = jnp.concatenate([wukv[..., :MLA_NOPE], jnp.zeros_like(wukv[..., :MLA_NOPE])], axis=-1)
    wk = wk.reshape(KV_LORA, MLA_HEADS * LANES).astype(BF16)
    wv = jnp.concatenate([wukv[..., MLA_NOPE:], jnp.zeros_like(wukv[..., :LANES - MLA_V])], axis=-1)
    wv = wv.reshape(KV_LORA, MLA_HEADS * LANES).astype(BF16)

    scale = dq ** -0.5
    zeros64 = jnp.zeros((MLA_NOPE,), F32)
    gq = jnp.concatenate([p["qn_nope"][l], _swapped_rope_cols(p["qn_rope"][l])]) * scale
    gk = jnp.concatenate([p["kn_nope"][l], zeros64])
    gkr = jnp.concatenate([zeros64, _swapped_rope_cols(p["kn_rope"][l])])
    one_lane = jnp.zeros((LANES,), F32).at[MLA_V].set(1.0)
    vecs = jnp.stack([jnp.tile(r, MLA_HEADS) for r in (gq, gk, gkr, one_lane)]
                     + [jnp.zeros((MLA_HEADS * LANES,), F32)] * 4)
    return dict(
        ffn1_gain=row(p["ffn1_norm"]),
        ffn2_gain=row(p["ffn2_norm"]),
        inproj=(row(p["mix_norm"]), kr_cols, row(p["q_lat_norm"]), wuq, row(p["kv_lat_norm"]),
                wk, wv),
        vecs=vecs,
        gn=p["ret_head_norm"][l].reshape(RET_HEADS, 1, RET_DIM),
    )


def _group_matrix():
    idx = jnp.arange(2 * LANES)
    slab, lane = idx // LANES, idx % LANES
    same = slab[:, None] == slab[None, :]
    nope = (lane[:, None] < MLA_NOPE) & (lane[None, :] < MLA_NOPE)
    rope = ((lane[:, None] >= MLA_NOPE) & (lane[:, None] < MLA_NOPE + MLA_ROPE)
            & (lane[None, :] >= MLA_NOPE))
    g = jnp.where(nope, 1.0 / MLA_NOPE, jnp.where(rope, 1.0 / MLA_ROPE, 0.0))
    return jnp.where(same, g, 0.0).astype(BF16)


def kernel(x, positions, ffn1_norm, ffn1_w_gate, ffn1_w_up, ffn1_w_down, mix_norm, w_in, ret_head_norm, q_lat_norm, w_uq, kv_lat_norm, w_ukv, qn_nope, qn_rope, kn_nope, kn_rope, w_o, ffn2_norm, ffn2_w_gate, ffn2_w_up, ffn2_w_down):
    p = dict(ffn1_norm=ffn1_norm, ffn1_w_gate=ffn1_w_gate, ffn1_w_up=ffn1_w_up,
             ffn1_w_down=ffn1_w_down, mix_norm=mix_norm, w_in=w_in,
             ret_head_norm=ret_head_norm, q_lat_norm=q_lat_norm, w_uq=w_uq,
             kv_lat_norm=kv_lat_norm, w_ukv=w_ukv, qn_nope=qn_nope, qn_rope=qn_rope,
             kn_nope=kn_nope, kn_rope=kn_rope, w_o=w_o, ffn2_norm=ffn2_norm,
             ffn2_w_gate=ffn2_w_gate, ffn2_w_up=ffn2_w_up, ffn2_w_down=ffn2_w_down)
    batch, seq, d = x.shape
    depth = w_in.shape[0]
    dec = _retention_decay()
    gmat = _group_matrix()
    ffn1_stacks = [ffn1_w_gate, ffn1_w_up, ffn1_w_down]
    ffn2_stacks = [ffn2_w_gate, ffn2_w_up, ffn2_w_down]
    w_ffn1 = _cast_stacks(ffn1_stacks[:2], 1) + _cast_stacks(ffn1_stacks[2:], 1)
    win_main = w_in[:, :, :IN_BASE].astype(BF16)
    tabs = None
    xf = x.reshape(batch * seq, d)
    for l in range(depth):
        lp = _layer_params(l, p)
        xf, new_tabs, side_w = _ffn(xf, lp["ffn1_gain"], *w_ffn1,
                                    side=(ffn2_stacks + [w_o], l),
                                    positions=positions if l == 0 else None)
        tabs = new_tabs if l == 0 else tabs
        w_ffn2, wo = side_w[:3], side_w[3]
        gmix, kr_cols, gq, wuq, gkv, wk, wv = lp["inproj"]
        rq, rk, rv, rg, qc, kc, v = _inproj(xf, tabs, gmix, win_main, l, kr_cols, gq, wuq, gkv,
                                            wk, wv, gmat, lp["vecs"])
        y_ret = _retention(rq, rk, rv, rg, dec, lp["gn"], batch, seq)
        y_mla = _attention(qc, kc, v, batch, seq)
        xf, _, w_ffn1 = _ffn(
            xf, lp["ffn2_gain"], *w_ffn2,
            proj=(y_ret.reshape(batch * seq, -1), y_mla.reshape(batch * seq, -1), wo),
            side=(ffn1_stacks, l + 1) if l + 1 < depth else None)
    return xf.reshape(batch, seq, d)
```

```python
import functools

import jax
import jax.numpy as jnp
from jax import lax
from jax.experimental import pallas as pl
from jax.experimental.pallas import tpu as pltpu

D_MODEL = 1024
D_FF = 2816
RET_HEADS = 4
RET_DIM = 128
RET_CHUNK = 128
MLA_HEADS = 8
MLA_NOPE = 64
MLA_ROPE = 32
MLA_V = 64
Q_LORA = 256
KV_LORA = 128
ROPE_THETA = 10000.0
EPS = 1e-6

LANES = 128
BF16_ROWS = 16
RET_W = RET_HEADS * RET_DIM
IN_BASE = 4 * RET_W + Q_LORA + KV_LORA
IN_W = IN_BASE + LANES
HALF_ROPE = MLA_ROPE // 2
SLAB_SHIFT = LANES - MLA_ROPE

TOK_TILE = 512
FF_TILE = 256
CAST_ROWS = 256
ATT_TQ = 1024
ATT_HEADS = 4
RET_TILE = 1024
VMEM_LIMIT = 48 * 1024 * 1024

F32 = jnp.float32
BF16 = jnp.bfloat16


def _rms(x, gain):
    return x * lax.rsqrt(jnp.mean(x * x, axis=-1, keepdims=True) + EPS) * gain


def _resident(shape):
    nd = len(shape)
    return pl.BlockSpec(shape, lambda *_: (0,) * nd)


def _tables_rows(pos_ref, c_ref, o_ref, rows):
    ang = pos_ref[rows, :].astype(F32) * c_ref[0:1, :]
    c, s = jnp.cos(ang), jnp.sin(ang)
    low = lax.broadcasted_iota(jnp.int32, c.shape, 1) < RET_DIM // 2
    o_ref[0, rows] = jnp.where(low, c, pltpu.roll(c, RET_DIM // 2, 1))
    o_ref[1, rows] = jnp.where(low, s, pltpu.roll(s, RET_DIM // 2, 1)) * c_ref[1:2, :]
    o_ref[2, rows] = c_ref[2:3, :] + c_ref[3:4, :] * c
    o_ref[3, rows] = s * c_ref[4:5, :]


def _table_consts():
    inv_r = 1.0 / (ROPE_THETA ** (jnp.arange(0, RET_DIM, 2, dtype=F32) / RET_DIM))
    inv_m = 1.0 / (ROPE_THETA ** (jnp.arange(0, MLA_ROPE, 2, dtype=F32) / MLA_ROPE))
    half = RET_DIM // 2
    z = lambda n: jnp.zeros((n,), F32)
    o = lambda n: jnp.ones((n,), F32)
    return jnp.stack([
        jnp.concatenate([inv_r, inv_m, inv_m, z(MLA_ROPE)]),
        jnp.concatenate([-o(half), o(half)]),
        jnp.concatenate([o(MLA_NOPE), z(LANES - MLA_NOPE)]),
        jnp.concatenate([z(MLA_NOPE), o(MLA_ROPE), z(MLA_ROPE)]),
        jnp.concatenate([z(MLA_NOPE), -o(HALF_ROPE), o(HALF_ROPE), z(MLA_ROPE)]),
        z(LANES), z(LANES), z(LANES)])


def _ffn_body(*refs, with_proj, with_tables, n_side):
    refs = list(refs)
    take = lambda n: [refs.pop(0) for _ in range(n)]
    (x_ref,) = take(1)
    if with_proj:
        yr_ref, ym_ref, wo_ref = take(3)
    g_ref, wg_ref, wu_ref, wd_ref = take(4)
    if with_tables:
        pos_ref, c_ref = take(2)
    side_in = take(n_side)
    (o_ref,) = take(1)
    if with_tables:
        (tab_ref,) = take(1)
    side_out = take(n_side)
    (a_ref,) = take(1)

    x = x_ref[...]
    if with_proj:
        x = (x + jnp.dot(yr_ref[...], wo_ref[0:RET_W, :], preferred_element_type=F32)
             + jnp.dot(ym_ref[...], wo_ref[RET_W:, :], preferred_element_type=F32))
    h = _rms(x, g_ref[...]).astype(BF16)
    n_chunks = D_FF // FF_TILE
    tab_rows = x.shape[0] // n_chunks // 8 * 8
    for c in range(n_chunks):
        cols = slice(c * FF_TILE, (c + 1) * FF_TILE)
        g = jnp.dot(h, wg_ref[:, cols], preferred_element_type=F32)
        u = jnp.dot(h, wu_ref[:, cols], preferred_element_type=F32)
        a_ref[:, cols] = (g * jax.nn.sigmoid(g) * u).astype(BF16)
        if with_tables:
            stop = x.shape[0] if c == n_chunks - 1 else (c + 1) * tab_rows
            _tables_rows(pos_ref, c_ref, tab_ref, slice(c * tab_rows, stop))
    for src, dst in zip(side_in, side_out):
        dst[...] = src[...].astype(BF16)
    y = jnp.dot(a_ref[...], wd_ref[...], preferred_element_type=F32)
    o_ref[...] = x + 0.5 * y


def _cast_body(*refs):
    n = len(refs) // 2
    for src, dst in zip(refs[:n], refs[n:]):
        dst[...] = src[...].astype(BF16)


def _cast_stacks(ws, layers):
    _, r, c = ws[0].shape
    tr = CAST_ROWS
    blk = pl.BlockSpec((1, tr, c), lambda l, i: (l, i, 0))
    return pl.pallas_call(
        _cast_body,
        grid=(layers, r // tr),
        in_specs=[blk] * len(ws),
        out_specs=[blk] * len(ws),
        out_shape=[jax.ShapeDtypeStruct((layers, r, c), BF16) for w in ws],
        compiler_params=pltpu.CompilerParams(
            dimension_semantics=("parallel", "parallel"), vmem_limit_bytes=VMEM_LIMIT),
        name="cast_weights",
    )(*ws)


def _weight_block(w):
    if w.ndim == 2:
        return _resident(w.shape)
    return pl.BlockSpec((None,) + w.shape[1:], lambda *_: (0, 0, 0))


def _side_cast_specs(w, l, steps):
    _, r, c = w.shape
    hold = 1
    while (r * hold) % (steps * BF16_ROWS) or steps % hold:
        hold += 1
    rows = r * hold // steps
    return (pl.BlockSpec((None, rows, c), lambda i: (l, i // hold, 0)),
            pl.BlockSpec((rows, c), lambda i: (i // hold, 0)),
            jax.ShapeDtypeStruct((r, c), BF16))


def _ffn(x, gain, wg, wu, wd, proj=None, positions=None, side=None):
    t = x.shape[0]
    tm = TOK_TILE
    steps = t // tm
    row = lambda w: pl.BlockSpec((tm, w), lambda i: (i, 0))
    ins, specs = [x], [row(D_MODEL)]
    if proj is not None:
        yr, ym, wo = proj
        ins += [yr, ym, wo]
        specs += [row(RET_W), row(MLA_HEADS * MLA_V), _resident(wo.shape)]
    ins += [gain, wg, wu, wd]
    specs += [_resident(gain.shape), _weight_block(wg), _weight_block(wu), _weight_block(wd)]
    out_specs = [row(D_MODEL)]
    out_shape = [jax.ShapeDtypeStruct((t, D_MODEL), F32)]
    if positions is not None:
        ins += [positions.reshape(t, 1), _table_consts()]
        specs += [pl.BlockSpec((tm, 1), lambda i: (i, 0)), _resident((8, LANES))]
        out_specs.append(pl.BlockSpec((4, tm, LANES), lambda i: (0, i, 0)))
        out_shape.append(jax.ShapeDtypeStruct((4, t, LANES), F32))
    side_ws, side_layer = side if side is not None else ([], 0)
    for w in side_ws:
        in_spec, out_spec, shape = _side_cast_specs(w, side_layer, steps)
        ins.append(w)
        specs.append(in_spec)
        out_specs.append(out_spec)
        out_shape.append(shape)
    outs = pl.pallas_call(
        functools.partial(_ffn_body, with_proj=proj is not None,
                          with_tables=positions is not None, n_side=len(side_ws)),
        grid=(steps,),
        in_specs=specs,
        out_specs=out_specs,
        out_shape=out_shape,
        scratch_shapes=[pltpu.VMEM((tm, D_FF), BF16)],
        compiler_params=pltpu.CompilerParams(
            dimension_semantics=("arbitrary",), vmem_limit_bytes=VMEM_LIMIT),
        name="ffn_proj" if proj is not None else "ffn",
    )(*ins)
    tabs = outs[1] if positions is not None else None
    return outs[0], tabs, list(outs[len(outs) - len(side_ws):])


def _group_mean(x, gmat):
    x2 = x * x
    hi = x2.astype(BF16)
    lo = (x2 - hi.astype(F32)).astype(BF16)
    return (jnp.dot(hi, gmat, preferred_element_type=F32)
            + jnp.dot(lo, gmat, preferred_element_type=F32))


def _inproj_body(x_ref, gmix_ref, win_ref, krc_ref, tab_ref, gq_ref, wuq_ref, gkv_ref, wk_ref,
                 wv_ref, gmat_ref, vec_ref,
                 rq_ref, rk_ref, rv_ref, rg_ref, qc_ref, kc_ref, v_ref, wfull_ref):
    @pl.when(pl.program_id(0) == 0)
    def _():
        wfull_ref[:, 0:IN_BASE] = win_ref[...]
        wfull_ref[:, IN_BASE:] = krc_ref[...]

    h = _rms(x_ref[...], gmix_ref[...]).astype(BF16)
    proj = jnp.dot(h, wfull_ref[...], preferred_element_type=F32)
    cos_r, sin_r, cos_m, sin_m = tab_ref[0], tab_ref[1], tab_ref[2], tab_ref[3]
    ret_scale = RET_DIM ** -0.5
    for hd in range(RET_HEADS):
        sl = slice(hd * RET_DIM, (hd + 1) * RET_DIM)
        xq = proj[:, sl]
        rq_ref[:, sl] = (xq * cos_r + pltpu.roll(xq, RET_DIM // 2, 1) * sin_r).astype(BF16)
        xk = proj[:, RET_W + hd * RET_DIM:RET_W + (hd + 1) * RET_DIM]
        rk_ref[:, sl] = ((xk * cos_r + pltpu.roll(xk, RET_DIM // 2, 1) * sin_r)
                         * ret_scale).astype(BF16)
    rv_ref[...] = proj[:, 2 * RET_W:3 * RET_W].astype(BF16)
    rg_ref[...] = proj[:, 3 * RET_W:4 * RET_W]
    off = 4 * RET_W
    c_q = proj[:, off:off + Q_LORA]
    c_kv = proj[:, off + Q_LORA:off + Q_LORA + KV_LORA]
    kr = proj[:, off + Q_LORA + KV_LORA:]
    cqn = _rms(c_q, gq_ref[...]).astype(BF16)
    ckvn = _rms(c_kv, gkv_ref[...]).astype(BF16)
    qraw = jnp.dot(cqn, wuq_ref[...], preferred_element_type=F32)
    kraw = jnp.dot(ckvn, wk_ref[...], preferred_element_type=F32)
    v_ref[...] = (jnp.dot(ckvn, wv_ref[...], preferred_element_type=F32)
                  + vec_ref[3:4, :]).astype(BF16)

    lane = lax.broadcasted_iota(jnp.int32, (1, LANES), 1)
    in_rope = (lane >= MLA_NOPE) & (lane < MLA_NOPE + MLA_ROPE)
    ss = jnp.sum(jnp.where(in_rope, kr * kr, 0.0), axis=-1, keepdims=True) * (1.0 / MLA_ROPE)
    krn = kr * lax.rsqrt(ss + EPS) * vec_ref[2:3, 0:LANES]
    kro = krn * cos_m + pltpu.roll(krn, SLAB_SHIFT, 1) * sin_m

    gmat = gmat_ref[...]
    for pr in range(MLA_HEADS // 2):
        cols = slice(pr * 2 * LANES, (pr + 1) * 2 * LANES)
        xs = qraw[:, cols]
        y = xs * lax.rsqrt(_group_mean(xs, gmat) + EPS) * vec_ref[0:1, 0:2 * LANES]
        for e in range(2):
            sl = slice(e * LANES, (e + 1) * LANES)
            dst = slice((2 * pr + e) * LANES, (2 * pr + e + 1) * LANES)
            ys = y[:, sl]
            qc_ref[:, dst] = (ys * cos_m + pltpu.roll(ys, SLAB_SHIFT, 1) * sin_m).astype(BF16)
            ks = kraw[:, dst]
            ms = jnp.sum(ks * ks, axis=-1, keepdims=True) * (1.0 / MLA_NOPE)
            kc_ref[:, dst] = (ks * lax.rsqrt(ms + EPS) * vec_ref[1:2, 0:LANES] + kro).astype(BF16)


def _inproj(x, tabs, gmix, win, krc, gq, wuq, gkv, wk, wv, gmat, vecs):
    t = x.shape[0]
    tm = TOK_TILE
    row = lambda w: pl.BlockSpec((tm, w), lambda i: (i, 0))
    slab_w = MLA_HEADS * LANES
    outs = [(RET_W, BF16), (RET_W, BF16), (RET_W, BF16), (RET_W, F32),
            (slab_w, BF16), (slab_w, BF16), (slab_w, BF16)]
    return pl.pallas_call(
        _inproj_body,
        grid=(t // tm,),
        in_specs=[row(D_MODEL), _resident(gmix.shape),
                  _resident(win.shape), _resident(krc.shape),
                  pl.BlockSpec((4, tm, LANES), lambda i: (0, i, 0)),
                  _resident(gq.shape), _resident(wuq.shape), _resident(gkv.shape),
                  _resident(wk.shape), _resident(wv.shape), _resident(gmat.shape),
                  _resident(vecs.shape)],
        out_specs=[row(w) for w, _ in outs],
        out_shape=[jax.ShapeDtypeStruct((t, w), dt) for w, dt in outs],
        scratch_shapes=[pltpu.VMEM((D_MODEL, IN_W), BF16)],
        compiler_params=pltpu.CompilerParams(
            dimension_semantics=("arbitrary",), vmem_limit_bytes=VMEM_LIMIT),
        name="inproj",
    )(x, gmix, win, krc, tabs, gq, wuq, gkv, wk, wv, gmat, vecs)


def _retention_body(q_ref, k_ref, v_ref, g_ref, dec_ref, gn_ref, o_ref,
                    state_ref, s_ref, kv_ref, prev_ref):
    @pl.when(pl.program_id(1) == 0)
    def _():
        state_ref[...] = jnp.zeros_like(state_ref)

    c = RET_CHUNK
    nt = (((1,), (1,)), ((), ()))
    n = RET_TILE // c
    blocks = [(hd, ci) for ci in range(n) for hd in range(RET_HEADS)]
    tile = lambda hd, ci: (slice(ci * c, (ci + 1) * c), slice(hd * RET_DIM, (hd + 1) * RET_DIM))

    for hd, ci in blocks:
        rows, cols = tile(hd, ci)
        k = k_ref[0, rows, cols]
        v = v_ref[0, rows, cols]
        s = lax.dot_general(q_ref[0, rows, cols], k, nt, preferred_element_type=F32)
        s_ref[hd * n + ci] = (s * dec_ref[0, hd]).astype(BF16)
        kz = (k.astype(F32) * dec_ref[1, hd]).astype(BF16)
        kv_ref[hd * n + ci] = jnp.dot(kz.T, v, preferred_element_type=F32)
    for hd in range(RET_HEADS):
        st = state_ref[hd]
        for ci in range(n):
            prev_ref[hd * n + ci] = st.astype(BF16)
            st = st * dec_ref[3, hd] + kv_ref[hd * n + ci]
        state_ref[hd] = st
    for hd, ci in blocks:
        rows, cols = tile(hd, ci)
        q = q_ref[0, rows, cols]
        y = (jnp.dot(s_ref[hd * n + ci], v_ref[0, rows, cols], preferred_element_type=F32)
             + jnp.dot(q, prev_ref[hd * n + ci], preferred_element_type=F32) * dec_ref[2, hd])
        g = g_ref[0, rows, cols]
        o_ref[0, rows, cols] = (_rms(y, gn_ref[hd]) * (g * jax.nn.sigmoid(g))).astype(BF16)


def _retention(rq, rk, rv, rg, dec, gn, batch, seq):
    shp = (batch, seq, RET_W)
    nblk = RET_HEADS * (RET_TILE // RET_CHUNK)
    blk = pl.BlockSpec((1, RET_TILE, RET_W), lambda b, s: (b, s, 0))
    return pl.pallas_call(
        _retention_body,
        grid=(batch, seq // RET_TILE),
        in_specs=[blk, blk, blk, blk, _resident(dec.shape), _resident(gn.shape)],
        out_specs=blk,
        out_shape=jax.ShapeDtypeStruct(shp, BF16),
        scratch_shapes=[pltpu.VMEM((RET_HEADS, RET_DIM, RET_DIM), F32),
                        pltpu.VMEM((nblk, RET_CHUNK, RET_CHUNK), BF16),
                        pltpu.VMEM((nblk, RET_DIM, RET_DIM), F32),
                        pltpu.VMEM((nblk, RET_DIM, RET_DIM), BF16)],
        compiler_params=pltpu.CompilerParams(
            dimension_semantics=("parallel", "arbitrary"), vmem_limit_bytes=VMEM_LIMIT),
        name="retention",
    )(rq.reshape(shp), rk.reshape(shp), rv.reshape(shp), rg.reshape(shp), dec, gn)


def _attention_body(q_ref, k_ref, v_ref, o_ref, m_ref, acc_ref):
    qi = pl.program_id(2)
    nt = (((1,), (1,)), ((), ()))
    half = ATT_TQ // 2
    m_ref[...] = jnp.full_like(m_ref, -1e30)
    acc_ref[...] = jnp.zeros_like(acc_ref)

    def steps(specs):
        cols = lambda e: slice(e * LANES, (e + 1) * LANES)
        keys = lambda start, width: pl.ds(pl.multiple_of(start, half), width)
        scores = [lax.dot_general(q_ref[0, q_rows, cols(e)], k_ref[0, keys(start, width), cols(e)],
                                  nt, preferred_element_type=F32)
                  for e, q_rows, start, width, _ in specs]
        probs = []
        for (e, q_rows, _, width, mask_from), s in zip(specs, scores):
            chunks = [s[:, c * LANES:(c + 1) * LANES] for c in range(width // LANES)]
            if mask_from is not None:
                r = lax.broadcasted_iota(jnp.int32, chunks[0].shape, 0)
                lane = lax.broadcasted_iota(jnp.int32, chunks[0].shape, 1)
                for c in range(mask_from, len(chunks)):
                    chunks[c] = jnp.where(r >= lane + (c - mask_from) * LANES, chunks[c], -1e30)
            m_old = m_ref[e, q_rows]
            m_new = jnp.maximum(
                m_old, jnp.max(functools.reduce(jnp.maximum, chunks), axis=-1, keepdims=True))
            m_ref[e, q_rows] = m_new
            probs.append((jnp.exp(m_old - m_new), jnp.concatenate(
                [jnp.exp(ch - m_new).astype(BF16) for ch in chunks], axis=1)))
        for (e, q_rows, start, width, _), (alpha, p) in zip(specs, probs):
            acc_ref[e, q_rows] = alpha * acc_ref[e, q_rows] + jnp.dot(
                p, v_ref[0, keys(start, width), cols(e)], preferred_element_type=F32)

    def loop_body(kb, carry):
        for e in range(ATT_HEADS):
            steps([(e, slice(0, ATT_TQ), kb * ATT_TQ, ATT_TQ, None)])
        return carry

    lax.fori_loop(0, qi, loop_body, 0)
    diag = qi * ATT_TQ
    steps([spec for e in range(ATT_HEADS) for spec in (
        (e, slice(0, half), diag, half, 0),
        (e, slice(half, ATT_TQ), diag, ATT_TQ, half // LANES))])
    lane = lax.broadcasted_iota(jnp.int32, (ATT_TQ, LANES), 1)
    for pr in range(ATT_HEADS // 2):
        o0, o1 = [acc * (1.0 / acc[:, MLA_V:MLA_V + 1])
                  for acc in (acc_ref[2 * pr], acc_ref[2 * pr + 1])]
        o_ref[0, :, pr * LANES:(pr + 1) * LANES] = jnp.where(
            lane < MLA_V, o0, pltpu.roll(o1, MLA_V, 1)).astype(BF16)


def _attention(qc, kc, vc, batch, seq):
    slab_w = MLA_HEADS * LANES
    grp_w = ATT_HEADS * LANES
    out_w = ATT_HEADS * MLA_V
    seq_blk = pl.BlockSpec((1, seq, grp_w), lambda b, j, i: (b, 0, j))
    return pl.pallas_call(
        _attention_body,
        grid=(batch, MLA_HEADS // ATT_HEADS, seq // ATT_TQ),
        in_specs=[pl.BlockSpec((1, ATT_TQ, grp_w), lambda b, j, i: (b, i, j)),
                  seq_blk, seq_blk],
        out_specs=pl.BlockSpec((1, ATT_TQ, out_w), lambda b, j, i: (b, i, j)),
        out_shape=jax.ShapeDtypeStruct((batch, seq, MLA_HEADS * MLA_V), BF16),
        scratch_shapes=[pltpu.VMEM((ATT_HEADS, ATT_TQ, LANES), F32),
                        pltpu.VMEM((ATT_HEADS, ATT_TQ, LANES), F32)],
        compiler_params=pltpu.CompilerParams(
            dimension_semantics=("parallel", "parallel", "arbitrary"),
            vmem_limit_bytes=VMEM_LIMIT),
        name="latent_attention",
    )(qc.reshape(batch, seq, slab_w), kc.reshape(batch, seq, slab_w),
      vc.reshape(batch, seq, slab_w))


def _retention_decay():
    c = RET_CHUNK
    lg = jnp.log(1.0 - 2.0 ** (-5.0 - jnp.arange(RET_HEADS, dtype=F32)))
    idx = jnp.arange(c, dtype=F32)
    rel = idx[:, None] - idx[None, :]
    decay = jnp.where(rel >= 0, jnp.exp(jnp.maximum(rel, 0.0)[None] * lg[:, None, None]), 0.0)
    zeta = jnp.exp((c - 1 - idx)[None, :] * lg[:, None])
    xi = jnp.exp((idx + 1.0)[None, :] * lg[:, None])
    cdec = jnp.exp(c * lg)
    full = (RET_HEADS, c, RET_DIM)
    return jnp.stack([decay,
                      jnp.broadcast_to(zeta[:, :, None], full),
                      jnp.broadcast_to(xi[:, :, None], full),
                      jnp.broadcast_to(cdec[:, None, None], full)])


def _swapped_rope_cols(w):
    a, b = w[..., :HALF_ROPE], w[..., HALF_ROPE:]
    return jnp.concatenate([a, b, b, a], axis=-1)


def _layer_params(l, p, win_bf16):
    row = lambda g: g[l].reshape(1, -1)
    kr_cols = jnp.concatenate(
        [jnp.zeros((D_MODEL, MLA_NOPE), BF16),
         _swapped_rope_cols(win_bf16[l][:, IN_BASE:])], axis=-1)

    dq = MLA_NOPE + MLA_ROPE
    wuq = p["w_uq"][l].reshape(Q_LORA, MLA_HEADS, dq)
    wuq = jnp.concatenate([wuq[..., :MLA_NOPE], _swapped_rope_cols(wuq[..., MLA_NOPE:])], axis=-1)
    wuq = wuq.reshape(Q_LORA, MLA_HEADS * LANES).astype(BF16)
    wukv = p["w_ukv"][l].reshape(KV_LORA, MLA_HEADS, MLA_NOPE + MLA_V)
    wk = jnp.concatenate([wukv[..., :MLA_NOPE], jnp.zeros_like(wukv[..., :MLA_NOPE])], axis=-1)
    wk = wk.reshape(KV_LORA, MLA_HEADS * LANES).astype(BF16)
    wv = jnp.concatenate([wukv[..., MLA_NOPE:], jnp.zeros_like(wukv[..., :LANES - MLA_V])], axis=-1)
    wv = wv.reshape(KV_LORA, MLA_HEADS * LANES).astype(BF16)

    scale = dq ** -0.5
    zeros64 = jnp.zeros((MLA_NOPE,), F32)
    gq = jnp.concatenate([p["qn_nope"][l], _swapped_rope_cols(p["qn_rope"][l])]) * scale
    gk = jnp.concatenate([p["kn_nope"][l], zeros64])
    gkr = jnp.concatenate([zeros64, _swapped_rope_cols(p["kn_rope"][l])])
    one_lane = jnp.zeros((LANES,), F32).at[MLA_V].set(1.0)
    vecs = jnp.stack([jnp.tile(r, MLA_HEADS) for r in (gq, gk, gkr, one_lane)]
                     + [jnp.zeros((MLA_HEADS * LANES,), F32)] * 4)
    return dict(
        ffn1_gain=row(p["ffn1_norm"]),
        ffn2_gain=row(p["ffn2_norm"]),
        inproj=(row(p["mix_norm"]), kr_cols, row(p["q_lat_norm"]), wuq, row(p["kv_lat_norm"]),
                wk, wv),
        vecs=vecs,
        gn=p["ret_head_norm"][l].reshape(RET_HEADS, 1, RET_DIM),
    )


def _group_matrix():
    idx = jnp.arange(2 * LANES)
    slab, lane = idx // LANES, idx % LANES
    same = slab[:, None] == slab[None, :]
    nope = (lane[:, None] < MLA_NOPE) & (lane[None, :] < MLA_NOPE)
    rope = ((lane[:, None] >= MLA_NOPE) & (lane[:, None] < MLA_NOPE + MLA_ROPE)
            & (lane[None, :] >= MLA_NOPE))
    g = jnp.where(nope, 1.0 / MLA_NOPE, jnp.where(rope, 1.0 / MLA_ROPE, 0.0))
    return jnp.where(same, g, 0.0).astype(BF16)


def kernel(x, positions, ffn1_norm, ffn1_w_gate, ffn1_w_up, ffn1_w_down, mix_norm, w_in, ret_head_norm, q_lat_norm, w_uq, kv_lat_norm, w_ukv, qn_nope, qn_rope, kn_nope, kn_rope, w_o, ffn2_norm, ffn2_w_gate, ffn2_w_up, ffn2_w_down):
    p = dict(ffn1_norm=ffn1_norm, ffn1_w_gate=ffn1_w_gate, ffn1_w_up=ffn1_w_up,
             ffn1_w_down=ffn1_w_down, mix_norm=mix_norm, w_in=w_in,
             ret_head_norm=ret_head_norm, q_lat_norm=q_lat_norm, w_uq=w_uq,
             kv_lat_norm=kv_lat_norm, w_ukv=w_ukv, qn_nope=qn_nope, qn_rope=qn_rope,
             kn_nope=kn_nope, kn_rope=kn_rope, w_o=w_o, ffn2_norm=ffn2_norm,
             ffn2_w_gate=ffn2_w_gate, ffn2_w_up=ffn2_w_up, ffn2_w_down=ffn2_w_down)
    batch, seq, d = x.shape
    depth = w_in.shape[0]
    dec = _retention_decay()
    gmat = _group_matrix()
    ffn1_stacks = [ffn1_w_gate, ffn1_w_up, ffn1_w_down]
    ffn2_stacks = [ffn2_w_gate, ffn2_w_up, ffn2_w_down]
    w_ffn1 = _cast_stacks(ffn1_stacks[:2], 1) + _cast_stacks(ffn1_stacks[2:], 1)
    win_bf16 = w_in.astype(BF16)
    tabs = None
    xf = x.reshape(batch * seq, d)
    for l in range(depth):
        lp = _layer_params(l, p, win_bf16)
        xf, new_tabs, side_w = _ffn(xf, lp["ffn1_gain"], *w_ffn1,
                                    side=(ffn2_stacks + [w_o], l),
                                    positions=positions if l == 0 else None)
        tabs = new_tabs if l == 0 else tabs
        w_ffn2, wo = side_w[:3], side_w[3]
        gmix, kr_cols, gq, wuq, gkv, wk, wv = lp["inproj"]
        win_main = win_bf16[l, :, :IN_BASE]
        rq, rk, rv, rg, qc, kc, v = _inproj(xf, tabs, gmix, win_main, kr_cols, gq, wuq, gkv,
                                            wk, wv, gmat, lp["vecs"])
        y_ret = _retention(rq, rk, rv, rg, dec, lp["gn"], batch, seq)
        y_mla = _attention(qc, kc, v, batch, seq)
        xf, _, w_ffn1 = _ffn(
            xf, lp["ffn2_gain"], *w_ffn2,
            proj=(y_ret.reshape(batch * seq, -1), y_mla.reshape(batch * seq, -1), wo),
            side=(ffn1_stacks, l + 1) if l + 1 < depth else None)
    return xf.reshape(batch, seq, d)
```

```python
import functools

import jax
import jax.numpy as jnp
from jax import lax
from jax.experimental import pallas as pl
from jax.experimental.pallas import tpu as pltpu

D_MODEL = 1024
D_FF = 2816
RET_HEADS = 4
RET_DIM = 128
RET_CHUNK = 128
MLA_HEADS = 8
MLA_NOPE = 64
MLA_ROPE = 32
MLA_V = 64
Q_LORA = 256
KV_LORA = 128
ROPE_THETA = 10000.0
EPS = 1e-6

LANES = 128
BF16_ROWS = 16
RET_W = RET_HEADS * RET_DIM
IN_BASE = 4 * RET_W + Q_LORA + KV_LORA
IN_W = IN_BASE + LANES
HALF_ROPE = MLA_ROPE // 2
SLAB_SHIFT = LANES - MLA_ROPE

TOK_TILE = 512
FF_TILE = 256
CAST_ROWS = 256
ATT_TQ = 1024
ATT_HEADS = 4
RET_TILE = 1024
VMEM_LIMIT = 48 * 1024 * 1024

F32 = jnp.float32
BF16 = jnp.bfloat16


def _rms(x, gain):
    return x * lax.rsqrt(jnp.mean(x * x, axis=-1, keepdims=True) + EPS) * gain


def _resident(shape):
    nd = len(shape)
    return pl.BlockSpec(shape, lambda *_: (0,) * nd)


def _tables_rows(pos_ref, c_ref, o_ref, rows):
    ang = pos_ref[rows, :].astype(F32) * c_ref[0:1, :]
    c, s = jnp.cos(ang), jnp.sin(ang)
    low = lax.broadcasted_iota(jnp.int32, c.shape, 1) < RET_DIM // 2
    o_ref[0, rows] = jnp.where(low, c, pltpu.roll(c, RET_DIM // 2, 1))
    o_ref[1, rows] = jnp.where(low, s, pltpu.roll(s, RET_DIM // 2, 1)) * c_ref[1:2, :]
    o_ref[2, rows] = c_ref[2:3, :] + c_ref[3:4, :] * c
    o_ref[3, rows] = s * c_ref[4:5, :]


def _table_consts():
    inv_r = 1.0 / (ROPE_THETA ** (jnp.arange(0, RET_DIM, 2, dtype=F32) / RET_DIM))
    inv_m = 1.0 / (ROPE_THETA ** (jnp.arange(0, MLA_ROPE, 2, dtype=F32) / MLA_ROPE))
    half = RET_DIM // 2
    z = lambda n: jnp.zeros((n,), F32)
    o = lambda n: jnp.ones((n,), F32)
    return jnp.stack([
        jnp.concatenate([inv_r, inv_m, inv_m, z(MLA_ROPE)]),
        jnp.concatenate([-o(half), o(half)]),
        jnp.concatenate([o(MLA_NOPE), z(LANES - MLA_NOPE)]),
        jnp.concatenate([z(MLA_NOPE), o(MLA_ROPE), z(MLA_ROPE)]),
        jnp.concatenate([z(MLA_NOPE), -o(HALF_ROPE), o(HALF_ROPE), z(MLA_ROPE)]),
        z(LANES), z(LANES), z(LANES)])


def _ffn_body(*refs, with_proj, with_tables, n_side):
    refs = list(refs)
    take = lambda n: [refs.pop(0) for _ in range(n)]
    (x_ref,) = take(1)
    if with_proj:
        yr_ref, ym_ref, wo_ref = take(3)
    g_ref, wg_ref, wu_ref, wd_ref = take(4)
    if with_tables:
        pos_ref, c_ref = take(2)
    side_in = take(n_side)
    (o_ref,) = take(1)
    if with_tables:
        (tab_ref,) = take(1)
    side_out = take(n_side)
    (a_ref,) = take(1)

    x = x_ref[...]
    if with_proj:
        x = (x + jnp.dot(yr_ref[...], wo_ref[0:RET_W, :], preferred_element_type=F32)
             + jnp.dot(ym_ref[...], wo_ref[RET_W:, :], preferred_element_type=F32))
    h = _rms(x, g_ref[...]).astype(BF16)
    n_chunks = D_FF // FF_TILE
    tab_rows = x.shape[0] // n_chunks // 8 * 8
    for c in range(n_chunks):
        cols = slice(c * FF_TILE, (c + 1) * FF_TILE)
        g = jnp.dot(h, wg_ref[:, cols], preferred_element_type=F32)
        u = jnp.dot(h, wu_ref[:, cols], preferred_element_type=F32)
        a_ref[:, cols] = (g * jax.nn.sigmoid(g) * u).astype(BF16)
        if with_tables:
            stop = x.shape[0] if c == n_chunks - 1 else (c + 1) * tab_rows
            _tables_rows(pos_ref, c_ref, tab_ref, slice(c * tab_rows, stop))
    for src, dst in zip(side_in, side_out):
        dst[...] = src[...].astype(BF16)
    y = jnp.dot(a_ref[...], wd_ref[...], preferred_element_type=F32)
    o_ref[...] = x + 0.5 * y


def _cast_body(*refs):
    n = len(refs) // 2
    for src, dst in zip(refs[:n], refs[n:]):
        dst[...] = src[...].astype(BF16)


def _cast_stacks(ws, layers):
    _, r, c = ws[0].shape
    tr = CAST_ROWS
    blk = pl.BlockSpec((1, tr, c), lambda l, i: (l, i, 0))
    return pl.pallas_call(
        _cast_body,
        grid=(layers, r // tr),
        in_specs=[blk] * len(ws),
        out_specs=[blk] * len(ws),
        out_shape=[jax.ShapeDtypeStruct((layers, r, c), BF16) for w in ws],
        compiler_params=pltpu.CompilerParams(
            dimension_semantics=("parallel", "parallel"), vmem_limit_bytes=VMEM_LIMIT),
        name="cast_weights",
    )(*ws)


def _weight_block(w):
    if w.ndim == 2:
        return _resident(w.shape)
    return pl.BlockSpec((None,) + w.shape[1:], lambda *_: (0, 0, 0))


def _side_cast_specs(w, l, steps):
    _, r, c = w.shape
    hold = 1
    while (r * hold) % (steps * BF16_ROWS) or steps % hold:
        hold += 1
    rows = r * hold // steps
    return (pl.BlockSpec((None, rows, c), lambda i: (l, i // hold, 0)),
            pl.BlockSpec((rows, c), lambda i: (i // hold, 0)),
            jax.ShapeDtypeStruct((r, c), BF16))


def _ffn(x, gain, wg, wu, wd, proj=None, positions=None, side=None):
    t = x.shape[0]
    tm = TOK_TILE
    steps = t // tm
    row = lambda w: pl.BlockSpec((tm, w), lambda i: (i, 0))
    ins, specs = [x], [row(D_MODEL)]
    if proj is not None:
        yr, ym, wo = proj
        ins += [yr, ym, wo]
        specs += [row(RET_W), row(MLA_HEADS * MLA_V), _resident(wo.shape)]
    ins += [gain, wg, wu, wd]
    specs += [_resident(gain.shape), _weight_block(wg), _weight_block(wu), _weight_block(wd)]
    out_specs = [row(D_MODEL)]
    out_shape = [jax.ShapeDtypeStruct((t, D_MODEL), F32)]
    if positions is not None:
        ins += [positions.reshape(t, 1), _table_consts()]
        specs += [pl.BlockSpec((tm, 1), lambda i: (i, 0)), _resident((8, LANES))]
        out_specs.append(pl.BlockSpec((4, tm, LANES), lambda i: (0, i, 0)))
        out_shape.append(jax.ShapeDtypeStruct((4, t, LANES), F32))
    side_ws, side_layer = side if side is not None else ([], 0)
    for w in side_ws:
        in_spec, out_spec, shape = _side_cast_specs(w, side_layer, steps)
        ins.append(w)
        specs.append(in_spec)
        out_specs.append(out_spec)
        out_shape.append(shape)
    outs = pl.pallas_call(
        functools.partial(_ffn_body, with_proj=proj is not None,
                          with_tables=positions is not None, n_side=len(side_ws)),
        grid=(steps,),
        in_specs=specs,
        out_specs=out_specs,
        out_shape=out_shape,
        scratch_shapes=[pltpu.VMEM((tm, D_FF), BF16)],
        compiler_params=pltpu.CompilerParams(
            dimension_semantics=("arbitrary",), vmem_limit_bytes=VMEM_LIMIT),
        name="ffn_proj" if proj is not None else "ffn",
    )(*ins)
    tabs = outs[1] if positions is not None else None
    return outs[0], tabs, list(outs[len(outs) - len(side_ws):])


def _group_mean(x, gmat):
    x2 = x * x
    hi = x2.astype(BF16)
    lo = (x2 - hi.astype(F32)).astype(BF16)
    return (jnp.dot(hi, gmat, preferred_element_type=F32)
            + jnp.dot(lo, gmat, preferred_element_type=F32))


def _inproj_body(x_ref, gmix_ref, win_ref, krc_ref, tab_ref, gq_ref, wuq_ref, gkv_ref, wk_ref,
                 wv_ref, gmat_ref, vec_ref,
                 rq_ref, rk_ref, rv_ref, rg_ref, qc_ref, kc_ref, v_ref, wlat_ref):
    lat0 = 4 * RET_W

    @pl.when(pl.program_id(0) == 0)
    def _():
        wlat_ref[:, 0:IN_BASE - lat0] = win_ref[:, lat0:]
        wlat_ref[:, IN_BASE - lat0:] = krc_ref[...]

    h = _rms(x_ref[...], gmix_ref[...]).astype(BF16)
    cos_r, sin_r, cos_m, sin_m = tab_ref[0], tab_ref[1], tab_ref[2], tab_ref[3]
    lat = jnp.dot(h, wlat_ref[...], preferred_element_type=F32)
    c_q = lat[:, 0:Q_LORA]
    c_kv = lat[:, Q_LORA:Q_LORA + KV_LORA]
    kr = lat[:, Q_LORA + KV_LORA:]

    cqn = _rms(c_q, gq_ref[...]).astype(BF16)
    ckvn = _rms(c_kv, gkv_ref[...]).astype(BF16)
    qraw = jnp.dot(cqn, wuq_ref[...], preferred_element_type=F32)
    kraw = jnp.dot(ckvn, wk_ref[...], preferred_element_type=F32)
    v_ref[...] = (jnp.dot(ckvn, wv_ref[...], preferred_element_type=F32)
                  + vec_ref[3:4, :]).astype(BF16)

    lane = lax.broadcasted_iota(jnp.int32, (1, LANES), 1)
    in_rope = (lane >= MLA_NOPE) & (lane < MLA_NOPE + MLA_ROPE)
    ss = jnp.sum(jnp.where(in_rope, kr * kr, 0.0), axis=-1, keepdims=True) * (1.0 / MLA_ROPE)
    krn = kr * lax.rsqrt(ss + EPS) * vec_ref[2:3, 0:LANES]
    kro = krn * cos_m + pltpu.roll(krn, SLAB_SHIFT, 1) * sin_m

    gmat = gmat_ref[...]
    for pr in range(MLA_HEADS // 2):
        cols = slice(pr * 2 * LANES, (pr + 1) * 2 * LANES)
        xs = qraw[:, cols]
        y = xs * lax.rsqrt(_group_mean(xs, gmat) + EPS) * vec_ref[0:1, 0:2 * LANES]
        for e in range(2):
            sl = slice(e * LANES, (e + 1) * LANES)
            dst = slice((2 * pr + e) * LANES, (2 * pr + e + 1) * LANES)
            ys = y[:, sl]
            qc_ref[:, dst] = (ys * cos_m + pltpu.roll(ys, SLAB_SHIFT, 1) * sin_m).astype(BF16)
            ks = kraw[:, dst]
            ms = jnp.sum(ks * ks, axis=-1, keepdims=True) * (1.0 / MLA_NOPE)
            kc_ref[:, dst] = (ks * lax.rsqrt(ms + EPS) * vec_ref[1:2, 0:LANES] + kro).astype(BF16)

    def ret_cols(i):
        return jnp.dot(h, win_ref[:, i * RET_W:(i + 1) * RET_W], preferred_element_type=F32)

    ret_scale = RET_DIM ** -0.5
    for dst_ref, scale, i in ((rq_ref, None, 0), (rk_ref, ret_scale, 1)):
        x = ret_cols(i)
        for hd in range(RET_HEADS):
            sl = slice(hd * RET_DIM, (hd + 1) * RET_DIM)
            roped = x[:, sl] * cos_r + pltpu.roll(x[:, sl], RET_DIM // 2, 1) * sin_r
            dst_ref[:, sl] = (roped if scale is None else roped * scale).astype(BF16)
    rv_ref[...] = ret_cols(2).astype(BF16)
    rg_ref[...] = ret_cols(3)


def _inproj(x, tabs, gmix, win, krc, gq, wuq, gkv, wk, wv, gmat, vecs):
    t = x.shape[0]
    tm = TOK_TILE
    row = lambda w: pl.BlockSpec((tm, w), lambda i: (i, 0))
    slab_w = MLA_HEADS * LANES
    outs = [(RET_W, BF16), (RET_W, BF16), (RET_W, BF16), (RET_W, F32),
            (slab_w, BF16), (slab_w, BF16), (slab_w, BF16)]
    return pl.pallas_call(
        _inproj_body,
        grid=(t // tm,),
        in_specs=[row(D_MODEL), _resident(gmix.shape),
                  _resident(win.shape), _resident(krc.shape),
                  pl.BlockSpec((4, tm, LANES), lambda i: (0, i, 0)),
                  _resident(gq.shape), _resident(wuq.shape), _resident(gkv.shape),
                  _resident(wk.shape), _resident(wv.shape), _resident(gmat.shape),
                  _resident(vecs.shape)],
        out_specs=[row(w) for w, _ in outs],
        out_shape=[jax.ShapeDtypeStruct((t, w), dt) for w, dt in outs],
        scratch_shapes=[pltpu.VMEM((D_MODEL, IN_W - 4 * RET_W), BF16)],
        compiler_params=pltpu.CompilerParams(
            dimension_semantics=("arbitrary",), vmem_limit_bytes=VMEM_LIMIT),
        name="inproj",
    )(x, gmix, win, krc, tabs, gq, wuq, gkv, wk, wv, gmat, vecs)


def _retention_body(q_ref, k_ref, v_ref, g_ref, dec_ref, gn_ref, o_ref,
                    state_ref, s_ref, kv_ref, prev_ref):
    @pl.when(pl.program_id(1) == 0)
    def _():
        state_ref[...] = jnp.zeros_like(state_ref)

    c = RET_CHUNK
    nt = (((1,), (1,)), ((), ()))
    n = RET_TILE // c
    blocks = [(hd, ci) for ci in range(n) for hd in range(RET_HEADS)]
    tile = lambda hd, ci: (slice(ci * c, (ci + 1) * c), slice(hd * RET_DIM, (hd + 1) * RET_DIM))

    for hd, ci in blocks:
        rows, cols = tile(hd, ci)
        k = k_ref[0, rows, cols]
        v = v_ref[0, rows, cols]
        s = lax.dot_general(q_ref[0, rows, cols], k, nt, preferred_element_type=F32)
        s_ref[hd * n + ci] = (s * dec_ref[0, hd]).astype(BF16)
        kz = (k.astype(F32) * dec_ref[1, hd]).astype(BF16)
        kv_ref[hd * n + ci] = jnp.dot(kz.T, v, preferred_element_type=F32)
    for hd in range(RET_HEADS):
        st = state_ref[hd]
        for ci in range(n):
            prev_ref[hd * n + ci] = st.astype(BF16)
            st = st * dec_ref[3, hd] + kv_ref[hd * n + ci]
        state_ref[hd] = st
    for hd, ci in blocks:
        rows, cols = tile(hd, ci)
        q = q_ref[0, rows, cols]
        y = (jnp.dot(s_ref[hd * n + ci], v_ref[0, rows, cols], preferred_element_type=F32)
             + jnp.dot(q, prev_ref[hd * n + ci], preferred_element_type=F32) * dec_ref[2, hd])
        g = g_ref[0, rows, cols]
        o_ref[0, rows, cols] = (_rms(y, gn_ref[hd]) * (g * jax.nn.sigmoid(g))).astype(BF16)


def _retention(rq, rk, rv, rg, dec, gn, batch, seq):
    shp = (batch, seq, RET_W)
    nblk = RET_HEADS * (RET_TILE // RET_CHUNK)
    blk = pl.BlockSpec((1, RET_TILE, RET_W), lambda b, s: (b, s, 0))
    return pl.pallas_call(
        _retention_body,
        grid=(batch, seq // RET_TILE),
        in_specs=[blk, blk, blk, blk, _resident(dec.shape), _resident(gn.shape)],
        out_specs=blk,
        out_shape=jax.ShapeDtypeStruct(shp, BF16),
        scratch_shapes=[pltpu.VMEM((RET_HEADS, RET_DIM, RET_DIM), F32),
                        pltpu.VMEM((nblk, RET_CHUNK, RET_CHUNK), BF16),
                        pltpu.VMEM((nblk, RET_DIM, RET_DIM), F32),
                        pltpu.VMEM((nblk, RET_DIM, RET_DIM), BF16)],
        compiler_params=pltpu.CompilerParams(
            dimension_semantics=("parallel", "arbitrary"), vmem_limit_bytes=VMEM_LIMIT),
        name="retention",
    )(rq.reshape(shp), rk.reshape(shp), rv.reshape(shp), rg.reshape(shp), dec, gn)


def _attention_body(q_ref, k_ref, v_ref, o_ref, m_ref, acc_ref):
    qi = pl.program_id(2)
    nt = (((1,), (1,)), ((), ()))
    half = ATT_TQ // 2
    m_ref[...] = jnp.full_like(m_ref, -1e30)
    acc_ref[...] = jnp.zeros_like(acc_ref)

    def steps(specs):
        cols = lambda e: slice(e * LANES, (e + 1) * LANES)
        keys = lambda start, width: pl.ds(pl.multiple_of(start, half), width)
        scores = [lax.dot_general(q_ref[0, q_rows, cols(e)], k_ref[0, keys(start, width), cols(e)],
                                  nt, preferred_element_type=F32)
                  for e, q_rows, start, width, _ in specs]
        probs = []
        for (e, q_rows, _, width, mask_from), s in zip(specs, scores):
            chunks = [s[:, c * LANES:(c + 1) * LANES] for c in range(width // LANES)]
            if mask_from is not None:
                r = lax.broadcasted_iota(jnp.int32, chunks[0].shape, 0)
                lane = lax.broadcasted_iota(jnp.int32, chunks[0].shape, 1)
                for c in range(mask_from, len(chunks)):
                    chunks[c] = jnp.where(r >= lane + (c - mask_from) * LANES, chunks[c], -1e30)
            m_old = m_ref[e, q_rows]
            m_new = jnp.maximum(
                m_old, jnp.max(functools.reduce(jnp.maximum, chunks), axis=-1, keepdims=True))
            m_ref[e, q_rows] = m_new
            probs.append((jnp.exp(m_old - m_new), jnp.concatenate(
                [jnp.exp(ch - m_new).astype(BF16) for ch in chunks], axis=1)))
        for (e, q_rows, start, width, _), (alpha, p) in zip(specs, probs):
            acc_ref[e, q_rows] = alpha * acc_ref[e, q_rows] + jnp.dot(
                p, v_ref[0, keys(start, width), cols(e)], preferred_element_type=F32)

    def loop_body(kb, carry):
        for e in range(ATT_HEADS):
            steps([(e, slice(0, ATT_TQ), kb * ATT_TQ, ATT_TQ, None)])
        return carry

    lax.fori_loop(0, qi, loop_body, 0)
    diag = qi * ATT_TQ
    steps([spec for e in range(ATT_HEADS) for spec in (
        (e, slice(0, half), diag, half, 0),
        (e, slice(half, ATT_TQ), diag, ATT_TQ, half // LANES))])
    lane = lax.broadcasted_iota(jnp.int32, (ATT_TQ, LANES), 1)
    for pr in range(ATT_HEADS // 2):
        o0, o1 = [acc * (1.0 / acc[:, MLA_V:MLA_V + 1])
                  for acc in (acc_ref[2 * pr], acc_ref[2 * pr + 1])]
        o_ref[0, :, pr * LANES:(pr + 1) * LANES] = jnp.where(
            lane < MLA_V, o0, pltpu.roll(o1, MLA_V, 1)).astype(BF16)


def _attention(qc, kc, vc, batch, seq):
    slab_w = MLA_HEADS * LANES
    grp_w = ATT_HEADS * LANES
    out_w = ATT_HEADS * MLA_V
    seq_blk = pl.BlockSpec((1, seq, grp_w), lambda b, j, i: (b, 0, j))
    return pl.pallas_call(
        _attention_body,
        grid=(batch, MLA_HEADS // ATT_HEADS, seq // ATT_TQ),
        in_specs=[pl.BlockSpec((1, ATT_TQ, grp_w), lambda b, j, i: (b, i, j)),
                  seq_blk, seq_blk],
        out_specs=pl.BlockSpec((1, ATT_TQ, out_w), lambda b, j, i: (b, i, j)),
        out_shape=jax.ShapeDtypeStruct((batch, seq, MLA_HEADS * MLA_V), BF16),
        scratch_shapes=[pltpu.VMEM((ATT_HEADS, ATT_TQ, LANES), F32),
                        pltpu.VMEM((ATT_HEADS, ATT_TQ, LANES), F32)],
        compiler_params=pltpu.CompilerParams(
            dimension_semantics=("parallel", "parallel", "arbitrary"),
            vmem_limit_bytes=VMEM_LIMIT),
        name="latent_attention",
    )(qc.reshape(batch, seq, slab_w), kc.reshape(batch, seq, slab_w),
      vc.reshape(batch, seq, slab_w))


def _retention_decay():
    c = RET_CHUNK
    lg = jnp.log(1.0 - 2.0 ** (-5.0 - jnp.arange(RET_HEADS, dtype=F32)))
    idx = jnp.arange(c, dtype=F32)
    rel = idx[:, None] - idx[None, :]
    decay = jnp.where(rel >= 0, jnp.exp(jnp.maximum(rel, 0.0)[None] * lg[:, None, None]), 0.0)
    zeta = jnp.exp((c - 1 - idx)[None, :] * lg[:, None])
    xi = jnp.exp((idx + 1.0)[None, :] * lg[:, None])
    cdec = jnp.exp(c * lg)
    full = (RET_HEADS, c, RET_DIM)
    return jnp.stack([decay,
                      jnp.broadcast_to(zeta[:, :, None], full),
                      jnp.broadcast_to(xi[:, :, None], full),
                      jnp.broadcast_to(cdec[:, None, None], full)])


def _swapped_rope_cols(w):
    a, b = w[..., :HALF_ROPE], w[..., HALF_ROPE:]
    return jnp.concatenate([a, b, b, a], axis=-1)


def _layer_params(l, p, win_bf16):
    row = lambda g: g[l].reshape(1, -1)
    kr_cols = jnp.concatenate(
        [jnp.zeros((D_MODEL, MLA_NOPE), BF16),
         _swapped_rope_cols(win_bf16[l][:, IN_BASE:])], axis=-1)

    dq = MLA_NOPE + MLA_ROPE
    wuq = p["w_uq"][l].reshape(Q_LORA, MLA_HEADS, dq)
    wuq = jnp.concatenate([wuq[..., :MLA_NOPE], _swapped_rope_cols(wuq[..., MLA_NOPE:])], axis=-1)
    wuq = wuq.reshape(Q_LORA, MLA_HEADS * LANES).astype(BF16)
    wukv = p["w_ukv"][l].reshape(KV_LORA, MLA_HEADS, MLA_NOPE + MLA_V)
    wk = jnp.concatenate([wukv[..., :MLA_NOPE], jnp.zeros_like(wukv[..., :MLA_NOPE])], axis=-1)
    wk = wk.reshape(KV_LORA, MLA_HEADS * LANES).astype(BF16)
    wv = jnp.concatenate([wukv[..., MLA_NOPE:], jnp.zeros_like(wukv[..., :LANES - MLA_V])], axis=-1)
    wv = wv.reshape(KV_LORA, MLA_HEADS * LANES).astype(BF16)

    scale = dq ** -0.5
    zeros64 = jnp.zeros((MLA_NOPE,), F32)
    gq = jnp.concatenate([p["qn_nope"][l], _swapped_rope_cols(p["qn_rope"][l])]) * scale
    gk = jnp.concatenate([p["kn_nope"][l], zeros64])
    gkr = jnp.concatenate([zeros64, _swapped_rope_cols(p["kn_rope"][l])])
    one_lane = jnp.zeros((LANES,), F32).at[MLA_V].set(1.0)
    vecs = jnp.stack([jnp.tile(r, MLA_HEADS) for r in (gq, gk, gkr, one_lane)]
                     + [jnp.zeros((MLA_HEADS * LANES,), F32)] * 4)
    return dict(
        ffn1_gain=row(p["ffn1_norm"]),
        ffn2_gain=row(p["ffn2_norm"]),
        inproj=(row(p["mix_norm"]), kr_cols, row(p["q_lat_norm"]), wuq, row(p["kv_lat_norm"]),
                wk, wv),
        vecs=vecs,
        gn=p["ret_head_norm"][l].reshape(RET_HEADS, 1, RET_DIM),
    )


def _group_matrix():
    idx = jnp.arange(2 * LANES)
    slab, lane = idx // LANES, idx % LANES
    same = slab[:, None] == slab[None, :]
    nope = (lane[:, None] < MLA_NOPE) & (lane[None, :] < MLA_NOPE)
    rope = ((lane[:, None] >= MLA_NOPE) & (lane[:, None] < MLA_NOPE + MLA_ROPE)
            & (lane[None, :] >= MLA_NOPE))
    g = jnp.where(nope, 1.0 / MLA_NOPE, jnp.where(rope, 1.0 / MLA_ROPE, 0.0))
    return jnp.where(same, g, 0.0).astype(BF16)


def kernel(x, positions, ffn1_norm, ffn1_w_gate, ffn1_w_up, ffn1_w_down, mix_norm, w_in, ret_head_norm, q_lat_norm, w_uq, kv_lat_norm, w_ukv, qn_nope, qn_rope, kn_nope, kn_rope, w_o, ffn2_norm, ffn2_w_gate, ffn2_w_up, ffn2_w_down):
    p = dict(ffn1_norm=ffn1_norm, ffn1_w_gate=ffn1_w_gate, ffn1_w_up=ffn1_w_up,
             ffn1_w_down=ffn1_w_down, mix_norm=mix_norm, w_in=w_in,
             ret_head_norm=ret_head_norm, q_lat_norm=q_lat_norm, w_uq=w_uq,
             kv_lat_norm=kv_lat_norm, w_ukv=w_ukv, qn_nope=qn_nope, qn_rope=qn_rope,
             kn_nope=kn_nope, kn_rope=kn_rope, w_o=w_o, ffn2_norm=ffn2_norm,
             ffn2_w_gate=ffn2_w_gate, ffn2_w_up=ffn2_w_up, ffn2_w_down=ffn2_w_down)
    batch, seq, d = x.shape
    depth = w_in.shape[0]
    dec = _retention_decay()
    gmat = _group_matrix()
    ffn1_stacks = [ffn1_w_gate, ffn1_w_up, ffn1_w_down]
    ffn2_stacks = [ffn2_w_gate, ffn2_w_up, ffn2_w_down]
    w_ffn1 = _cast_stacks(ffn1_stacks[:2], 1) + _cast_stacks(ffn1_stacks[2:], 1)
    win_bf16 = w_in.astype(BF16)
    tabs = None
    xf = x.reshape(batch * seq, d)
    for l in range(depth):
        lp = _layer_params(l, p, win_bf16)
        xf, new_tabs, side_w = _ffn(xf, lp["ffn1_gain"], *w_ffn1,
                                    side=(ffn2_stacks + [w_o], l),
                                    positions=positions if l == 0 else None)
        tabs = new_tabs if l == 0 else tabs
        w_ffn2, wo = side_w[:3], side_w[3]
        gmix, kr_cols, gq, wuq, gkv, wk, wv = lp["inproj"]
        win_main = win_bf16[l, :, :IN_BASE]
        rq, rk, rv, rg, qc, kc, v = _inproj(xf, tabs, gmix, win_main, kr_cols, gq, wuq, gkv,
                                            wk, wv, gmat, lp["vecs"])
        y_ret = _retention(rq, rk, rv, rg, dec, lp["gn"], batch, seq)
        y_mla = _attention(qc, kc, v, batch, seq)
        xf, _, w_ffn1 = _ffn(
            xf, lp["ffn2_gain"], *w_ffn2,
            proj=(y_ret.reshape(batch * seq, -1), y_mla.reshape(batch * seq, -1), wo),
            side=(ffn1_stacks, l + 1) if l + 1 < depth else None)
    return xf.reshape(batch, seq, d)
```

```python
import functools

import jax
import jax.numpy as jnp
from jax import lax
from jax.experimental import pallas as pl
from jax.experimental.pallas import tpu as pltpu

D_MODEL = 1024
D_FF = 2816
RET_HEADS = 4
RET_DIM = 128
RET_CHUNK = 128
MLA_HEADS = 8
MLA_NOPE = 64
MLA_ROPE = 32
MLA_V = 64
Q_LORA = 256
KV_LORA = 128
ROPE_THETA = 10000.0
EPS = 1e-6

LANES = 128
BF16_ROWS = 16
RET_W = RET_HEADS * RET_DIM
IN_BASE = 4 * RET_W + Q_LORA + KV_LORA
IN_W = IN_BASE + LANES
HALF_ROPE = MLA_ROPE // 2
SLAB_SHIFT = LANES - MLA_ROPE

TOK_TILE = 512
FF_TILE = 256
CAST_ROWS = 256
ATT_TQ = 1024
ATT_HEADS = 4
DIAG_SKEW = 2
RET_TILE = 1024
V7X_VMEM_BYTES = 64 * 1024 * 1024
VMEM_LIMIT = V7X_VMEM_BYTES * 3 // 4
MASK_VALUE = -1e30

F32 = jnp.float32
BF16 = jnp.bfloat16


def _rms(x, gain):
    return x * lax.rsqrt(jnp.mean(x * x, axis=-1, keepdims=True) + EPS) * gain


def _resident(shape):
    nd = len(shape)
    return pl.BlockSpec(shape, lambda *_: (0,) * nd)


def _tables_rows(pos_ref, c_ref, o_ref, rows):
    ang = pos_ref[rows, :].astype(F32) * c_ref[0:1, :]
    c, s = jnp.cos(ang), jnp.sin(ang)
    low = lax.broadcasted_iota(jnp.int32, c.shape, 1) < RET_DIM // 2
    o_ref[0, rows] = jnp.where(low, c, pltpu.roll(c, RET_DIM // 2, 1))
    o_ref[1, rows] = jnp.where(low, s, pltpu.roll(s, RET_DIM // 2, 1)) * c_ref[1:2, :]
    o_ref[2, rows] = c_ref[2:3, :] + c_ref[3:4, :] * c
    o_ref[3, rows] = s * c_ref[4:5, :]


def _table_consts():
    inv_r = 1.0 / (ROPE_THETA ** (jnp.arange(0, RET_DIM, 2, dtype=F32) / RET_DIM))
    inv_m = 1.0 / (ROPE_THETA ** (jnp.arange(0, MLA_ROPE, 2, dtype=F32) / MLA_ROPE))
    half = RET_DIM // 2
    z = lambda n: jnp.zeros((n,), F32)
    o = lambda n: jnp.ones((n,), F32)
    return jnp.stack([
        jnp.concatenate([inv_r, inv_m, inv_m, z(MLA_ROPE)]),
        jnp.concatenate([-o(half), o(half)]),
        jnp.concatenate([o(MLA_NOPE), z(LANES - MLA_NOPE)]),
        jnp.concatenate([z(MLA_NOPE), o(MLA_ROPE), z(MLA_ROPE)]),
        jnp.concatenate([z(MLA_NOPE), -o(HALF_ROPE), o(HALF_ROPE), z(MLA_ROPE)]),
        z(LANES), z(LANES), z(LANES)])


def _ffn_body(*refs, with_proj, with_tables, n_side):
    refs = list(refs)
    take = lambda n: [refs.pop(0) for _ in range(n)]
    (x_ref,) = take(1)
    if with_proj:
        yr_ref, ym_ref, wo_ref = take(3)
    g_ref, wg_ref, wu_ref, wd_ref = take(4)
    if with_tables:
        pos_ref, c_ref = take(2)
    side_in = take(n_side)
    (o_ref,) = take(1)
    if with_tables:
        (tab_ref,) = take(1)
    side_out = take(n_side)
    (a_ref,) = take(1)

    x = x_ref[...]
    if with_proj:
        x = (x + jnp.dot(yr_ref[...], wo_ref[0:RET_W, :], preferred_element_type=F32)
             + jnp.dot(ym_ref[...], wo_ref[RET_W:, :], preferred_element_type=F32))
    h = _rms(x, g_ref[...]).astype(BF16)
    n_chunks = D_FF // FF_TILE
    tab_rows = x.shape[0] // n_chunks // 8 * 8
    for c in range(n_chunks):
        cols = slice(c * FF_TILE, (c + 1) * FF_TILE)
        g = jnp.dot(h, wg_ref[:, cols], preferred_element_type=F32)
        u = jnp.dot(h, wu_ref[:, cols], preferred_element_type=F32)
        a_ref[:, cols] = (g * jax.nn.sigmoid(g) * u).astype(BF16)
        if with_tables:
            stop = x.shape[0] if c == n_chunks - 1 else (c + 1) * tab_rows
            _tables_rows(pos_ref, c_ref, tab_ref, slice(c * tab_rows, stop))
    for src, dst in zip(side_in, side_out):
        dst[...] = src[...].astype(BF16)
    y = jnp.dot(a_ref[...], wd_ref[...], preferred_element_type=F32)
    o_ref[...] = x + 0.5 * y


def _cast_body(*refs):
    n = len(refs) // 2
    for src, dst in zip(refs[:n], refs[n:]):
        dst[...] = src[...].astype(BF16)


def _cast_stacks(ws, layers):
    _, r, c = ws[0].shape
    tr = CAST_ROWS
    blk = pl.BlockSpec((1, tr, c), lambda l, i: (l, i, 0))
    return pl.pallas_call(
        _cast_body,
        grid=(layers, r // tr),
        in_specs=[blk] * len(ws),
        out_specs=[blk] * len(ws),
        out_shape=[jax.ShapeDtypeStruct((layers, r, c), BF16)] * len(ws),
        compiler_params=pltpu.CompilerParams(
            dimension_semantics=("parallel", "parallel"), vmem_limit_bytes=VMEM_LIMIT),
        name="cast_weights",
    )(*ws)


def _weight_block(w):
    if w.ndim == 2:
        return _resident(w.shape)
    return pl.BlockSpec((None,) + w.shape[1:], lambda *_: (0, 0, 0))


def _side_cast_specs(w, l, steps):
    _, r, c = w.shape
    hold = 1
    while (r * hold) % (steps * BF16_ROWS) or steps % hold:
        hold += 1
    rows = r * hold // steps
    return (pl.BlockSpec((None, rows, c), lambda i: (l, i // hold, 0)),
            pl.BlockSpec((rows, c), lambda i: (i // hold, 0)),
            jax.ShapeDtypeStruct((r, c), BF16))


def _ffn(x, gain, wg, wu, wd, proj=None, positions=None, side=None):
    t = x.shape[0]
    tm = TOK_TILE
    steps = t // tm
    row = lambda w: pl.BlockSpec((tm, w), lambda i: (i, 0))
    ins, specs = [x], [row(D_MODEL)]
    if proj is not None:
        yr, ym, wo = proj
        ins += [yr, ym, wo]
        specs += [row(RET_W), row(MLA_HEADS * MLA_V), _resident(wo.shape)]
    ins += [gain, wg, wu, wd]
    specs += [_resident(gain.shape), _weight_block(wg), _weight_block(wu), _weight_block(wd)]
    out_specs = [row(D_MODEL)]
    out_shape = [jax.ShapeDtypeStruct((t, D_MODEL), F32)]
    if positions is not None:
        ins += [positions.reshape(t, 1), _table_consts()]
        specs += [pl.BlockSpec((tm, 1), lambda i: (i, 0)), _resident((8, LANES))]
        out_specs.append(pl.BlockSpec((4, tm, LANES), lambda i: (0, i, 0)))
        out_shape.append(jax.ShapeDtypeStruct((4, t, LANES), F32))
    side_ws, side_layer = side if side is not None else ([], 0)
    for w in side_ws:
        in_spec, out_spec, shape = _side_cast_specs(w, side_layer, steps)
        ins.append(w)
        specs.append(in_spec)
        out_specs.append(out_spec)
        out_shape.append(shape)
    outs = pl.pallas_call(
        functools.partial(_ffn_body, with_proj=proj is not None,
                          with_tables=positions is not None, n_side=len(side_ws)),
        grid=(steps,),
        in_specs=specs,
        out_specs=out_specs,
        out_shape=out_shape,
        scratch_shapes=[pltpu.VMEM((tm, D_FF), BF16)],
        compiler_params=pltpu.CompilerParams(
            dimension_semantics=("arbitrary",), vmem_limit_bytes=VMEM_LIMIT),
        name="ffn_proj" if proj is not None else "ffn",
    )(*ins)
    tabs = outs[1] if positions is not None else None
    return outs[0], tabs, list(outs[len(outs) - len(side_ws):])


def _group_mean(x, gmat):
    x2 = x * x
    hi = x2.astype(BF16)
    lo = (x2 - hi.astype(F32)).astype(BF16)
    return (jnp.dot(hi, gmat, preferred_element_type=F32)
            + jnp.dot(lo, gmat, preferred_element_type=F32))


def _inproj_body(x_ref, gmix_ref, win_ref, krc_ref, tab_ref, gq_ref, wuq_ref, gkv_ref, wk_ref,
                 wv_ref, gmat_ref, vec_ref,
                 rq_ref, rk_ref, rv_ref, rg_ref, qc_ref, kc_ref, v_ref, wlat_ref):
    lat0 = 4 * RET_W

    @pl.when(pl.program_id(0) == 0)
    def _():
        wlat_ref[:, 0:IN_BASE - lat0] = win_ref[:, lat0:]
        wlat_ref[:, IN_BASE - lat0:] = krc_ref[...]

    h = _rms(x_ref[...], gmix_ref[...]).astype(BF16)
    cos_r, sin_r, cos_m, sin_m = tab_ref[0], tab_ref[1], tab_ref[2], tab_ref[3]
    lat = jnp.dot(h, wlat_ref[...], preferred_element_type=F32)
    c_q = lat[:, 0:Q_LORA]
    c_kv = lat[:, Q_LORA:Q_LORA + KV_LORA]
    kr = lat[:, Q_LORA + KV_LORA:]

    cqn = _rms(c_q, gq_ref[...]).astype(BF16)
    ckvn = _rms(c_kv, gkv_ref[...]).astype(BF16)
    qraw = jnp.dot(cqn, wuq_ref[...], preferred_element_type=F32)
    kraw = jnp.dot(ckvn, wk_ref[...], preferred_element_type=F32)
    v_ref[...] = (jnp.dot(ckvn, wv_ref[...], preferred_element_type=F32)
                  + vec_ref[3:4, :]).astype(BF16)

    lane = lax.broadcasted_iota(jnp.int32, (1, LANES), 1)
    in_rope = (lane >= MLA_NOPE) & (lane < MLA_NOPE + MLA_ROPE)
    ss = jnp.sum(jnp.where(in_rope, kr * kr, 0.0), axis=-1, keepdims=True) * (1.0 / MLA_ROPE)
    krn = kr * lax.rsqrt(ss + EPS) * vec_ref[2:3, 0:LANES]
    kro = krn * cos_m + pltpu.roll(krn, SLAB_SHIFT, 1) * sin_m

    gmat = gmat_ref[...]
    for pr in range(MLA_HEADS // 2):
        cols = slice(pr * 2 * LANES, (pr + 1) * 2 * LANES)
        xs = qraw[:, cols]
        y = xs * lax.rsqrt(_group_mean(xs, gmat) + EPS) * vec_ref[0:1, 0:2 * LANES]
        for e in range(2):
            sl = slice(e * LANES, (e + 1) * LANES)
            dst = slice((2 * pr + e) * LANES, (2 * pr + e + 1) * LANES)
            ys = y[:, sl]
            qc_ref[:, dst] = (ys * cos_m + pltpu.roll(ys, SLAB_SHIFT, 1) * sin_m).astype(BF16)
            ks = kraw[:, dst]
            ms = jnp.sum(ks * ks, axis=-1, keepdims=True) * (1.0 / MLA_NOPE)
            kc_ref[:, dst] = (ks * lax.rsqrt(ms + EPS) * vec_ref[1:2, 0:LANES] + kro).astype(BF16)

    def ret_cols(i):
        return jnp.dot(h, win_ref[:, i * RET_W:(i + 1) * RET_W], preferred_element_type=F32)

    ret_scale = RET_DIM ** -0.5
    for dst_ref, scale, i in ((rq_ref, None, 0), (rk_ref, ret_scale, 1)):
        x = ret_cols(i)
        for hd in range(RET_HEADS):
            sl = slice(hd * RET_DIM, (hd + 1) * RET_DIM)
            roped = x[:, sl] * cos_r + pltpu.roll(x[:, sl], RET_DIM // 2, 1) * sin_r
            dst_ref[:, sl] = (roped if scale is None else roped * scale).astype(BF16)
    rv_ref[...] = ret_cols(2).astype(BF16)
    rg_ref[...] = ret_cols(3)


def _inproj(x, tabs, gmix, win, krc, gq, wuq, gkv, wk, wv, gmat, vecs):
    t = x.shape[0]
    tm = TOK_TILE
    row = lambda w: pl.BlockSpec((tm, w), lambda i: (i, 0))
    slab_w = MLA_HEADS * LANES
    outs = [(RET_W, BF16), (RET_W, BF16), (RET_W, BF16), (RET_W, F32),
            (slab_w, BF16), (slab_w, BF16), (slab_w, BF16)]
    return pl.pallas_call(
        _inproj_body,
        grid=(t // tm,),
        in_specs=[row(D_MODEL), _resident(gmix.shape),
                  _resident(win.shape), _resident(krc.shape),
                  pl.BlockSpec((4, tm, LANES), lambda i: (0, i, 0)),
                  _resident(gq.shape), _resident(wuq.shape), _resident(gkv.shape),
                  _resident(wk.shape), _resident(wv.shape), _resident(gmat.shape),
                  _resident(vecs.shape)],
        out_specs=[row(w) for w, _ in outs],
        out_shape=[jax.ShapeDtypeStruct((t, w), dt) for w, dt in outs],
        scratch_shapes=[pltpu.VMEM((D_MODEL, IN_W - 4 * RET_W), BF16)],
        compiler_params=pltpu.CompilerParams(
            dimension_semantics=("arbitrary",), vmem_limit_bytes=VMEM_LIMIT),
        name="inproj",
    )(x, gmix, win, krc, tabs, gq, wuq, gkv, wk, wv, gmat, vecs)


def _retention_body(q_ref, k_ref, v_ref, g_ref, dec_ref, gn_ref, o_ref,
                    state_ref, s_ref, kv_ref, prev_ref):
    @pl.when(pl.program_id(1) == 0)
    def _():
        state_ref[...] = jnp.zeros_like(state_ref)

    c = RET_CHUNK
    nt = (((1,), (1,)), ((), ()))
    n = RET_TILE // c
    blocks = [(hd, ci) for ci in range(n) for hd in range(RET_HEADS)]
    tile = lambda hd, ci: (slice(ci * c, (ci + 1) * c), slice(hd * RET_DIM, (hd + 1) * RET_DIM))

    for hd, ci in blocks:
        rows, cols = tile(hd, ci)
        k = k_ref[0, rows, cols]
        v = v_ref[0, rows, cols]
        s = lax.dot_general(q_ref[0, rows, cols], k, nt, preferred_element_type=F32)
        s_ref[hd * n + ci] = (s * dec_ref[0, hd]).astype(BF16)
        kz = (k.astype(F32) * dec_ref[1, hd]).astype(BF16)
        kv_ref[hd * n + ci] = jnp.dot(kz.T, v, preferred_element_type=F32)
    for hd in range(RET_HEADS):
        st = state_ref[hd]
        for ci in range(n):
            prev_ref[hd * n + ci] = st.astype(BF16)
            st = st * dec_ref[3, hd] + kv_ref[hd * n + ci]
        state_ref[hd] = st
    for hd, ci in blocks:
        rows, cols = tile(hd, ci)
        q = q_ref[0, rows, cols]
        y = (jnp.dot(s_ref[hd * n + ci], v_ref[0, rows, cols], preferred_element_type=F32)
             + jnp.dot(q, prev_ref[hd * n + ci], preferred_element_type=F32) * dec_ref[2, hd])
        g = g_ref[0, rows, cols]
        o_ref[0, rows, cols] = (_rms(y, gn_ref[hd]) * (g * jax.nn.sigmoid(g))).astype(BF16)


def _retention(rq, rk, rv, rg, dec, gn, batch, seq):
    shp = (batch, seq, RET_W)
    nblk = RET_HEADS * (RET_TILE // RET_CHUNK)
    blk = pl.BlockSpec((1, RET_TILE, RET_W), lambda b, s: (b, s, 0))
    return pl.pallas_call(
        _retention_body,
        grid=(batch, seq // RET_TILE),
        in_specs=[blk, blk, blk, blk, _resident(dec.shape), _resident(gn.shape)],
        out_specs=blk,
        out_shape=jax.ShapeDtypeStruct(shp, BF16),
        scratch_shapes=[pltpu.VMEM((RET_HEADS, RET_DIM, RET_DIM), F32),
                        pltpu.VMEM((nblk, RET_CHUNK, RET_CHUNK), BF16),
                        pltpu.VMEM((nblk, RET_DIM, RET_DIM), F32),
                        pltpu.VMEM((nblk, RET_DIM, RET_DIM), BF16)],
        compiler_params=pltpu.CompilerParams(
            dimension_semantics=("parallel", "arbitrary"), vmem_limit_bytes=VMEM_LIMIT),
        name="retention",
    )(rq.reshape(shp), rk.reshape(shp), rv.reshape(shp), rg.reshape(shp), dec, gn)


def _attention_body(q_ref, k_ref, v_ref, o_ref, m_ref, acc_ref):
    qi = pl.program_id(2)
    nt = (((1,), (1,)), ((), ()))
    half = ATT_TQ // 2
    m_ref[...] = jnp.full_like(m_ref, MASK_VALUE)
    acc_ref[...] = jnp.zeros_like(acc_ref)

    def steps(specs, skew):
        cols = lambda e: slice(e * LANES, (e + 1) * LANES)
        keys = lambda start, width: pl.ds(pl.multiple_of(start, half), width)

        def score(spec):
            e, q_rows, start, width, _ = spec
            return lax.dot_general(q_ref[0, q_rows, cols(e)], k_ref[0, keys(start, width), cols(e)],
                                   nt, preferred_element_type=F32)

        def softmax(spec, s):
            e, q_rows, _, width, mask_from = spec
            chunks = [s[:, c * LANES:(c + 1) * LANES] for c in range(width // LANES)]
            if mask_from is not None:
                r = lax.broadcasted_iota(jnp.int32, chunks[0].shape, 0)
                lane = lax.broadcasted_iota(jnp.int32, chunks[0].shape, 1)
                for c in range(mask_from, len(chunks)):
                    chunks[c] = jnp.where(r >= lane + (c - mask_from) * LANES, chunks[c],
                                          MASK_VALUE)
            m_old = m_ref[e, q_rows]
            m_new = jnp.maximum(
                m_old, jnp.max(functools.reduce(jnp.maximum, chunks), axis=-1, keepdims=True))
            m_ref[e, q_rows] = m_new
            return jnp.exp(m_old - m_new), jnp.concatenate(
                [jnp.exp(ch - m_new).astype(BF16) for ch in chunks], axis=1)

        def value(spec, alpha, p):
            e, q_rows, start, width, _ = spec
            acc_ref[e, q_rows] = alpha * acc_ref[e, q_rows] + jnp.dot(
                p, v_ref[0, keys(start, width), cols(e)], preferred_element_type=F32)

        n = len(specs)
        scores, probs = {}, {}
        for i in range(n + 2 * skew):
            if i < n:
                scores[i] = score(specs[i])
            if 0 <= i - skew < n:
                probs[i - skew] = softmax(specs[i - skew], scores.pop(i - skew))
            if 0 <= i - 2 * skew < n:
                value(specs[i - 2 * skew], *probs.pop(i - 2 * skew))

    def loop_body(kb, carry):
        for e in range(ATT_HEADS):
            steps([(e, slice(0, ATT_TQ), kb * ATT_TQ, ATT_TQ, None)], 1)
        return carry

    lax.fori_loop(0, qi, loop_body, 0)
    diag = qi * ATT_TQ
    steps([spec for e in range(ATT_HEADS) for spec in (
        (e, slice(0, half), diag, half, 0),
        (e, slice(half, ATT_TQ), diag, ATT_TQ, half // LANES))], DIAG_SKEW)
    lane = lax.broadcasted_iota(jnp.int32, (ATT_TQ, LANES), 1)
    for pr in range(ATT_HEADS // 2):
        o0, o1 = [acc * (1.0 / acc[:, MLA_V:MLA_V + 1])
                  for acc in (acc_ref[2 * pr], acc_ref[2 * pr + 1])]
        o_ref[0, :, pr * LANES:(pr + 1) * LANES] = jnp.where(
            lane < MLA_V, o0, pltpu.roll(o1, MLA_V, 1)).astype(BF16)


def _attention(qc, kc, vc, batch, seq):
    slab_w = MLA_HEADS * LANES
    grp_w = ATT_HEADS * LANES
    out_w = ATT_HEADS * MLA_V
    seq_blk = pl.BlockSpec((1, seq, grp_w), lambda b, j, i: (b, 0, j))
    return pl.pallas_call(
        _attention_body,
        grid=(batch, MLA_HEADS // ATT_HEADS, seq // ATT_TQ),
        in_specs=[pl.BlockSpec((1, ATT_TQ, grp_w), lambda b, j, i: (b, i, j)),
                  seq_blk, seq_blk],
        out_specs=pl.BlockSpec((1, ATT_TQ, out_w), lambda b, j, i: (b, i, j)),
        out_shape=jax.ShapeDtypeStruct((batch, seq, MLA_HEADS * MLA_V), BF16),
        scratch_shapes=[pltpu.VMEM((ATT_HEADS, ATT_TQ, LANES), F32),
                        pltpu.VMEM((ATT_HEADS, ATT_TQ, LANES), F32)],
        compiler_params=pltpu.CompilerParams(
            dimension_semantics=("parallel", "parallel", "arbitrary"),
            vmem_limit_bytes=VMEM_LIMIT),
        name="latent_attention",
    )(qc.reshape(batch, seq, slab_w), kc.reshape(batch, seq, slab_w),
      vc.reshape(batch, seq, slab_w))


def _retention_decay():
    c = RET_CHUNK
    lg = jnp.log(1.0 - 2.0 ** (-5.0 - jnp.arange(RET_HEADS, dtype=F32)))
    idx = jnp.arange(c, dtype=F32)
    rel = idx[:, None] - idx[None, :]
    decay = jnp.where(rel >= 0, jnp.exp(jnp.maximum(rel, 0.0)[None] * lg[:, None, None]), 0.0)
    zeta = jnp.exp((c - 1 - idx)[None, :] * lg[:, None])
    xi = jnp.exp((idx + 1.0)[None, :] * lg[:, None])
    cdec = jnp.exp(c * lg)
    full = (RET_HEADS, c, RET_DIM)
    return jnp.stack([decay,
                      jnp.broadcast_to(zeta[:, :, None], full),
                      jnp.broadcast_to(xi[:, :, None], full),
                      jnp.broadcast_to(cdec[:, None, None], full)])


def _swapped_rope_cols(w):
    a, b = w[..., :HALF_ROPE], w[..., HALF_ROPE:]
    return jnp.concatenate([a, b, b, a], axis=-1)


def _layer_params(l, p, win_bf16):
    row = lambda g: g[l].reshape(1, -1)
    kr_cols = jnp.concatenate(
        [jnp.zeros((D_MODEL, MLA_NOPE), BF16),
         _swapped_rope_cols(win_bf16[l][:, IN_BASE:])], axis=-1)

    dq = MLA_NOPE + MLA_ROPE
    wuq = p["w_uq"][l].reshape(Q_LORA, MLA_HEADS, dq)
    wuq = jnp.concatenate([wuq[..., :MLA_NOPE], _swapped_rope_cols(wuq[..., MLA_NOPE:])], axis=-1)
    wuq = wuq.reshape(Q_LORA, MLA_HEADS * LANES).astype(BF16)
    wukv = p["w_ukv"][l].reshape(KV_LORA, MLA_HEADS, MLA_NOPE + MLA_V)
    wk = jnp.concatenate([wukv[..., :MLA_NOPE], jnp.zeros_like(wukv[..., :MLA_NOPE])], axis=-1)
    wk = wk.reshape(KV_LORA, MLA_HEADS * LANES).astype(BF16)
    wv = jnp.concatenate([wukv[..., MLA_NOPE:], jnp.zeros_like(wukv[..., :LANES - MLA_V])], axis=-1)
    wv = wv.reshape(KV_LORA, MLA_HEADS * LANES).astype(BF16)

    scale = dq ** -0.5
    zeros64 = jnp.zeros((MLA_NOPE,), F32)
    gq = jnp.concatenate([p["qn_nope"][l], _swapped_rope_cols(p["qn_rope"][l])]) * scale
    gk = jnp.concatenate([p["kn_nope"][l], zeros64])
    gkr = jnp.concatenate([zeros64, _swapped_rope_cols(p["kn_rope"][l])])
    one_lane = jnp.zeros((LANES,), F32).at[MLA_V].set(1.0)
    vecs = jnp.stack([jnp.tile(r, MLA_HEADS) for r in (gq, gk, gkr, one_lane)]
                     + [jnp.zeros((MLA_HEADS * LANES,), F32)] * 4)
    return dict(
        ffn1_gain=row(p["ffn1_norm"]),
        ffn2_gain=row(p["ffn2_norm"]),
        inproj=(row(p["mix_norm"]), kr_cols, row(p["q_lat_norm"]), wuq, row(p["kv_lat_norm"]),
                wk, wv),
        vecs=vecs,
        gn=p["ret_head_norm"][l].reshape(RET_HEADS, 1, RET_DIM),
    )


def _group_matrix():
    idx = jnp.arange(2 * LANES)
    slab, lane = idx // LANES, idx % LANES
    same = slab[:, None] == slab[None, :]
    nope = (lane[:, None] < MLA_NOPE) & (lane[None, :] < MLA_NOPE)
    rope = ((lane[:, None] >= MLA_NOPE) & (lane[:, None] < MLA_NOPE + MLA_ROPE)
            & (lane[None, :] >= MLA_NOPE))
    g = jnp.where(nope, 1.0 / MLA_NOPE, jnp.where(rope, 1.0 / MLA_ROPE, 0.0))
    return jnp.where(same, g, 0.0).astype(BF16)


def kernel(x, positions, ffn1_norm, ffn1_w_gate, ffn1_w_up, ffn1_w_down, mix_norm, w_in, ret_head_norm, q_lat_norm, w_uq, kv_lat_norm, w_ukv, qn_nope, qn_rope, kn_nope, kn_rope, w_o, ffn2_norm, ffn2_w_gate, ffn2_w_up, ffn2_w_down):
    p = dict(ffn1_norm=ffn1_norm, ffn1_w_gate=ffn1_w_gate, ffn1_w_up=ffn1_w_up,
             ffn1_w_down=ffn1_w_down, mix_norm=mix_norm, w_in=w_in,
             ret_head_norm=ret_head_norm, q_lat_norm=q_lat_norm, w_uq=w_uq,
             kv_lat_norm=kv_lat_norm, w_ukv=w_ukv, qn_nope=qn_nope, qn_rope=qn_rope,
             kn_nope=kn_nope, kn_rope=kn_rope, w_o=w_o, ffn2_norm=ffn2_norm,
             ffn2_w_gate=ffn2_w_gate, ffn2_w_up=ffn2_w_up, ffn2_w_down=ffn2_w_down)
    batch, seq, d = x.shape
    depth = w_in.shape[0]
    assert d == D_MODEL and positions.shape == (batch, seq)
    assert seq % ATT_TQ == 0 and seq % RET_TILE == 0 and (batch * seq) % TOK_TILE == 0
    assert w_in.shape[1:] == (D_MODEL, IN_BASE + MLA_ROPE) and ffn1_w_gate.shape[1:] == (d, D_FF)
    dec = _retention_decay()
    gmat = _group_matrix()
    ffn1_stacks = [ffn1_w_gate, ffn1_w_up, ffn1_w_down]
    ffn2_stacks = [ffn2_w_gate, ffn2_w_up, ffn2_w_down]
    w_ffn1 = _cast_stacks(ffn1_stacks[:2], 1) + _cast_stacks(ffn1_stacks[2:], 1)
    win_bf16 = w_in.astype(BF16)
    tabs = None
    xf = x.reshape(batch * seq, d)
    for l in range(depth):
        lp = _layer_params(l, p, win_bf16)
        xf, new_tabs, side_w = _ffn(xf, lp["ffn1_gain"], *w_ffn1,
                                    side=(ffn2_stacks + [w_o], l),
                                    positions=positions if l == 0 else None)
        tabs = new_tabs if l == 0 else tabs
        w_ffn2, wo = side_w[:3], side_w[3]
        gmix, kr_cols, gq, wuq, gkv, wk, wv = lp["inproj"]
        win_main = win_bf16[l, :, :IN_BASE]
        rq, rk, rv, rg, qc, kc, v = _inproj(xf, tabs, gmix, win_main, kr_cols, gq, wuq, gkv,
                                            wk, wv, gmat, lp["vecs"])
        y_ret = _retention(rq, rk, rv, rg, dec, lp["gn"], batch, seq)
        y_mla = _attention(qc, kc, v, batch, seq)
        xf, _, w_ffn1 = _ffn(
            xf, lp["ffn2_gain"], *w_ffn2,
            proj=(y_ret.reshape(batch * seq, -1), y_mla.reshape(batch * seq, -1), wo),
            side=(ffn1_stacks, l + 1) if l + 1 < depth else None)
    return xf.reshape(batch, seq, d)
```

```python
import functools

import jax
import jax.numpy as jnp
from jax import lax
from jax.experimental import pallas as pl
from jax.experimental.pallas import tpu as pltpu

D_MODEL = 1024
D_FF = 2816
RET_HEADS = 4
RET_DIM = 128
RET_CHUNK = 128
MLA_HEADS = 8
MLA_NOPE = 64
MLA_ROPE = 32
MLA_V = 64
Q_LORA = 256
KV_LORA = 128
ROPE_THETA = 10000.0
EPS = 1e-6

LANES = 128
BF16_ROWS = 16
RET_W = RET_HEADS * RET_DIM
IN_BASE = 4 * RET_W + Q_LORA + KV_LORA
IN_W = IN_BASE + LANES
HALF_ROPE = MLA_ROPE // 2
SLAB_SHIFT = LANES - MLA_ROPE

TOK_TILE = 512
FF_TILE = 256
CAST_ROWS = 256
ATT_TQ = 1024
ATT_HEADS = 4
DIAG_SKEW = 2
RET_TILE = 1024
V7X_VMEM_BYTES = 64 * 1024 * 1024
VMEM_LIMIT = V7X_VMEM_BYTES * 3 // 4
FFN_TILE = 1024
FFN_VMEM_LIMIT = V7X_VMEM_BYTES * 29 // 32
MASK_VALUE = -1e30

F32 = jnp.float32
BF16 = jnp.bfloat16


def _rms(x, gain):
    return x * lax.rsqrt(jnp.mean(x * x, axis=-1, keepdims=True) + EPS) * gain


def _resident(shape):
    nd = len(shape)
    return pl.BlockSpec(shape, lambda *_: (0,) * nd)


def _tables_rows(pos_ref, c_ref, o_ref, rows):
    ang = pos_ref[rows, :].astype(F32) * c_ref[0:1, :]
    c, s = jnp.cos(ang), jnp.sin(ang)
    low = lax.broadcasted_iota(jnp.int32, c.shape, 1) < RET_DIM // 2
    o_ref[0, rows] = jnp.where(low, c, pltpu.roll(c, RET_DIM // 2, 1))
    o_ref[1, rows] = jnp.where(low, s, pltpu.roll(s, RET_DIM // 2, 1)) * c_ref[1:2, :]
    o_ref[2, rows] = c_ref[2:3, :] + c_ref[3:4, :] * c
    o_ref[3, rows] = s * c_ref[4:5, :]


def _table_consts():
    inv_r = 1.0 / (ROPE_THETA ** (jnp.arange(0, RET_DIM, 2, dtype=F32) / RET_DIM))
    inv_m = 1.0 / (ROPE_THETA ** (jnp.arange(0, MLA_ROPE, 2, dtype=F32) / MLA_ROPE))
    half = RET_DIM // 2
    z = lambda n: jnp.zeros((n,), F32)
    o = lambda n: jnp.ones((n,), F32)
    return jnp.stack([
        jnp.concatenate([inv_r, inv_m, inv_m, z(MLA_ROPE)]),
        jnp.concatenate([-o(half), o(half)]),
        jnp.concatenate([o(MLA_NOPE), z(LANES - MLA_NOPE)]),
        jnp.concatenate([z(MLA_NOPE), o(MLA_ROPE), z(MLA_ROPE)]),
        jnp.concatenate([z(MLA_NOPE), -o(HALF_ROPE), o(HALF_ROPE), z(MLA_ROPE)]),
        z(LANES), z(LANES), z(LANES)])


def _ffn_body(*refs, with_proj, with_tables, n_side):
    refs = list(refs)
    take = lambda n: [refs.pop(0) for _ in range(n)]
    (x_ref,) = take(1)
    if with_proj:
        yr_ref, ym_ref, wo_ref = take(3)
    g_ref, wg_ref, wu_ref, wd_ref = take(4)
    if with_tables:
        pos_ref, c_ref = take(2)
    side_in = take(n_side)
    (o_ref,) = take(1)
    if with_tables:
        (tab_ref,) = take(1)
    side_out = take(n_side)
    (a_ref,) = take(1)

    x = x_ref[...]
    if with_proj:
        x = (x + jnp.dot(yr_ref[...], wo_ref[0:RET_W, :], preferred_element_type=F32)
             + jnp.dot(ym_ref[...], wo_ref[RET_W:, :], preferred_element_type=F32))
    h = _rms(x, g_ref[...]).astype(BF16)
    n_chunks = D_FF // FF_TILE
    tab_rows = x.shape[0] // n_chunks // 8 * 8
    for c in range(n_chunks):
        cols = slice(c * FF_TILE, (c + 1) * FF_TILE)
        g = jnp.dot(h, wg_ref[:, cols], preferred_element_type=F32)
        u = jnp.dot(h, wu_ref[:, cols], preferred_element_type=F32)
        a_ref[:, cols] = (g * jax.nn.sigmoid(g) * u).astype(BF16)
        if with_tables:
            stop = x.shape[0] if c == n_chunks - 1 else (c + 1) * tab_rows
            _tables_rows(pos_ref, c_ref, tab_ref, slice(c * tab_rows, stop))
    for src, dst in zip(side_in, side_out):
        dst[...] = src[...].astype(BF16)
    y = jnp.dot(a_ref[...], wd_ref[...], preferred_element_type=F32)
    o_ref[...] = x + 0.5 * y


def _cast_body(*refs):
    n = len(refs) // 2
    for src, dst in zip(refs[:n], refs[n:]):
        dst[...] = src[...].astype(BF16)


def _cast_stacks(ws, layers):
    _, r, c = ws[0].shape
    tr = CAST_ROWS
    blk = pl.BlockSpec((1, tr, c), lambda l, i: (l, i, 0))
    return pl.pallas_call(
        _cast_body,
        grid=(layers, r // tr),
        in_specs=[blk] * len(ws),
        out_specs=[blk] * len(ws),
        out_shape=[jax.ShapeDtypeStruct((layers, r, c), BF16)] * len(ws),
        compiler_params=pltpu.CompilerParams(
            dimension_semantics=("parallel", "parallel"), vmem_limit_bytes=VMEM_LIMIT),
        name="cast_weights",
    )(*ws)


def _weight_block(w):
    if w.ndim == 2:
        return _resident(w.shape)
    return pl.BlockSpec((None,) + w.shape[1:], lambda *_: (0, 0, 0))


def _side_cast_specs(w, l, steps):
    _, r, c = w.shape
    hold = 1
    while (r * hold) % (steps * BF16_ROWS) or steps % hold:
        hold += 1
    rows = r * hold // steps
    return (pl.BlockSpec((None, rows, c), lambda i: (l, i // hold, 0)),
            pl.BlockSpec((rows, c), lambda i: (i // hold, 0)),
            jax.ShapeDtypeStruct((r, c), BF16))


def _ffn(x, gain, wg, wu, wd, proj=None, positions=None, side=None):
    t = x.shape[0]
    tm = FFN_TILE
    steps = t // tm
    row = lambda w: pl.BlockSpec((tm, w), lambda i: (i, 0))
    ins, specs = [x], [row(D_MODEL)]
    if proj is not None:
        yr, ym, wo = proj
        ins += [yr, ym, wo]
        specs += [row(RET_W), row(MLA_HEADS * MLA_V), _resident(wo.shape)]
    ins += [gain, wg, wu, wd]
    specs += [_resident(gain.shape), _weight_block(wg), _weight_block(wu), _weight_block(wd)]
    out_specs = [row(D_MODEL)]
    out_shape = [jax.ShapeDtypeStruct((t, D_MODEL), F32)]
    if positions is not None:
        ins += [positions.reshape(t, 1), _table_consts()]
        specs += [pl.BlockSpec((tm, 1), lambda i: (i, 0)), _resident((8, LANES))]
        out_specs.append(pl.BlockSpec((4, tm, LANES), lambda i: (0, i, 0)))
        out_shape.append(jax.ShapeDtypeStruct((4, t, LANES), F32))
    side_ws, side_layer = side if side is not None else ([], 0)
    for w in side_ws:
        in_spec, out_spec, shape = _side_cast_specs(w, side_layer, steps)
        ins.append(w)
        specs.append(in_spec)
        out_specs.append(out_spec)
        out_shape.append(shape)
    outs = pl.pallas_call(
        functools.partial(_ffn_body, with_proj=proj is not None,
                          with_tables=positions is not None, n_side=len(side_ws)),
        grid=(steps,),
        in_specs=specs,
        out_specs=out_specs,
        out_shape=out_shape,
        scratch_shapes=[pltpu.VMEM((tm, D_FF), BF16)],
        compiler_params=pltpu.CompilerParams(
            dimension_semantics=("arbitrary",), vmem_limit_bytes=FFN_VMEM_LIMIT),
        name="ffn_proj" if proj is not None else "ffn",
    )(*ins)
    tabs = outs[1] if positions is not None else None
    return outs[0], tabs, list(outs[len(outs) - len(side_ws):])


def _group_mean(x, gmat):
    x2 = x * x
    hi = x2.astype(BF16)
    lo = (x2 - hi.astype(F32)).astype(BF16)
    return (jnp.dot(hi, gmat, preferred_element_type=F32)
            + jnp.dot(lo, gmat, preferred_element_type=F32))


def _inproj_body(x_ref, gmix_ref, win_ref, krc_ref, tab_ref, gq_ref, wuq_ref, gkv_ref, wk_ref,
                 wv_ref, gmat_ref, vec_ref,
                 rq_ref, rk_ref, rv_ref, rg_ref, qc_ref, kc_ref, v_ref, wlat_ref):
    lat0 = 4 * RET_W

    @pl.when(pl.program_id(0) == 0)
    def _():
        wlat_ref[:, 0:IN_BASE - lat0] = win_ref[:, lat0:]
        wlat_ref[:, IN_BASE - lat0:] = krc_ref[...]

    h = _rms(x_ref[...], gmix_ref[...]).astype(BF16)
    cos_r, sin_r, cos_m, sin_m = tab_ref[0], tab_ref[1], tab_ref[2], tab_ref[3]
    lat = jnp.dot(h, wlat_ref[...], preferred_element_type=F32)
    c_q = lat[:, 0:Q_LORA]
    c_kv = lat[:, Q_LORA:Q_LORA + KV_LORA]
    kr = lat[:, Q_LORA + KV_LORA:]

    cqn = _rms(c_q, gq_ref[...]).astype(BF16)
    ckvn = _rms(c_kv, gkv_ref[...]).astype(BF16)
    qraw = jnp.dot(cqn, wuq_ref[...], preferred_element_type=F32)
    kraw = jnp.dot(ckvn, wk_ref[...], preferred_element_type=F32)
    v_ref[...] = (jnp.dot(ckvn, wv_ref[...], preferred_element_type=F32)
                  + vec_ref[3:4, :]).astype(BF16)

    lane = lax.broadcasted_iota(jnp.int32, (1, LANES), 1)
    in_rope = (lane >= MLA_NOPE) & (lane < MLA_NOPE + MLA_ROPE)
    ss = jnp.sum(jnp.where(in_rope, kr * kr, 0.0), axis=-1, keepdims=True) * (1.0 / MLA_ROPE)
    krn = kr * lax.rsqrt(ss + EPS) * vec_ref[2:3, 0:LANES]
    kro = krn * cos_m + pltpu.roll(krn, SLAB_SHIFT, 1) * sin_m

    gmat = gmat_ref[...]
    for pr in range(MLA_HEADS // 2):
        cols = slice(pr * 2 * LANES, (pr + 1) * 2 * LANES)
        xs = qraw[:, cols]
        y = xs * lax.rsqrt(_group_mean(xs, gmat) + EPS) * vec_ref[0:1, 0:2 * LANES]
        for e in range(2):
            sl = slice(e * LANES, (e + 1) * LANES)
            dst = slice((2 * pr + e) * LANES, (2 * pr + e + 1) * LANES)
            ys = y[:, sl]
            qc_ref[:, dst] = (ys * cos_m + pltpu.roll(ys, SLAB_SHIFT, 1) * sin_m).astype(BF16)
            ks = kraw[:, dst]
            ms = jnp.sum(ks * ks, axis=-1, keepdims=True) * (1.0 / MLA_NOPE)
            kc_ref[:, dst] = (ks * lax.rsqrt(ms + EPS) * vec_ref[1:2, 0:LANES] + kro).astype(BF16)

    def ret_cols(i):
        return jnp.dot(h, win_ref[:, i * RET_W:(i + 1) * RET_W], preferred_element_type=F32)

    ret_scale = RET_DIM ** -0.5
    for dst_ref, scale, i in ((rq_ref, None, 0), (rk_ref, ret_scale, 1)):
        x = ret_cols(i)
        for hd in range(RET_HEADS):
            sl = slice(hd * RET_DIM, (hd + 1) * RET_DIM)
            roped = x[:, sl] * cos_r + pltpu.roll(x[:, sl], RET_DIM // 2, 1) * sin_r
            dst_ref[:, sl] = (roped if scale is None else roped * scale).astype(BF16)
    rv_ref[...] = ret_cols(2).astype(BF16)
    rg_ref[...] = ret_cols(3)


def _inproj(x, tabs, gmix, win, krc, gq, wuq, gkv, wk, wv, gmat, vecs):
    t = x.shape[0]
    tm = TOK_TILE
    row = lambda w: pl.BlockSpec((tm, w), lambda i: (i, 0))
    slab_w = MLA_HEADS * LANES
    outs = [(RET_W, BF16), (RET_W, BF16), (RET_W, BF16), (RET_W, F32),
            (slab_w, BF16), (slab_w, BF16), (slab_w, BF16)]
    return pl.pallas_call(
        _inproj_body,
        grid=(t // tm,),
        in_specs=[row(D_MODEL), _resident(gmix.shape),
                  _resident(win.shape), _resident(krc.shape),
                  pl.BlockSpec((4, tm, LANES), lambda i: (0, i, 0)),
                  _resident(gq.shape), _resident(wuq.shape), _resident(gkv.shape),
                  _resident(wk.shape), _resident(wv.shape), _resident(gmat.shape),
                  _resident(vecs.shape)],
        out_specs=[row(w) for w, _ in outs],
        out_shape=[jax.ShapeDtypeStruct((t, w), dt) for w, dt in outs],
        scratch_shapes=[pltpu.VMEM((D_MODEL, IN_W - 4 * RET_W), BF16)],
        compiler_params=pltpu.CompilerParams(
            dimension_semantics=("arbitrary",), vmem_limit_bytes=VMEM_LIMIT),
        name="inproj",
    )(x, gmix, win, krc, tabs, gq, wuq, gkv, wk, wv, gmat, vecs)


def _retention_body(q_ref, k_ref, v_ref, g_ref, dec_ref, gn_ref, o_ref,
                    state_ref, s_ref, kv_ref, prev_ref):
    @pl.when(pl.program_id(1) == 0)
    def _():
        state_ref[...] = jnp.zeros_like(state_ref)

    c = RET_CHUNK
    nt = (((1,), (1,)), ((), ()))
    n = RET_TILE // c
    blocks = [(hd, ci) for ci in range(n) for hd in range(RET_HEADS)]
    tile = lambda hd, ci: (slice(ci * c, (ci + 1) * c), slice(hd * RET_DIM, (hd + 1) * RET_DIM))

    for hd, ci in blocks:
        rows, cols = tile(hd, ci)
        k = k_ref[0, rows, cols]
        v = v_ref[0, rows, cols]
        s = lax.dot_general(q_ref[0, rows, cols], k, nt, preferred_element_type=F32)
        s_ref[hd * n + ci] = (s * dec_ref[0, hd]).astype(BF16)
        kz = (k.astype(F32) * dec_ref[1, hd]).astype(BF16)
        kv_ref[hd * n + ci] = jnp.dot(kz.T, v, preferred_element_type=F32)
    for hd in range(RET_HEADS):
        st = state_ref[hd]
        for ci in range(n):
            prev_ref[hd * n + ci] = st.astype(BF16)
            st = st * dec_ref[3, hd] + kv_ref[hd * n + ci]
        state_ref[hd] = st
    for hd, ci in blocks:
        rows, cols = tile(hd, ci)
        q = q_ref[0, rows, cols]
        y = (jnp.dot(s_ref[hd * n + ci], v_ref[0, rows, cols], preferred_element_type=F32)
             + jnp.dot(q, prev_ref[hd * n + ci], preferred_element_type=F32) * dec_ref[2, hd])
        g = g_ref[0, rows, cols]
        o_ref[0, rows, cols] = (_rms(y, gn_ref[hd]) * (g * jax.nn.sigmoid(g))).astype(BF16)


def _retention(rq, rk, rv, rg, dec, gn, batch, seq):
    shp = (batch, seq, RET_W)
    nblk = RET_HEADS * (RET_TILE // RET_CHUNK)
    blk = pl.BlockSpec((1, RET_TILE, RET_W), lambda b, s: (b, s, 0))
    return pl.pallas_call(
        _retention_body,
        grid=(batch, seq // RET_TILE),
        in_specs=[blk, blk, blk, blk, _resident(dec.shape), _resident(gn.shape)],
        out_specs=blk,
        out_shape=jax.ShapeDtypeStruct(shp, BF16),
        scratch_shapes=[pltpu.VMEM((RET_HEADS, RET_DIM, RET_DIM), F32),
                        pltpu.VMEM((nblk, RET_CHUNK, RET_CHUNK), BF16),
                        pltpu.VMEM((nblk, RET_DIM, RET_DIM), F32),
                        pltpu.VMEM((nblk, RET_DIM, RET_DIM), BF16)],
        compiler_params=pltpu.CompilerParams(
            dimension_semantics=("parallel", "arbitrary"), vmem_limit_bytes=VMEM_LIMIT),
        name="retention",
    )(rq.reshape(shp), rk.reshape(shp), rv.reshape(shp), rg.reshape(shp), dec, gn)


def _attention_body(q_ref, k_ref, v_ref, o_ref, m_ref, acc_ref):
    qi = pl.program_id(2)
    nt = (((1,), (1,)), ((), ()))
    half = ATT_TQ // 2
    m_ref[...] = jnp.full_like(m_ref, MASK_VALUE)
    acc_ref[...] = jnp.zeros_like(acc_ref)

    def steps(specs, skew):
        cols = lambda e: slice(e * LANES, (e + 1) * LANES)
        keys = lambda start, width: pl.ds(pl.multiple_of(start, half), width)

        def score(spec):
            e, q_rows, start, width, _ = spec
            return lax.dot_general(q_ref[0, q_rows, cols(e)], k_ref[0, keys(start, width), cols(e)],
                                   nt, preferred_element_type=F32)

        def softmax(spec, s):
            e, q_rows, _, width, mask_from = spec
            chunks = [s[:, c * LANES:(c + 1) * LANES] for c in range(width // LANES)]
            if mask_from is not None:
                r = lax.broadcasted_iota(jnp.int32, chunks[0].shape, 0)
                lane = lax.broadcasted_iota(jnp.int32, chunks[0].shape, 1)
                for c in range(mask_from, len(chunks)):
                    chunks[c] = jnp.where(r >= lane + (c - mask_from) * LANES, chunks[c],
                                          MASK_VALUE)
            m_old = m_ref[e, q_rows]
            m_new = jnp.maximum(
                m_old, jnp.max(functools.reduce(jnp.maximum, chunks), axis=-1, keepdims=True))
            m_ref[e, q_rows] = m_new
            return jnp.exp(m_old - m_new), jnp.concatenate(
                [jnp.exp(ch - m_new).astype(BF16) for ch in chunks], axis=1)

        def value(spec, alpha, p):
            e, q_rows, start, width, _ = spec
            acc_ref[e, q_rows] = alpha * acc_ref[e, q_rows] + jnp.dot(
                p, v_ref[0, keys(start, width), cols(e)], preferred_element_type=F32)

        n = len(specs)
        scores, probs = {}, {}
        for i in range(n + 2 * skew):
            if i < n:
                scores[i] = score(specs[i])
            if 0 <= i - skew < n:
                probs[i - skew] = softmax(specs[i - skew], scores.pop(i - skew))
            if 0 <= i - 2 * skew < n:
                value(specs[i - 2 * skew], *probs.pop(i - 2 * skew))

    def loop_body(kb, carry):
        for e in range(ATT_HEADS):
            steps([(e, slice(0, ATT_TQ), kb * ATT_TQ, ATT_TQ, None)], 1)
        return carry

    lax.fori_loop(0, qi, loop_body, 0)
    diag = qi * ATT_TQ
    steps([spec for e in range(ATT_HEADS) for spec in (
        (e, slice(0, half), diag, half, 0),
        (e, slice(half, ATT_TQ), diag, ATT_TQ, half // LANES))], DIAG_SKEW)
    lane = lax.broadcasted_iota(jnp.int32, (ATT_TQ, LANES), 1)
    for pr in range(ATT_HEADS // 2):
        o0, o1 = [acc * (1.0 / acc[:, MLA_V:MLA_V + 1])
                  for acc in (acc_ref[2 * pr], acc_ref[2 * pr + 1])]
        o_ref[0, :, pr * LANES:(pr + 1) * LANES] = jnp.where(
            lane < MLA_V, o0, pltpu.roll(o1, MLA_V, 1)).astype(BF16)


def _attention(qc, kc, vc, batch, seq):
    slab_w = MLA_HEADS * LANES
    grp_w = ATT_HEADS * LANES
    out_w = ATT_HEADS * MLA_V
    seq_blk = pl.BlockSpec((1, seq, grp_w), lambda b, j, i: (b, 0, j))
    return pl.pallas_call(
        _attention_body,
        grid=(batch, MLA_HEADS // ATT_HEADS, seq // ATT_TQ),
        in_specs=[pl.BlockSpec((1, ATT_TQ, grp_w), lambda b, j, i: (b, i, j)),
                  seq_blk, seq_blk],
        out_specs=pl.BlockSpec((1, ATT_TQ, out_w), lambda b, j, i: (b, i, j)),
        out_shape=jax.ShapeDtypeStruct((batch, seq, MLA_HEADS * MLA_V), BF16),
        scratch_shapes=[pltpu.VMEM((ATT_HEADS, ATT_TQ, LANES), F32),
                        pltpu.VMEM((ATT_HEADS, ATT_TQ, LANES), F32)],
        compiler_params=pltpu.CompilerParams(
            dimension_semantics=("parallel", "parallel", "arbitrary"),
            vmem_limit_bytes=VMEM_LIMIT),
        name="latent_attention",
    )(qc.reshape(batch, seq, slab_w), kc.reshape(batch, seq, slab_w),
      vc.reshape(batch, seq, slab_w))


def _retention_decay():
    c = RET_CHUNK
    lg = jnp.log(1.0 - 2.0 ** (-5.0 - jnp.arange(RET_HEADS, dtype=F32)))
    idx = jnp.arange(c, dtype=F32)
    rel = idx[:, None] - idx[None, :]
    decay = jnp.where(rel >= 0, jnp.exp(jnp.maximum(rel, 0.0)[None] * lg[:, None, None]), 0.0)
    zeta = jnp.exp((c - 1 - idx)[None, :] * lg[:, None])
    xi = jnp.exp((idx + 1.0)[None, :] * lg[:, None])
    cdec = jnp.exp(c * lg)
    full = (RET_HEADS, c, RET_DIM)
    return jnp.stack([decay,
                      jnp.broadcast_to(zeta[:, :, None], full),
                      jnp.broadcast_to(xi[:, :, None], full),
                      jnp.broadcast_to(cdec[:, None, None], full)])


def _swapped_rope_cols(w):
    a, b = w[..., :HALF_ROPE], w[..., HALF_ROPE:]
    return jnp.concatenate([a, b, b, a], axis=-1)


def _layer_params(l, p, win_bf16):
    row = lambda g: g[l].reshape(1, -1)
    kr_cols = jnp.concatenate(
        [jnp.zeros((D_MODEL, MLA_NOPE), BF16),
         _swapped_rope_cols(win_bf16[l][:, IN_BASE:])], axis=-1)

    dq = MLA_NOPE + MLA_ROPE
    wuq = p["w_uq"][l].reshape(Q_LORA, MLA_HEADS, dq)
    wuq = jnp.concatenate([wuq[..., :MLA_NOPE], _swapped_rope_cols(wuq[..., MLA_NOPE:])], axis=-1)
    wuq = wuq.reshape(Q_LORA, MLA_HEADS * LANES).astype(BF16)
    wukv = p["w_ukv"][l].reshape(KV_LORA, MLA_HEADS, MLA_NOPE + MLA_V)
    wk = jnp.concatenate([wukv[..., :MLA_NOPE], jnp.zeros_like(wukv[..., :MLA_NOPE])], axis=-1)
    wk = wk.reshape(KV_LORA, MLA_HEADS * LANES).astype(BF16)
    wv = jnp.concatenate([wukv[..., MLA_NOPE:], jnp.zeros_like(wukv[..., :LANES - MLA_V])], axis=-1)
    wv = wv.reshape(KV_LORA, MLA_HEADS * LANES).astype(BF16)

    scale = dq ** -0.5
    zeros64 = jnp.zeros((MLA_NOPE,), F32)
    gq = jnp.concatenate([p["qn_nope"][l], _swapped_rope_cols(p["qn_rope"][l])]) * scale
    gk = jnp.concatenate([p["kn_nope"][l], zeros64])
    gkr = jnp.concatenate([zeros64, _swapped_rope_cols(p["kn_rope"][l])])
    one_lane = jnp.zeros((LANES,), F32).at[MLA_V].set(1.0)
    vecs = jnp.stack([jnp.tile(r, MLA_HEADS) for r in (gq, gk, gkr, one_lane)]
                     + [jnp.zeros((MLA_HEADS * LANES,), F32)] * 4)
    return dict(
        ffn1_gain=row(p["ffn1_norm"]),
        ffn2_gain=row(p["ffn2_norm"]),
        inproj=(row(p["mix_norm"]), kr_cols, row(p["q_lat_norm"]), wuq, row(p["kv_lat_norm"]),
                wk, wv),
        vecs=vecs,
        gn=p["ret_head_norm"][l].reshape(RET_HEADS, 1, RET_DIM),
    )


def _group_matrix():
    idx = jnp.arange(2 * LANES)
    slab, lane = idx // LANES, idx % LANES
    same = slab[:, None] == slab[None, :]
    nope = (lane[:, None] < MLA_NOPE) & (lane[None, :] < MLA_NOPE)
    rope = ((lane[:, None] >= MLA_NOPE) & (lane[:, None] < MLA_NOPE + MLA_ROPE)
            & (lane[None, :] >= MLA_NOPE))
    g = jnp.where(nope, 1.0 / MLA_NOPE, jnp.where(rope, 1.0 / MLA_ROPE, 0.0))
    return jnp.where(same, g, 0.0).astype(BF16)


def kernel(x, positions, ffn1_norm, ffn1_w_gate, ffn1_w_up, ffn1_w_down, mix_norm, w_in, ret_head_norm, q_lat_norm, w_uq, kv_lat_norm, w_ukv, qn_nope, qn_rope, kn_nope, kn_rope, w_o, ffn2_norm, ffn2_w_gate, ffn2_w_up, ffn2_w_down):
    p = dict(ffn1_norm=ffn1_norm, ffn1_w_gate=ffn1_w_gate, ffn1_w_up=ffn1_w_up,
             ffn1_w_down=ffn1_w_down, mix_norm=mix_norm, w_in=w_in,
             ret_head_norm=ret_head_norm, q_lat_norm=q_lat_norm, w_uq=w_uq,
             kv_lat_norm=kv_lat_norm, w_ukv=w_ukv, qn_nope=qn_nope, qn_rope=qn_rope,
             kn_nope=kn_nope, kn_rope=kn_rope, w_o=w_o, ffn2_norm=ffn2_norm,
             ffn2_w_gate=ffn2_w_gate, ffn2_w_up=ffn2_w_up, ffn2_w_down=ffn2_w_down)
    batch, seq, d = x.shape
    depth = w_in.shape[0]
    assert d == D_MODEL and positions.shape == (batch, seq)
    assert seq % ATT_TQ == 0 and seq % RET_TILE == 0
    assert (batch * seq) % TOK_TILE == 0 and (batch * seq) % FFN_TILE == 0
    assert w_in.shape[1:] == (D_MODEL, IN_BASE + MLA_ROPE) and ffn1_w_gate.shape[1:] == (d, D_FF)
    dec = _retention_decay()
    gmat = _group_matrix()
    ffn1_stacks = [ffn1_w_gate, ffn1_w_up, ffn1_w_down]
    ffn2_stacks = [ffn2_w_gate, ffn2_w_up, ffn2_w_down]
    w_ffn1 = _cast_stacks(ffn1_stacks[:2], 1) + _cast_stacks(ffn1_stacks[2:], 1)
    win_bf16 = w_in.astype(BF16)
    tabs = None
    xf = x.reshape(batch * seq, d)
    for l in range(depth):
        lp = _layer_params(l, p, win_bf16)
        xf, new_tabs, side_w = _ffn(xf, lp["ffn1_gain"], *w_ffn1,
                                    side=(ffn2_stacks + [w_o], l),
                                    positions=positions if l == 0 else None)
        tabs = new_tabs if l == 0 else tabs
        w_ffn2, wo = side_w[:3], side_w[3]
        gmix, kr_cols, gq, wuq, gkv, wk, wv = lp["inproj"]
        win_main = win_bf16[l, :, :IN_BASE]
        rq, rk, rv, rg, qc, kc, v = _inproj(xf, tabs, gmix, win_main, kr_cols, gq, wuq, gkv,
                                            wk, wv, gmat, lp["vecs"])
        y_ret = _retention(rq, rk, rv, rg, dec, lp["gn"], batch, seq)
        y_mla = _attention(qc, kc, v, batch, seq)
        xf, _, w_ffn1 = _ffn(
            xf, lp["ffn2_gain"], *w_ffn2,
            proj=(y_ret.reshape(batch * seq, -1), y_mla.reshape(batch * seq, -1), wo),
            side=(ffn1_stacks, l + 1) if l + 1 < depth else None)
    return xf.reshape(batch, seq, d)
```

```python
import functools

import jax
import jax.numpy as jnp
from jax import lax
from jax.experimental import pallas as pl
from jax.experimental.pallas import tpu as pltpu

D_MODEL = 1024
D_FF = 2816
RET_HEADS = 4
RET_DIM = 128
RET_CHUNK = 128
MLA_HEADS = 8
MLA_NOPE = 64
MLA_ROPE = 32
MLA_V = 64
Q_LORA = 256
KV_LORA = 128
ROPE_THETA = 10000.0
EPS = 1e-6

LANES = 128
BF16_ROWS = 16
RET_W = RET_HEADS * RET_DIM
IN_BASE = 4 * RET_W + Q_LORA + KV_LORA
IN_W = IN_BASE + LANES
HALF_ROPE = MLA_ROPE // 2
SLAB_SHIFT = LANES - MLA_ROPE

TOK_TILE = 512
FF_TILE = 256
PROLOGUE_STEPS = 16
ATT_TQ = 1024
ATT_HEADS = 4
DIAG_SKEW = 2
RET_TILE = 1024
V7X_VMEM_BYTES = 64 * 1024 * 1024
VMEM_LIMIT = V7X_VMEM_BYTES * 3 // 4
FFN_TILE = 1024
FFN_VMEM_LIMIT = V7X_VMEM_BYTES * 29 // 32
MASK_VALUE = -1e30

F32 = jnp.float32
BF16 = jnp.bfloat16


def _rms(x, gain):
    return x * lax.rsqrt(jnp.mean(x * x, axis=-1, keepdims=True) + EPS) * gain


def _resident(shape):
    nd = len(shape)
    return pl.BlockSpec(shape, lambda *_: (0,) * nd)


def _tables_rows(pos_ref, c_ref, o_ref, rows):
    ang = pos_ref[rows, :].astype(F32) * c_ref[0:1, :]
    c, s = jnp.cos(ang), jnp.sin(ang)
    low = lax.broadcasted_iota(jnp.int32, c.shape, 1) < RET_DIM // 2
    o_ref[0, rows] = jnp.where(low, c, pltpu.roll(c, RET_DIM // 2, 1))
    o_ref[1, rows] = jnp.where(low, s, pltpu.roll(s, RET_DIM // 2, 1)) * c_ref[1:2, :]
    o_ref[2, rows] = c_ref[2:3, :] + c_ref[3:4, :] * c
    o_ref[3, rows] = s * c_ref[4:5, :]


def _table_consts():
    inv_r = 1.0 / (ROPE_THETA ** (jnp.arange(0, RET_DIM, 2, dtype=F32) / RET_DIM))
    inv_m = 1.0 / (ROPE_THETA ** (jnp.arange(0, MLA_ROPE, 2, dtype=F32) / MLA_ROPE))
    half = RET_DIM // 2
    z = lambda n: jnp.zeros((n,), F32)
    o = lambda n: jnp.ones((n,), F32)
    return jnp.stack([
        jnp.concatenate([inv_r, inv_m, inv_m, z(MLA_ROPE)]),
        jnp.concatenate([-o(half), o(half)]),
        jnp.concatenate([o(MLA_NOPE), z(LANES - MLA_NOPE)]),
        jnp.concatenate([z(MLA_NOPE), o(MLA_ROPE), z(MLA_ROPE)]),
        jnp.concatenate([z(MLA_NOPE), -o(HALF_ROPE), o(HALF_ROPE), z(MLA_ROPE)]),
        z(LANES), z(LANES), z(LANES)])


def _ffn_body(*refs, with_proj, n_side):
    refs = list(refs)
    take = lambda n: [refs.pop(0) for _ in range(n)]
    (x_ref,) = take(1)
    if with_proj:
        yr_ref, ym_ref, wo_ref = take(3)
    g_ref, wg_ref, wu_ref, wd_ref = take(4)
    side_in = take(n_side)
    (o_ref,) = take(1)
    side_out = take(n_side)
    (a_ref,) = take(1)

    x = x_ref[...]
    if with_proj:
        x = (x + jnp.dot(yr_ref[...], wo_ref[0:RET_W, :], preferred_element_type=F32)
             + jnp.dot(ym_ref[...], wo_ref[RET_W:, :], preferred_element_type=F32))
    h = _rms(x, g_ref[...]).astype(BF16)
    for c in range(D_FF // FF_TILE):
        cols = slice(c * FF_TILE, (c + 1) * FF_TILE)
        g = jnp.dot(h, wg_ref[:, cols], preferred_element_type=F32)
        u = jnp.dot(h, wu_ref[:, cols], preferred_element_type=F32)
        a_ref[:, cols] = (g * jax.nn.sigmoid(g) * u).astype(BF16)
    for src, dst in zip(side_in, side_out):
        dst[...] = src[...].astype(BF16)
    y = jnp.dot(a_ref[...], wd_ref[...], preferred_element_type=F32)
    o_ref[...] = x + 0.5 * y


def _prologue_body(*refs):
    pos_ref, c_ref = refs[:2]
    n = (len(refs) - 3) // 2
    srcs, tab_ref, dsts = refs[2:2 + n], refs[2 + n], refs[3 + n:]
    for src, dst in zip(srcs, dsts):
        dst[...] = src[...].astype(BF16)
    _tables_rows(pos_ref, c_ref, tab_ref, slice(0, pos_ref.shape[0]))


def _prologue(positions, ws):
    t = positions.size
    steps = PROLOGUE_STEPS
    rows = t // steps
    specs = [_side_cast_specs(w, 0, steps) for w in ws]
    outs = pl.pallas_call(
        _prologue_body,
        grid=(steps,),
        in_specs=[pl.BlockSpec((rows, 1), lambda i: (i, 0)), _resident((8, LANES))]
        + [s[0] for s in specs],
        out_specs=[pl.BlockSpec((4, rows, LANES), lambda i: (0, i, 0))] + [s[1] for s in specs],
        out_shape=[jax.ShapeDtypeStruct((4, t, LANES), F32)] + [s[2] for s in specs],
        compiler_params=pltpu.CompilerParams(
            dimension_semantics=("arbitrary",), vmem_limit_bytes=VMEM_LIMIT),
        name="prologue",
    )(positions.reshape(t, 1), _table_consts(), *ws)
    return outs[0], list(outs[1:])


def _side_cast_specs(w, l, steps):
    _, r, c = w.shape
    hold = 1
    while (r * hold) % (steps * BF16_ROWS) or steps % hold:
        hold += 1
    rows = r * hold // steps
    return (pl.BlockSpec((None, rows, c), lambda i: (l, i // hold, 0)),
            pl.BlockSpec((rows, c), lambda i: (i // hold, 0)),
            jax.ShapeDtypeStruct((r, c), BF16))


def _ffn(x, gain, wg, wu, wd, proj=None, side=None):
    t = x.shape[0]
    tm = FFN_TILE
    steps = t // tm
    row = lambda w: pl.BlockSpec((tm, w), lambda i: (i, 0))
    ins, specs = [x], [row(D_MODEL)]
    if proj is not None:
        yr, ym, wo = proj
        ins += [yr, ym, wo]
        specs += [row(RET_W), row(MLA_HEADS * MLA_V), _resident(wo.shape)]
    ins += [gain, wg, wu, wd]
    specs += [_resident(w.shape) for w in (gain, wg, wu, wd)]
    out_specs = [row(D_MODEL)]
    out_shape = [jax.ShapeDtypeStruct((t, D_MODEL), F32)]
    side_ws, side_layer = side if side is not None else ([], 0)
    for w in side_ws:
        in_spec, out_spec, shape = _side_cast_specs(w, side_layer, steps)
        ins.append(w)
        specs.append(in_spec)
        out_specs.append(out_spec)
        out_shape.append(shape)
    outs = pl.pallas_call(
        functools.partial(_ffn_body, with_proj=proj is not None, n_side=len(side_ws)),
        grid=(steps,),
        in_specs=specs,
        out_specs=out_specs,
        out_shape=out_shape,
        scratch_shapes=[pltpu.VMEM((tm, D_FF), BF16)],
        compiler_params=pltpu.CompilerParams(
            dimension_semantics=("arbitrary",), vmem_limit_bytes=FFN_VMEM_LIMIT),
        name="ffn_proj" if proj is not None else "ffn",
    )(*ins)
    return outs[0], list(outs[1:])


def _group_mean(x, gmat):
    x2 = x * x
    hi = x2.astype(BF16)
    lo = (x2 - hi.astype(F32)).astype(BF16)
    return (jnp.dot(hi, gmat, preferred_element_type=F32)
            + jnp.dot(lo, gmat, preferred_element_type=F32))


def _inproj_body(x_ref, gmix_ref, win_ref, krc_ref, tab_ref, gq_ref, wuq_ref, gkv_ref, wk_ref,
                 wv_ref, gmat_ref, vec_ref,
                 rq_ref, rk_ref, rv_ref, rg_ref, qc_ref, kc_ref, v_ref, wlat_ref):
    lat0 = 4 * RET_W

    @pl.when(pl.program_id(0) == 0)
    def _():
        wlat_ref[:, 0:IN_BASE - lat0] = win_ref[:, lat0:]
        wlat_ref[:, IN_BASE - lat0:] = krc_ref[...]

    h = _rms(x_ref[...], gmix_ref[...]).astype(BF16)
    cos_r, sin_r, cos_m, sin_m = tab_ref[0], tab_ref[1], tab_ref[2], tab_ref[3]
    lat = jnp.dot(h, wlat_ref[...], preferred_element_type=F32)
    c_q = lat[:, 0:Q_LORA]
    c_kv = lat[:, Q_LORA:Q_LORA + KV_LORA]
    kr = lat[:, Q_LORA + KV_LORA:]

    cqn = _rms(c_q, gq_ref[...]).astype(BF16)
    ckvn = _rms(c_kv, gkv_ref[...]).astype(BF16)
    qraw = jnp.dot(cqn, wuq_ref[...], preferred_element_type=F32)
    kraw = jnp.dot(ckvn, wk_ref[...], preferred_element_type=F32)
    v_ref[...] = (jnp.dot(ckvn, wv_ref[...], preferred_element_type=F32)
                  + vec_ref[3:4, :]).astype(BF16)

    lane = lax.broadcasted_iota(jnp.int32, (1, LANES), 1)
    in_rope = (lane >= MLA_NOPE) & (lane < MLA_NOPE + MLA_ROPE)
    ss = jnp.sum(jnp.where(in_rope, kr * kr, 0.0), axis=-1, keepdims=True) * (1.0 / MLA_ROPE)
    krn = kr * lax.rsqrt(ss + EPS) * vec_ref[2:3, 0:LANES]
    kro = krn * cos_m + pltpu.roll(krn, SLAB_SHIFT, 1) * sin_m

    gmat = gmat_ref[...]
    for pr in range(MLA_HEADS // 2):
        cols = slice(pr * 2 * LANES, (pr + 1) * 2 * LANES)
        xs = qraw[:, cols]
        y = xs * lax.rsqrt(_group_mean(xs, gmat) + EPS) * vec_ref[0:1, 0:2 * LANES]
        for e in range(2):
            sl = slice(e * LANES, (e + 1) * LANES)
            dst = slice((2 * pr + e) * LANES, (2 * pr + e + 1) * LANES)
            ys = y[:, sl]
            qc_ref[:, dst] = (ys * cos_m + pltpu.roll(ys, SLAB_SHIFT, 1) * sin_m).astype(BF16)
            ks = kraw[:, dst]
            ms = jnp.sum(ks * ks, axis=-1, keepdims=True) * (1.0 / MLA_NOPE)
            kc_ref[:, dst] = (ks * lax.rsqrt(ms + EPS) * vec_ref[1:2, 0:LANES] + kro).astype(BF16)

    def ret_cols(i):
        return jnp.dot(h, win_ref[:, i * RET_W:(i + 1) * RET_W], preferred_element_type=F32)

    ret_scale = RET_DIM ** -0.5
    for dst_ref, scale, i in ((rq_ref, None, 0), (rk_ref, ret_scale, 1)):
        x = ret_cols(i)
        for hd in range(RET_HEADS):
            sl = slice(hd * RET_DIM, (hd + 1) * RET_DIM)
            roped = x[:, sl] * cos_r + pltpu.roll(x[:, sl], RET_DIM // 2, 1) * sin_r
            dst_ref[:, sl] = (roped if scale is None else roped * scale).astype(BF16)
    rv_ref[...] = ret_cols(2).astype(BF16)
    rg_ref[...] = ret_cols(3)


def _inproj(x, tabs, gmix, win, krc, gq, wuq, gkv, wk, wv, gmat, vecs):
    t = x.shape[0]
    tm = TOK_TILE
    row = lambda w: pl.BlockSpec((tm, w), lambda i: (i, 0))
    slab_w = MLA_HEADS * LANES
    outs = [(RET_W, BF16), (RET_W, BF16), (RET_W, BF16), (RET_W, F32),
            (slab_w, BF16), (slab_w, BF16), (slab_w, BF16)]
    return pl.pallas_call(
        _inproj_body,
        grid=(t // tm,),
        in_specs=[row(D_MODEL), _resident(gmix.shape),
                  _resident(win.shape), _resident(krc.shape),
                  pl.BlockSpec((4, tm, LANES), lambda i: (0, i, 0)),
                  _resident(gq.shape), _resident(wuq.shape), _resident(gkv.shape),
                  _resident(wk.shape), _resident(wv.shape), _resident(gmat.shape),
                  _resident(vecs.shape)],
        out_specs=[row(w) for w, _ in outs],
        out_shape=[jax.ShapeDtypeStruct((t, w), dt) for w, dt in outs],
        scratch_shapes=[pltpu.VMEM((D_MODEL, IN_W - 4 * RET_W), BF16)],
        compiler_params=pltpu.CompilerParams(
            dimension_semantics=("arbitrary",), vmem_limit_bytes=VMEM_LIMIT),
        name="inproj",
    )(x, gmix, win, krc, tabs, gq, wuq, gkv, wk, wv, gmat, vecs)


def _retention_body(q_ref, k_ref, v_ref, g_ref, dec_ref, gn_ref, o_ref,
                    state_ref, s_ref, kv_ref, prev_ref):
    @pl.when(pl.program_id(1) == 0)
    def _():
        state_ref[...] = jnp.zeros_like(state_ref)

    c = RET_CHUNK
    nt = (((1,), (1,)), ((), ()))
    n = RET_TILE // c
    blocks = [(hd, ci) for ci in range(n) for hd in range(RET_HEADS)]
    tile = lambda hd, ci: (slice(ci * c, (ci + 1) * c), slice(hd * RET_DIM, (hd + 1) * RET_DIM))

    for hd, ci in blocks:
        rows, cols = tile(hd, ci)
        k = k_ref[0, rows, cols]
        v = v_ref[0, rows, cols]
        s = lax.dot_general(q_ref[0, rows, cols], k, nt, preferred_element_type=F32)
        s_ref[hd * n + ci] = (s * dec_ref[0, hd]).astype(BF16)
        kz = (k.astype(F32) * dec_ref[1, hd]).astype(BF16)
        kv_ref[hd * n + ci] = jnp.dot(kz.T, v, preferred_element_type=F32)
    for hd in range(RET_HEADS):
        st = state_ref[hd]
        for ci in range(n):
            prev_ref[hd * n + ci] = st.astype(BF16)
            st = st * dec_ref[3, hd] + kv_ref[hd * n + ci]
        state_ref[hd] = st
    for hd, ci in blocks:
        rows, cols = tile(hd, ci)
        q = q_ref[0, rows, cols]
        y = (jnp.dot(s_ref[hd * n + ci], v_ref[0, rows, cols], preferred_element_type=F32)
             + jnp.dot(q, prev_ref[hd * n + ci], preferred_element_type=F32) * dec_ref[2, hd])
        g = g_ref[0, rows, cols]
        o_ref[0, rows, cols] = (_rms(y, gn_ref[hd]) * (g * jax.nn.sigmoid(g))).astype(BF16)


def _retention(rq, rk, rv, rg, dec, gn, batch, seq):
    shp = (batch, seq, RET_W)
    nblk = RET_HEADS * (RET_TILE // RET_CHUNK)
    blk = pl.BlockSpec((1, RET_TILE, RET_W), lambda b, s: (b, s, 0))
    return pl.pallas_call(
        _retention_body,
        grid=(batch, seq // RET_TILE),
        in_specs=[blk, blk, blk, blk, _resident(dec.shape), _resident(gn.shape)],
        out_specs=blk,
        out_shape=jax.ShapeDtypeStruct(shp, BF16),
        scratch_shapes=[pltpu.VMEM((RET_HEADS, RET_DIM, RET_DIM), F32),
                        pltpu.VMEM((nblk, RET_CHUNK, RET_CHUNK), BF16),
                        pltpu.VMEM((nblk, RET_DIM, RET_DIM), F32),
                        pltpu.VMEM((nblk, RET_DIM, RET_DIM), BF16)],
        compiler_params=pltpu.CompilerParams(
            dimension_semantics=("parallel", "arbitrary"), vmem_limit_bytes=VMEM_LIMIT),
        name="retention",
    )(rq.reshape(shp), rk.reshape(shp), rv.reshape(shp), rg.reshape(shp), dec, gn)


def _attention_body(q_ref, k_ref, v_ref, o_ref, m_ref, acc_ref):
    qi = pl.program_id(2)
    nt = (((1,), (1,)), ((), ()))
    half = ATT_TQ // 2
    m_ref[...] = jnp.full_like(m_ref, MASK_VALUE)
    acc_ref[...] = jnp.zeros_like(acc_ref)

    def steps(specs, skew):
        cols = lambda e: slice(e * LANES, (e + 1) * LANES)
        keys = lambda start, width: pl.ds(pl.multiple_of(start, half), width)

        def score(spec):
            e, q_rows, start, width, _ = spec
            return lax.dot_general(q_ref[0, q_rows, cols(e)], k_ref[0, keys(start, width), cols(e)],
                                   nt, preferred_element_type=F32)

        def softmax(spec, s):
            e, q_rows, _, width, mask_from = spec
            chunks = [s[:, c * LANES:(c + 1) * LANES] for c in range(width // LANES)]
            if mask_from is not None:
                r = lax.broadcasted_iota(jnp.int32, chunks[0].shape, 0)
                lane = lax.broadcasted_iota(jnp.int32, chunks[0].shape, 1)
                for c in range(mask_from, len(chunks)):
                    chunks[c] = jnp.where(r >= lane + (c - mask_from) * LANES, chunks[c],
                                          MASK_VALUE)
            m_old = m_ref[e, q_rows]
            m_new = jnp.maximum(
                m_old, jnp.max(functools.reduce(jnp.maximum, chunks), axis=-1, keepdims=True))
            m_ref[e, q_rows] = m_new
            return jnp.exp(m_old - m_new), jnp.concatenate(
                [jnp.exp(ch - m_new).astype(BF16) for ch in chunks], axis=1)

        def value(spec, alpha, p):
            e, q_rows, start, width, _ = spec
            acc_ref[e, q_rows] = alpha * acc_ref[e, q_rows] + jnp.dot(
                p, v_ref[0, keys(start, width), cols(e)], preferred_element_type=F32)

        n = len(specs)
        scores, probs = {}, {}
        for i in range(n + 2 * skew):
            if i < n:
                scores[i] = score(specs[i])
            if 0 <= i - skew < n:
                probs[i - skew] = softmax(specs[i - skew], scores.pop(i - skew))
            if 0 <= i - 2 * skew < n:
                value(specs[i - 2 * skew], *probs.pop(i - 2 * skew))

    def loop_body(kb, carry):
        for e in range(ATT_HEADS):
            steps([(e, slice(0, ATT_TQ), kb * ATT_TQ, ATT_TQ, None)], 1)
        return carry

    lax.fori_loop(0, qi, loop_body, 0)
    diag = qi * ATT_TQ
    steps([spec for e in range(ATT_HEADS) for spec in (
        (e, slice(0, half), diag, half, 0),
        (e, slice(half, ATT_TQ), diag, ATT_TQ, half // LANES))], DIAG_SKEW)
    lane = lax.broadcasted_iota(jnp.int32, (ATT_TQ, LANES), 1)
    for pr in range(ATT_HEADS // 2):
        o0, o1 = [acc * (1.0 / acc[:, MLA_V:MLA_V + 1])
                  for acc in (acc_ref[2 * pr], acc_ref[2 * pr + 1])]
        o_ref[0, :, pr * LANES:(pr + 1) * LANES] = jnp.where(
            lane < MLA_V, o0, pltpu.roll(o1, MLA_V, 1)).astype(BF16)


def _attention(qc, kc, vc, batch, seq):
    slab_w = MLA_HEADS * LANES
    grp_w = ATT_HEADS * LANES
    out_w = ATT_HEADS * MLA_V
    seq_blk = pl.BlockSpec((1, seq, grp_w), lambda b, j, i: (b, 0, j))
    return pl.pallas_call(
        _attention_body,
        grid=(batch, MLA_HEADS // ATT_HEADS, seq // ATT_TQ),
        in_specs=[pl.BlockSpec((1, ATT_TQ, grp_w), lambda b, j, i: (b, i, j)),
                  seq_blk, seq_blk],
        out_specs=pl.BlockSpec((1, ATT_TQ, out_w), lambda b, j, i: (b, i, j)),
        out_shape=jax.ShapeDtypeStruct((batch, seq, MLA_HEADS * MLA_V), BF16),
        scratch_shapes=[pltpu.VMEM((ATT_HEADS, ATT_TQ, LANES), F32),
                        pltpu.VMEM((ATT_HEADS, ATT_TQ, LANES), F32)],
        compiler_params=pltpu.CompilerParams(
            dimension_semantics=("parallel", "parallel", "arbitrary"),
            vmem_limit_bytes=VMEM_LIMIT),
        name="latent_attention",
    )(qc.reshape(batch, seq, slab_w), kc.reshape(batch, seq, slab_w),
      vc.reshape(batch, seq, slab_w))


def _retention_decay():
    c = RET_CHUNK
    lg = jnp.log(1.0 - 2.0 ** (-5.0 - jnp.arange(RET_HEADS, dtype=F32)))
    idx = jnp.arange(c, dtype=F32)
    rel = idx[:, None] - idx[None, :]
    decay = jnp.where(rel >= 0, jnp.exp(jnp.maximum(rel, 0.0)[None] * lg[:, None, None]), 0.0)
    zeta = jnp.exp((c - 1 - idx)[None, :] * lg[:, None])
    xi = jnp.exp((idx + 1.0)[None, :] * lg[:, None])
    cdec = jnp.exp(c * lg)
    full = (RET_HEADS, c, RET_DIM)
    return jnp.stack([decay,
                      jnp.broadcast_to(zeta[:, :, None], full),
                      jnp.broadcast_to(xi[:, :, None], full),
                      jnp.broadcast_to(cdec[:, None, None], full)])


def _swapped_rope_cols(w):
    a, b = w[..., :HALF_ROPE], w[..., HALF_ROPE:]
    return jnp.concatenate([a, b, b, a], axis=-1)


def _layer_params(l, p, win_bf16):
    row = lambda g: g[l].reshape(1, -1)
    kr_cols = jnp.concatenate(
        [jnp.zeros((D_MODEL, MLA_NOPE), BF16),
         _swapped_rope_cols(win_bf16[l][:, IN_BASE:])], axis=-1)

    dq = MLA_NOPE + MLA_ROPE
    wuq = p["w_uq"][l].reshape(Q_LORA, MLA_HEADS, dq)
    wuq = jnp.concatenate([wuq[..., :MLA_NOPE], _swapped_rope_cols(wuq[..., MLA_NOPE:])], axis=-1)
    wuq = wuq.reshape(Q_LORA, MLA_HEADS * LANES).astype(BF16)
    wukv = p["w_ukv"][l].reshape(KV_LORA, MLA_HEADS, MLA_NOPE + MLA_V)
    wk = jnp.concatenate([wukv[..., :MLA_NOPE], jnp.zeros_like(wukv[..., :MLA_NOPE])], axis=-1)
    wk = wk.reshape(KV_LORA, MLA_HEADS * LANES).astype(BF16)
    wv = jnp.concatenate([wukv[..., MLA_NOPE:], jnp.zeros_like(wukv[..., :LANES - MLA_V])], axis=-1)
    wv = wv.reshape(KV_LORA, MLA_HEADS * LANES).astype(BF16)

    scale = dq ** -0.5
    zeros64 = jnp.zeros((MLA_NOPE,), F32)
    gq = jnp.concatenate([p["qn_nope"][l], _swapped_rope_cols(p["qn_rope"][l])]) * scale
    gk = jnp.concatenate([p["kn_nope"][l], zeros64])
    gkr = jnp.concatenate([zeros64, _swapped_rope_cols(p["kn_rope"][l])])
    one_lane = jnp.zeros((LANES,), F32).at[MLA_V].set(1.0)
    vecs = jnp.stack([jnp.tile(r, MLA_HEADS) for r in (gq, gk, gkr, one_lane)]
                     + [jnp.zeros((MLA_HEADS * LANES,), F32)] * 4)
    return dict(
        ffn1_gain=row(p["ffn1_norm"]),
        ffn2_gain=row(p["ffn2_norm"]),
        inproj=(row(p["mix_norm"]), kr_cols, row(p["q_lat_norm"]), wuq, row(p["kv_lat_norm"]),
                wk, wv),
        vecs=vecs,
        gn=p["ret_head_norm"][l].reshape(RET_HEADS, 1, RET_DIM),
    )


def _group_matrix():
    idx = jnp.arange(2 * LANES)
    slab, lane = idx // LANES, idx % LANES
    same = slab[:, None] == slab[None, :]
    nope = (lane[:, None] < MLA_NOPE) & (lane[None, :] < MLA_NOPE)
    rope = ((lane[:, None] >= MLA_NOPE) & (lane[:, None] < MLA_NOPE + MLA_ROPE)
            & (lane[None, :] >= MLA_NOPE))
    g = jnp.where(nope, 1.0 / MLA_NOPE, jnp.where(rope, 1.0 / MLA_ROPE, 0.0))
    return jnp.where(same, g, 0.0).astype(BF16)


def kernel(x, positions, ffn1_norm, ffn1_w_gate, ffn1_w_up, ffn1_w_down, mix_norm, w_in, ret_head_norm, q_lat_norm, w_uq, kv_lat_norm, w_ukv, qn_nope, qn_rope, kn_nope, kn_rope, w_o, ffn2_norm, ffn2_w_gate, ffn2_w_up, ffn2_w_down):
    p = dict(ffn1_norm=ffn1_norm, ffn1_w_gate=ffn1_w_gate, ffn1_w_up=ffn1_w_up,
             ffn1_w_down=ffn1_w_down, mix_norm=mix_norm, w_in=w_in,
             ret_head_norm=ret_head_norm, q_lat_norm=q_lat_norm, w_uq=w_uq,
             kv_lat_norm=kv_lat_norm, w_ukv=w_ukv, qn_nope=qn_nope, qn_rope=qn_rope,
             kn_nope=kn_nope, kn_rope=kn_rope, w_o=w_o, ffn2_norm=ffn2_norm,
             ffn2_w_gate=ffn2_w_gate, ffn2_w_up=ffn2_w_up, ffn2_w_down=ffn2_w_down)
    batch, seq, d = x.shape
    depth = w_in.shape[0]
    assert d == D_MODEL and positions.shape == (batch, seq)
    assert seq % ATT_TQ == 0 and seq % RET_TILE == 0
    assert (batch * seq) % TOK_TILE == 0 and (batch * seq) % FFN_TILE == 0
    assert w_in.shape[1:] == (D_MODEL, IN_BASE + MLA_ROPE) and ffn1_w_gate.shape[1:] == (d, D_FF)
    dec = _retention_decay()
    gmat = _group_matrix()
    ffn1_stacks = [ffn1_w_gate, ffn1_w_up, ffn1_w_down]
    ffn2_stacks = [ffn2_w_gate, ffn2_w_up, ffn2_w_down]
    tabs, w_ffn1 = _prologue(positions, ffn1_stacks)
    win_bf16 = w_in.astype(BF16)
    xf = x.reshape(batch * seq, d)
    for l in range(depth):
        lp = _layer_params(l, p, win_bf16)
        xf, side_w = _ffn(xf, lp["ffn1_gain"], *w_ffn1, side=(ffn2_stacks + [w_o], l))
        w_ffn2, wo = side_w[:3], side_w[3]
        gmix, kr_cols, gq, wuq, gkv, wk, wv = lp["inproj"]
        win_main = win_bf16[l, :, :IN_BASE]
        rq, rk, rv, rg, qc, kc, v = _inproj(xf, tabs, gmix, win_main, kr_cols, gq, wuq, gkv,
                                            wk, wv, gmat, lp["vecs"])
        y_ret = _retention(rq, rk, rv, rg, dec, lp["gn"], batch, seq)
        y_mla = _attention(qc, kc, v, batch, seq)
        xf, w_ffn1 = _ffn(
            xf, lp["ffn2_gain"], *w_ffn2,
            proj=(y_ret.reshape(batch * seq, -1), y_mla.reshape(batch * seq, -1), wo),
            side=(ffn1_stacks, l + 1) if l + 1 < depth else None)
    return xf.reshape(batch, seq, d)
```

```python
import functools

import jax
import jax.numpy as jnp
from jax import lax
from jax.experimental import pallas as pl
from jax.experimental.pallas import tpu as pltpu

D_MODEL = 1024
D_FF = 2816
RET_HEADS = 4
RET_DIM = 128
RET_CHUNK = 128
MLA_HEADS = 8
MLA_NOPE = 64
MLA_ROPE = 32
MLA_V = 64
Q_LORA = 256
KV_LORA = 128
ROPE_THETA = 10000.0
EPS = 1e-6

LANES = 128
BF16_ROWS = 16
RET_W = RET_HEADS * RET_DIM
IN_BASE = 4 * RET_W + Q_LORA + KV_LORA
IN_W = IN_BASE + LANES
HALF_ROPE = MLA_ROPE // 2
SLAB_SHIFT = LANES - MLA_ROPE

TOK_TILE = 512
FF_TILE = 256
PROLOGUE_STEPS = 16
ATT_TQ = 1024
ATT_HEADS = 4
DIAG_SKEW = 2
RET_TILE = 1024
V7X_VMEM_BYTES = 64 * 1024 * 1024
VMEM_LIMIT = V7X_VMEM_BYTES * 3 // 4
FFN_TILE = 1024
FFN_VMEM_LIMIT = V7X_VMEM_BYTES * 29 // 32
MASK_VALUE = -1e30
LOG2_E = 1.4426950408889634

F32 = jnp.float32
BF16 = jnp.bfloat16


def _rms(x, gain):
    return x * lax.rsqrt(jnp.mean(x * x, axis=-1, keepdims=True) + EPS) * gain


def _resident(shape):
    nd = len(shape)
    return pl.BlockSpec(shape, lambda *_: (0,) * nd)


def _tables_rows(pos_ref, c_ref, o_ref, rows):
    ang = pos_ref[rows, :].astype(F32) * c_ref[0:1, :]
    c, s = jnp.cos(ang), jnp.sin(ang)
    low = lax.broadcasted_iota(jnp.int32, c.shape, 1) < RET_DIM // 2
    o_ref[0, rows] = jnp.where(low, c, pltpu.roll(c, RET_DIM // 2, 1))
    o_ref[1, rows] = jnp.where(low, s, pltpu.roll(s, RET_DIM // 2, 1)) * c_ref[1:2, :]
    o_ref[2, rows] = c_ref[2:3, :] + c_ref[3:4, :] * c
    o_ref[3, rows] = s * c_ref[4:5, :]


def _table_consts():
    inv_r = 1.0 / (ROPE_THETA ** (jnp.arange(0, RET_DIM, 2, dtype=F32) / RET_DIM))
    inv_m = 1.0 / (ROPE_THETA ** (jnp.arange(0, MLA_ROPE, 2, dtype=F32) / MLA_ROPE))
    half = RET_DIM // 2
    z = lambda n: jnp.zeros((n,), F32)
    o = lambda n: jnp.ones((n,), F32)
    return jnp.stack([
        jnp.concatenate([inv_r, inv_m, inv_m, z(MLA_ROPE)]),
        jnp.concatenate([-o(half), o(half)]),
        jnp.concatenate([o(MLA_NOPE), z(LANES - MLA_NOPE)]),
        jnp.concatenate([z(MLA_NOPE), o(MLA_ROPE), z(MLA_ROPE)]),
        jnp.concatenate([z(MLA_NOPE), -o(HALF_ROPE), o(HALF_ROPE), z(MLA_ROPE)]),
        z(LANES), z(LANES), z(LANES)])


def _ffn_body(*refs, with_proj, n_side):
    refs = list(refs)
    take = lambda n: [refs.pop(0) for _ in range(n)]
    (x_ref,) = take(1)
    if with_proj:
        yr_ref, ym_ref, wo_ref = take(3)
    g_ref, wg_ref, wu_ref, wd_ref = take(4)
    side_in = take(n_side)
    (o_ref,) = take(1)
    side_out = take(n_side)
    (a_ref,) = take(1)

    x = x_ref[...]
    if with_proj:
        x = (x + jnp.dot(yr_ref[...], wo_ref[0:RET_W, :], preferred_element_type=F32)
             + jnp.dot(ym_ref[...], wo_ref[RET_W:, :], preferred_element_type=F32))
    h = _rms(x, g_ref[...]).astype(BF16)
    for c in range(D_FF // FF_TILE):
        cols = slice(c * FF_TILE, (c + 1) * FF_TILE)
        g = jnp.dot(h, wg_ref[:, cols], preferred_element_type=F32)
        u = jnp.dot(h, wu_ref[:, cols], preferred_element_type=F32)
        a_ref[:, cols] = (g * jax.nn.sigmoid(g) * u).astype(BF16)
    for src, dst in zip(side_in, side_out):
        dst[...] = src[...].astype(BF16)
    y = jnp.dot(a_ref[...], wd_ref[...], preferred_element_type=F32)
    o_ref[...] = x + 0.5 * y


def _prologue_body(*refs):
    pos_ref, c_ref = refs[:2]
    n = (len(refs) - 3) // 2
    srcs, tab_ref, dsts = refs[2:2 + n], refs[2 + n], refs[3 + n:]
    for src, dst in zip(srcs, dsts):
        dst[...] = src[...].astype(BF16)
    _tables_rows(pos_ref, c_ref, tab_ref, slice(0, pos_ref.shape[0]))


def _prologue(positions, ws):
    t = positions.size
    steps = PROLOGUE_STEPS
    rows = t // steps
    specs = [_side_cast_specs(w, 0, steps) for w in ws]
    outs = pl.pallas_call(
        _prologue_body,
        grid=(steps,),
        in_specs=[pl.BlockSpec((rows, 1), lambda i: (i, 0)), _resident((8, LANES))]
        + [s[0] for s in specs],
        out_specs=[pl.BlockSpec((4, rows, LANES), lambda i: (0, i, 0))] + [s[1] for s in specs],
        out_shape=[jax.ShapeDtypeStruct((4, t, LANES), F32)] + [s[2] for s in specs],
        compiler_params=pltpu.CompilerParams(
            dimension_semantics=("arbitrary",), vmem_limit_bytes=VMEM_LIMIT),
        name="prologue",
    )(positions.reshape(t, 1), _table_consts(), *ws)
    return outs[0], list(outs[1:])


def _side_cast_specs(w, l, steps):
    _, r, c = w.shape
    hold = 1
    while (r * hold) % (steps * BF16_ROWS) or steps % hold:
        hold += 1
    rows = r * hold // steps
    return (pl.BlockSpec((None, rows, c), lambda i: (l, i // hold, 0)),
            pl.BlockSpec((rows, c), lambda i: (i // hold, 0)),
            jax.ShapeDtypeStruct((r, c), BF16))


def _ffn(x, gain, wg, wu, wd, proj=None, side=None):
    t = x.shape[0]
    tm = FFN_TILE
    steps = t // tm
    row = lambda w: pl.BlockSpec((tm, w), lambda i: (i, 0))
    ins, specs = [x], [row(D_MODEL)]
    if proj is not None:
        yr, ym, wo = proj
        ins += [yr, ym, wo]
        specs += [row(RET_W), row(MLA_HEADS * MLA_V), _resident(wo.shape)]
    ins += [gain, wg, wu, wd]
    specs += [_resident(w.shape) for w in (gain, wg, wu, wd)]
    out_specs = [row(D_MODEL)]
    out_shape = [jax.ShapeDtypeStruct((t, D_MODEL), F32)]
    side_ws, side_layer = side if side is not None else ([], 0)
    for w in side_ws:
        in_spec, out_spec, shape = _side_cast_specs(w, side_layer, steps)
        ins.append(w)
        specs.append(in_spec)
        out_specs.append(out_spec)
        out_shape.append(shape)
    outs = pl.pallas_call(
        functools.partial(_ffn_body, with_proj=proj is not None, n_side=len(side_ws)),
        grid=(steps,),
        in_specs=specs,
        out_specs=out_specs,
        out_shape=out_shape,
        scratch_shapes=[pltpu.VMEM((tm, D_FF), BF16)],
        compiler_params=pltpu.CompilerParams(
            dimension_semantics=("arbitrary",), vmem_limit_bytes=FFN_VMEM_LIMIT),
        name="ffn_proj" if proj is not None else "ffn",
    )(*ins)
    return outs[0], list(outs[1:])


def _group_mean(x, gmat):
    x2 = x * x
    hi = x2.astype(BF16)
    lo = (x2 - hi.astype(F32)).astype(BF16)
    return (jnp.dot(hi, gmat, preferred_element_type=F32)
            + jnp.dot(lo, gmat, preferred_element_type=F32))


def _inproj_body(x_ref, gmix_ref, win_ref, krc_ref, tab_ref, gq_ref, wuq_ref, gkv_ref, wk_ref,
                 wv_ref, gmat_ref, vec_ref,
                 rq_ref, rk_ref, rv_ref, rg_ref, qc_ref, kc_ref, v_ref, wlat_ref):
    lat0 = 4 * RET_W

    @pl.when(pl.program_id(0) == 0)
    def _():
        wlat_ref[:, 0:IN_BASE - lat0] = win_ref[:, lat0:]
        wlat_ref[:, IN_BASE - lat0:] = krc_ref[...]

    h = _rms(x_ref[...], gmix_ref[...]).astype(BF16)
    cos_r, sin_r, cos_m, sin_m = tab_ref[0], tab_ref[1], tab_ref[2], tab_ref[3]
    lat = jnp.dot(h, wlat_ref[...], preferred_element_type=F32)
    c_q = lat[:, 0:Q_LORA]
    c_kv = lat[:, Q_LORA:Q_LORA + KV_LORA]
    kr = lat[:, Q_LORA + KV_LORA:]

    cqn = _rms(c_q, gq_ref[...]).astype(BF16)
    ckvn = _rms(c_kv, gkv_ref[...]).astype(BF16)
    qraw = jnp.dot(cqn, wuq_ref[...], preferred_element_type=F32)
    kraw = jnp.dot(ckvn, wk_ref[...], preferred_element_type=F32)
    v_ref[...] = (jnp.dot(ckvn, wv_ref[...], preferred_element_type=F32)
                  + vec_ref[3:4, :]).astype(BF16)

    lane = lax.broadcasted_iota(jnp.int32, (1, LANES), 1)
    in_rope = (lane >= MLA_NOPE) & (lane < MLA_NOPE + MLA_ROPE)
    ss = jnp.sum(jnp.where(in_rope, kr * kr, 0.0), axis=-1, keepdims=True) * (1.0 / MLA_ROPE)
    krn = kr * lax.rsqrt(ss + EPS) * vec_ref[2:3, 0:LANES]
    kro = krn * cos_m + pltpu.roll(krn, SLAB_SHIFT, 1) * sin_m

    gmat = gmat_ref[...]
    for pr in range(MLA_HEADS // 2):
        cols = slice(pr * 2 * LANES, (pr + 1) * 2 * LANES)
        xs = qraw[:, cols]
        y = xs * lax.rsqrt(_group_mean(xs, gmat) + EPS) * vec_ref[0:1, 0:2 * LANES]
        for e in range(2):
            sl = slice(e * LANES, (e + 1) * LANES)
            dst = slice((2 * pr + e) * LANES, (2 * pr + e + 1) * LANES)
            ys = y[:, sl]
            qc_ref[:, dst] = (ys * cos_m + pltpu.roll(ys, SLAB_SHIFT, 1) * sin_m).astype(BF16)
            ks = kraw[:, dst]
            ms = jnp.sum(ks * ks, axis=-1, keepdims=True) * (1.0 / MLA_NOPE)
            kc_ref[:, dst] = (ks * lax.rsqrt(ms + EPS) * vec_ref[1:2, 0:LANES] + kro).astype(BF16)

    def ret_cols(i):
        return jnp.dot(h, win_ref[:, i * RET_W:(i + 1) * RET_W], preferred_element_type=F32)

    ret_scale = RET_DIM ** -0.5
    for dst_ref, scale, i in ((rq_ref, None, 0), (rk_ref, ret_scale, 1)):
        x = ret_cols(i)
        for hd in range(RET_HEADS):
            sl = slice(hd * RET_DIM, (hd + 1) * RET_DIM)
            roped = x[:, sl] * cos_r + pltpu.roll(x[:, sl], RET_DIM // 2, 1) * sin_r
            dst_ref[:, sl] = (roped if scale is None else roped * scale).astype(BF16)
    rv_ref[...] = ret_cols(2).astype(BF16)
    rg_ref[...] = ret_cols(3)


def _inproj(x, tabs, gmix, win, krc, gq, wuq, gkv, wk, wv, gmat, vecs):
    t = x.shape[0]
    tm = TOK_TILE
    row = lambda w: pl.BlockSpec((tm, w), lambda i: (i, 0))
    slab_w = MLA_HEADS * LANES
    outs = [(RET_W, BF16), (RET_W, BF16), (RET_W, BF16), (RET_W, F32),
            (slab_w, BF16), (slab_w, BF16), (slab_w, BF16)]
    return pl.pallas_call(
        _inproj_body,
        grid=(t // tm,),
        in_specs=[row(D_MODEL), _resident(gmix.shape),
                  _resident(win.shape), _resident(krc.shape),
                  pl.BlockSpec((4, tm, LANES), lambda i: (0, i, 0)),
                  _resident(gq.shape), _resident(wuq.shape), _resident(gkv.shape),
                  _resident(wk.shape), _resident(wv.shape), _resident(gmat.shape),
                  _resident(vecs.shape)],
        out_specs=[row(w) for w, _ in outs],
        out_shape=[jax.ShapeDtypeStruct((t, w), dt) for w, dt in outs],
        scratch_shapes=[pltpu.VMEM((D_MODEL, IN_W - 4 * RET_W), BF16)],
        compiler_params=pltpu.CompilerParams(
            dimension_semantics=("arbitrary",), vmem_limit_bytes=VMEM_LIMIT),
        name="inproj",
    )(x, gmix, win, krc, tabs, gq, wuq, gkv, wk, wv, gmat, vecs)


def _retention_body(q_ref, k_ref, v_ref, g_ref, dec_ref, gn_ref, o_ref,
                    state_ref, s_ref, kv_ref, prev_ref):
    @pl.when(pl.program_id(1) == 0)
    def _():
        state_ref[...] = jnp.zeros_like(state_ref)

    c = RET_CHUNK
    nt = (((1,), (1,)), ((), ()))
    n = RET_TILE // c
    blocks = [(hd, ci) for ci in range(n) for hd in range(RET_HEADS)]
    tile = lambda hd, ci: (slice(ci * c, (ci + 1) * c), slice(hd * RET_DIM, (hd + 1) * RET_DIM))

    for hd, ci in blocks:
        rows, cols = tile(hd, ci)
        k = k_ref[0, rows, cols]
        v = v_ref[0, rows, cols]
        s = lax.dot_general(q_ref[0, rows, cols], k, nt, preferred_element_type=F32)
        s_ref[hd * n + ci] = (s * dec_ref[0, hd]).astype(BF16)
        kz = (k.astype(F32) * dec_ref[1, hd]).astype(BF16)
        kv_ref[hd * n + ci] = jnp.dot(kz.T, v, preferred_element_type=F32)
    for hd in range(RET_HEADS):
        st = state_ref[hd]
        for ci in range(n):
            prev_ref[hd * n + ci] = st.astype(BF16)
            st = st * dec_ref[3, hd] + kv_ref[hd * n + ci]
        state_ref[hd] = st
    for hd, ci in blocks:
        rows, cols = tile(hd, ci)
        q = q_ref[0, rows, cols]
        y = (jnp.dot(s_ref[hd * n + ci], v_ref[0, rows, cols], preferred_element_type=F32)
             + jnp.dot(q, prev_ref[hd * n + ci], preferred_element_type=F32) * dec_ref[2, hd])
        g = g_ref[0, rows, cols]
        o_ref[0, rows, cols] = (_rms(y, gn_ref[hd]) * (g * jax.nn.sigmoid(g))).astype(BF16)


def _retention(rq, rk, rv, rg, dec, gn, batch, seq):
    shp = (batch, seq, RET_W)
    nblk = RET_HEADS * (RET_TILE // RET_CHUNK)
    blk = pl.BlockSpec((1, RET_TILE, RET_W), lambda b, s: (b, s, 0))
    return pl.pallas_call(
        _retention_body,
        grid=(batch, seq // RET_TILE),
        in_specs=[blk, blk, blk, blk, _resident(dec.shape), _resident(gn.shape)],
        out_specs=blk,
        out_shape=jax.ShapeDtypeStruct(shp, BF16),
        scratch_shapes=[pltpu.VMEM((RET_HEADS, RET_DIM, RET_DIM), F32),
                        pltpu.VMEM((nblk, RET_CHUNK, RET_CHUNK), BF16),
                        pltpu.VMEM((nblk, RET_DIM, RET_DIM), F32),
                        pltpu.VMEM((nblk, RET_DIM, RET_DIM), BF16)],
        compiler_params=pltpu.CompilerParams(
            dimension_semantics=("parallel", "arbitrary"), vmem_limit_bytes=VMEM_LIMIT),
        name="retention",
    )(rq.reshape(shp), rk.reshape(shp), rv.reshape(shp), rg.reshape(shp), dec, gn)


def _attention_body(q_ref, k_ref, v_ref, o_ref, m_ref, acc_ref):
    qi = pl.program_id(2)
    nt = (((1,), (1,)), ((), ()))
    half = ATT_TQ // 2
    m_ref[...] = jnp.full_like(m_ref, MASK_VALUE)
    acc_ref[...] = jnp.zeros_like(acc_ref)

    def steps(specs, skew):
        cols = lambda e: slice(e * LANES, (e + 1) * LANES)
        keys = lambda start, width: pl.ds(pl.multiple_of(start, half), width)

        def score(spec):
            e, q_rows, start, width, _ = spec
            return lax.dot_general(q_ref[0, q_rows, cols(e)], k_ref[0, keys(start, width), cols(e)],
                                   nt, preferred_element_type=F32)

        def softmax(spec, s):
            e, q_rows, _, width, mask_from = spec
            chunks = [s[:, c * LANES:(c + 1) * LANES] for c in range(width // LANES)]
            if mask_from is not None:
                r = lax.broadcasted_iota(jnp.int32, chunks[0].shape, 0)
                lane = lax.broadcasted_iota(jnp.int32, chunks[0].shape, 1)
                for c in range(mask_from, len(chunks)):
                    chunks[c] = jnp.where(r >= lane + (c - mask_from) * LANES, chunks[c],
                                          MASK_VALUE)
            m_old = m_ref[e, q_rows]
            m_new = jnp.maximum(
                m_old, jnp.max(functools.reduce(jnp.maximum, chunks), axis=-1, keepdims=True))
            m_ref[e, q_rows] = m_new
            return jnp.exp2(m_old - m_new), jnp.concatenate(
                [jnp.exp2(ch - m_new).astype(BF16) for ch in chunks], axis=1)

        def value(spec, alpha, p):
            e, q_rows, start, width, _ = spec
            acc_ref[e, q_rows] = alpha * acc_ref[e, q_rows] + jnp.dot(
                p, v_ref[0, keys(start, width), cols(e)], preferred_element_type=F32)

        n = len(specs)
        scores, probs = {}, {}
        for i in range(n + 2 * skew):
            if i < n:
                scores[i] = score(specs[i])
            if 0 <= i - skew < n:
                probs[i - skew] = softmax(specs[i - skew], scores.pop(i - skew))
            if 0 <= i - 2 * skew < n:
                value(specs[i - 2 * skew], *probs.pop(i - 2 * skew))

    def loop_body(kb, carry):
        for e in range(ATT_HEADS):
            steps([(e, slice(0, ATT_TQ), kb * ATT_TQ, ATT_TQ, None)], 1)
        return carry

    lax.fori_loop(0, qi, loop_body, 0)
    diag = qi * ATT_TQ
    steps([spec for e in range(ATT_HEADS) for spec in (
        (e, slice(0, half), diag, half, 0),
        (e, slice(half, ATT_TQ), diag, ATT_TQ, half // LANES))], DIAG_SKEW)
    lane = lax.broadcasted_iota(jnp.int32, (ATT_TQ, LANES), 1)
    for pr in range(ATT_HEADS // 2):
        o0, o1 = [acc * (1.0 / acc[:, MLA_V:MLA_V + 1])
                  for acc in (acc_ref[2 * pr], acc_ref[2 * pr + 1])]
        o_ref[0, :, pr * LANES:(pr + 1) * LANES] = jnp.where(
            lane < MLA_V, o0, pltpu.roll(o1, MLA_V, 1)).astype(BF16)


def _attention(qc, kc, vc, batch, seq):
    slab_w = MLA_HEADS * LANES
    grp_w = ATT_HEADS * LANES
    out_w = ATT_HEADS * MLA_V
    seq_blk = pl.BlockSpec((1, seq, grp_w), lambda b, j, i: (b, 0, j))
    return pl.pallas_call(
        _attention_body,
        grid=(batch, MLA_HEADS // ATT_HEADS, seq // ATT_TQ),
        in_specs=[pl.BlockSpec((1, ATT_TQ, grp_w), lambda b, j, i: (b, i, j)),
                  seq_blk, seq_blk],
        out_specs=pl.BlockSpec((1, ATT_TQ, out_w), lambda b, j, i: (b, i, j)),
        out_shape=jax.ShapeDtypeStruct((batch, seq, MLA_HEADS * MLA_V), BF16),
        scratch_shapes=[pltpu.VMEM((ATT_HEADS, ATT_TQ, LANES), F32),
                        pltpu.VMEM((ATT_HEADS, ATT_TQ, LANES), F32)],
        compiler_params=pltpu.CompilerParams(
            dimension_semantics=("parallel", "parallel", "arbitrary"),
            vmem_limit_bytes=VMEM_LIMIT),
        name="latent_attention",
    )(qc.reshape(batch, seq, slab_w), kc.reshape(batch, seq, slab_w),
      vc.reshape(batch, seq, slab_w))


def _retention_decay():
    c = RET_CHUNK
    lg = jnp.log(1.0 - 2.0 ** (-5.0 - jnp.arange(RET_HEADS, dtype=F32)))
    idx = jnp.arange(c, dtype=F32)
    rel = idx[:, None] - idx[None, :]
    decay = jnp.where(rel >= 0, jnp.exp(jnp.maximum(rel, 0.0)[None] * lg[:, None, None]), 0.0)
    zeta = jnp.exp((c - 1 - idx)[None, :] * lg[:, None])
    xi = jnp.exp((idx + 1.0)[None, :] * lg[:, None])
    cdec = jnp.exp(c * lg)
    full = (RET_HEADS, c, RET_DIM)
    return jnp.stack([decay,
                      jnp.broadcast_to(zeta[:, :, None], full),
                      jnp.broadcast_to(xi[:, :, None], full),
                      jnp.broadcast_to(cdec[:, None, None], full)])


def _swapped_rope_cols(w):
    a, b = w[..., :HALF_ROPE], w[..., HALF_ROPE:]
    return jnp.concatenate([a, b, b, a], axis=-1)


def _layer_params(l, p, win_bf16):
    row = lambda g: g[l].reshape(1, -1)
    kr_cols = jnp.concatenate(
        [jnp.zeros((D_MODEL, MLA_NOPE), BF16),
         _swapped_rope_cols(win_bf16[l][:, IN_BASE:])], axis=-1)

    dq = MLA_NOPE + MLA_ROPE
    wuq = p["w_uq"][l].reshape(Q_LORA, MLA_HEADS, dq)
    wuq = jnp.concatenate([wuq[..., :MLA_NOPE], _swapped_rope_cols(wuq[..., MLA_NOPE:])], axis=-1)
    wuq = wuq.reshape(Q_LORA, MLA_HEADS * LANES).astype(BF16)
    wukv = p["w_ukv"][l].reshape(KV_LORA, MLA_HEADS, MLA_NOPE + MLA_V)
    wk = jnp.concatenate([wukv[..., :MLA_NOPE], jnp.zeros_like(wukv[..., :MLA_NOPE])], axis=-1)
    wk = wk.reshape(KV_LORA, MLA_HEADS * LANES).astype(BF16)
    wv = jnp.concatenate([wukv[..., MLA_NOPE:], jnp.zeros_like(wukv[..., :LANES - MLA_V])], axis=-1)
    wv = wv.reshape(KV_LORA, MLA_HEADS * LANES).astype(BF16)

    scale = dq ** -0.5 * LOG2_E
    zeros64 = jnp.zeros((MLA_NOPE,), F32)
    gq = jnp.concatenate([p["qn_nope"][l], _swapped_rope_cols(p["qn_rope"][l])]) * scale
    gk = jnp.concatenate([p["kn_nope"][l], zeros64])
    gkr = jnp.concatenate([zeros64, _swapped_rope_cols(p["kn_rope"][l])])
    one_lane = jnp.zeros((LANES,), F32).at[MLA_V].set(1.0)
    vecs = jnp.stack([jnp.tile(r, MLA_HEADS) for r in (gq, gk, gkr, one_lane)]
                     + [jnp.zeros((MLA_HEADS * LANES,), F32)] * 4)
    return dict(
        ffn1_gain=row(p["ffn1_norm"]),
        ffn2_gain=row(p["ffn2_norm"]),
        inproj=(row(p["mix_norm"]), kr_cols, row(p["q_lat_norm"]), wuq, row(p["kv_lat_norm"]),
                wk, wv),
        vecs=vecs,
        gn=p["ret_head_norm"][l].reshape(RET_HEADS, 1, RET_DIM),
    )


def _group_matrix():
    idx = jnp.arange(2 * LANES)
    slab, lane = idx // LANES, idx % LANES
    same = slab[:, None] == slab[None, :]
    nope = (lane[:, None] < MLA_NOPE) & (lane[None, :] < MLA_NOPE)
    rope = ((lane[:, None] >= MLA_NOPE) & (lane[:, None] < MLA_NOPE + MLA_ROPE)
            & (lane[None, :] >= MLA_NOPE))
    g = jnp.where(nope, 1.0 / MLA_NOPE, jnp.where(rope, 1.0 / MLA_ROPE, 0.0))
    return jnp.where(same, g, 0.0).astype(BF16)


def kernel(x, positions, ffn1_norm, ffn1_w_gate, ffn1_w_up, ffn1_w_down, mix_norm, w_in, ret_head_norm, q_lat_norm, w_uq, kv_lat_norm, w_ukv, qn_nope, qn_rope, kn_nope, kn_rope, w_o, ffn2_norm, ffn2_w_gate, ffn2_w_up, ffn2_w_down):
    p = dict(ffn1_norm=ffn1_norm, ffn1_w_gate=ffn1_w_gate, ffn1_w_up=ffn1_w_up,
             ffn1_w_down=ffn1_w_down, mix_norm=mix_norm, w_in=w_in,
             ret_head_norm=ret_head_norm, q_lat_norm=q_lat_norm, w_uq=w_uq,
             kv_lat_norm=kv_lat_norm, w_ukv=w_ukv, qn_nope=qn_nope, qn_rope=qn_rope,
             kn_nope=kn_nope, kn_rope=kn_rope, w_o=w_o, ffn2_norm=ffn2_norm,
             ffn2_w_gate=ffn2_w_gate, ffn2_w_up=ffn2_w_up, ffn2_w_down=ffn2_w_down)
    batch, seq, d = x.shape
    depth = w_in.shape[0]
    assert d == D_MODEL and positions.shape == (batch, seq)
    assert seq % ATT_TQ == 0 and seq % RET_TILE == 0
    assert (batch * seq) % TOK_TILE == 0 and (batch * seq) % FFN_TILE == 0
    assert w_in.shape[1:] == (D_MODEL, IN_BASE + MLA_ROPE) and ffn1_w_gate.shape[1:] == (d, D_FF)
    dec = _retention_decay()
    gmat = _group_matrix()
    ffn1_stacks = [ffn1_w_gate, ffn1_w_up, ffn1_w_down]
    ffn2_stacks = [ffn2_w_gate, ffn2_w_up, ffn2_w_down]
    tabs, w_ffn1 = _prologue(positions, ffn1_stacks)
    win_bf16 = w_in.astype(BF16)
    xf = x.reshape(batch * seq, d)
    for l in range(depth):
        lp = _layer_params(l, p, win_bf16)
        xf, side_w = _ffn(xf, lp["ffn1_gain"], *w_ffn1, side=(ffn2_stacks + [w_o], l))
        w_ffn2, wo = side_w[:3], side_w[3]
        gmix, kr_cols, gq, wuq, gkv, wk, wv = lp["inproj"]
        win_main = win_bf16[l, :, :IN_BASE]
        rq, rk, rv, rg, qc, kc, v = _inproj(xf, tabs, gmix, win_main, kr_cols, gq, wuq, gkv,
                                            wk, wv, gmat, lp["vecs"])
        y_ret = _retention(rq, rk, rv, rg, dec, lp["gn"], batch, seq)
        y_mla = _attention(qc, kc, v, batch, seq)
        xf, w_ffn1 = _ffn(
            xf, lp["ffn2_gain"], *w_ffn2,
            proj=(y_ret.reshape(batch * seq, -1), y_mla.reshape(batch * seq, -1), wo),
            side=(ffn1_stacks, l + 1) if l + 1 < depth else None)
    return xf.reshape(batch, seq, d)
```

```python
import functools

import jax
import jax.numpy as jnp
from jax import lax
from jax.experimental import pallas as pl
from jax.experimental.pallas import tpu as pltpu

D_MODEL = 1024
D_FF = 2816
RET_HEADS = 4
RET_DIM = 128
RET_CHUNK = 128
MLA_HEADS = 8
MLA_NOPE = 64
MLA_ROPE = 32
MLA_V = 64
Q_LORA = 256
KV_LORA = 128
ROPE_THETA = 10000.0
EPS = 1e-6

LANES = 128
BF16_ROWS = 16
RET_W = RET_HEADS * RET_DIM
IN_BASE = 4 * RET_W + Q_LORA + KV_LORA
IN_W = IN_BASE + LANES
HALF_ROPE = MLA_ROPE // 2
SLAB_SHIFT = LANES - MLA_ROPE

TOK_TILE = 512
FF_TILE = 256
PROLOGUE_STEPS = 16
ATT_TQ = 1024
ATT_HEADS = 4
DIAG_SKEW = 2
RET_TILE = 1024
RET_SLOTS = 3
V7X_VMEM_BYTES = 64 * 1024 * 1024
VMEM_LIMIT = V7X_VMEM_BYTES * 3 // 4
FFN_TILE = 1024
FFN_VMEM_LIMIT = V7X_VMEM_BYTES * 29 // 32
MASK_VALUE = -1e30
LOG2_E = 1.4426950408889634

F32 = jnp.float32
BF16 = jnp.bfloat16


def _rms(x, gain):
    return x * lax.rsqrt(jnp.mean(x * x, axis=-1, keepdims=True) + EPS) * gain


def _resident(shape):
    nd = len(shape)
    return pl.BlockSpec(shape, lambda *_: (0,) * nd)


def _tables_rows(pos_ref, c_ref, o_ref, rows):
    ang = pos_ref[rows, :].astype(F32) * c_ref[0:1, :]
    c, s = jnp.cos(ang), jnp.sin(ang)
    low = lax.broadcasted_iota(jnp.int32, c.shape, 1) < RET_DIM // 2
    o_ref[0, rows] = jnp.where(low, c, pltpu.roll(c, RET_DIM // 2, 1))
    o_ref[1, rows] = jnp.where(low, s, pltpu.roll(s, RET_DIM // 2, 1)) * c_ref[1:2, :]
    o_ref[2, rows] = c_ref[2:3, :] + c_ref[3:4, :] * c
    o_ref[3, rows] = s * c_ref[4:5, :]


def _table_consts():
    inv_r = 1.0 / (ROPE_THETA ** (jnp.arange(0, RET_DIM, 2, dtype=F32) / RET_DIM))
    inv_m = 1.0 / (ROPE_THETA ** (jnp.arange(0, MLA_ROPE, 2, dtype=F32) / MLA_ROPE))
    half = RET_DIM // 2
    z = lambda n: jnp.zeros((n,), F32)
    o = lambda n: jnp.ones((n,), F32)
    return jnp.stack([
        jnp.concatenate([inv_r, inv_m, inv_m, z(MLA_ROPE)]),
        jnp.concatenate([-o(half), o(half)]),
        jnp.concatenate([o(MLA_NOPE), z(LANES - MLA_NOPE)]),
        jnp.concatenate([z(MLA_NOPE), o(MLA_ROPE), z(MLA_ROPE)]),
        jnp.concatenate([z(MLA_NOPE), -o(HALF_ROPE), o(HALF_ROPE), z(MLA_ROPE)]),
        z(LANES), z(LANES), z(LANES)])


def _ffn_body(*refs, with_proj, n_side):
    refs = list(refs)
    take = lambda n: [refs.pop(0) for _ in range(n)]
    (x_ref,) = take(1)
    if with_proj:
        yr_ref, ym_ref, wo_ref = take(3)
    g_ref, wg_ref, wu_ref, wd_ref = take(4)
    side_in = take(n_side)
    (o_ref,) = take(1)
    side_out = take(n_side)
    (a_ref,) = take(1)

    x = x_ref[...]
    if with_proj:
        x = (x + jnp.dot(yr_ref[...], wo_ref[0:RET_W, :], preferred_element_type=F32)
             + jnp.dot(ym_ref[...], wo_ref[RET_W:, :], preferred_element_type=F32))
    h = _rms(x, g_ref[...]).astype(BF16)
    for c in range(D_FF // FF_TILE):
        cols = slice(c * FF_TILE, (c + 1) * FF_TILE)
        g = jnp.dot(h, wg_ref[:, cols], preferred_element_type=F32)
        u = jnp.dot(h, wu_ref[:, cols], preferred_element_type=F32)
        a_ref[:, cols] = (g * jax.nn.sigmoid(g) * u).astype(BF16)
    for src, dst in zip(side_in, side_out):
        dst[...] = src[...].astype(BF16)
    y = jnp.dot(a_ref[...], wd_ref[...], preferred_element_type=F32)
    o_ref[...] = x + 0.5 * y


def _prologue_body(*refs):
    pos_ref, c_ref = refs[:2]
    n = (len(refs) - 3) // 2
    srcs, tab_ref, dsts = refs[2:2 + n], refs[2 + n], refs[3 + n:]
    for src, dst in zip(srcs, dsts):
        dst[...] = src[...].astype(BF16)
    _tables_rows(pos_ref, c_ref, tab_ref, slice(0, pos_ref.shape[0]))


def _prologue(positions, ws):
    t = positions.size
    steps = PROLOGUE_STEPS
    rows = t // steps
    specs = [_side_cast_specs(w, 0, steps) for w in ws]
    outs = pl.pallas_call(
        _prologue_body,
        grid=(steps,),
        in_specs=[pl.BlockSpec((rows, 1), lambda i: (i, 0)), _resident((8, LANES))]
        + [s[0] for s in specs],
        out_specs=[pl.BlockSpec((4, rows, LANES), lambda i: (0, i, 0))] + [s[1] for s in specs],
        out_shape=[jax.ShapeDtypeStruct((4, t, LANES), F32)] + [s[2] for s in specs],
        compiler_params=pltpu.CompilerParams(
            dimension_semantics=("arbitrary",), vmem_limit_bytes=VMEM_LIMIT),
        name="prologue",
    )(positions.reshape(t, 1), _table_consts(), *ws)
    return outs[0], list(outs[1:])


def _side_cast_specs(w, l, steps):
    _, r, c = w.shape
    hold = 1
    while (r * hold) % (steps * BF16_ROWS) or steps % hold:
        hold += 1
    rows = r * hold // steps
    return (pl.BlockSpec((None, rows, c), lambda i: (l, i // hold, 0)),
            pl.BlockSpec((rows, c), lambda i: (i // hold, 0)),
            jax.ShapeDtypeStruct((r, c), BF16))


def _ffn(x, gain, wg, wu, wd, proj=None, side=None):
    t = x.shape[0]
    tm = FFN_TILE
    steps = t // tm
    row = lambda w: pl.BlockSpec((tm, w), lambda i: (i, 0))
    ins, specs = [x], [row(D_MODEL)]
    if proj is not None:
        yr, ym, wo = proj
        ins += [yr, ym, wo]
        specs += [row(RET_W), row(MLA_HEADS * MLA_V), _resident(wo.shape)]
    ins += [gain, wg, wu, wd]
    specs += [_resident(w.shape) for w in (gain, wg, wu, wd)]
    out_specs = [row(D_MODEL)]
    out_shape = [jax.ShapeDtypeStruct((t, D_MODEL), F32)]
    side_ws, side_layer = side if side is not None else ([], 0)
    for w in side_ws:
        in_spec, out_spec, shape = _side_cast_specs(w, side_layer, steps)
        ins.append(w)
        specs.append(in_spec)
        out_specs.append(out_spec)
        out_shape.append(shape)
    outs = pl.pallas_call(
        functools.partial(_ffn_body, with_proj=proj is not None, n_side=len(side_ws)),
        grid=(steps,),
        in_specs=specs,
        out_specs=out_specs,
        out_shape=out_shape,
        scratch_shapes=[pltpu.VMEM((tm, D_FF), BF16)],
        compiler_params=pltpu.CompilerParams(
            dimension_semantics=("arbitrary",), vmem_limit_bytes=FFN_VMEM_LIMIT),
        name="ffn_proj" if proj is not None else "ffn",
    )(*ins)
    return outs[0], list(outs[1:])


def _group_mean(x, gmat):
    x2 = x * x
    hi = x2.astype(BF16)
    lo = (x2 - hi.astype(F32)).astype(BF16)
    return (jnp.dot(hi, gmat, preferred_element_type=F32)
            + jnp.dot(lo, gmat, preferred_element_type=F32))


def _inproj_body(x_ref, gmix_ref, win_ref, krc_ref, tab_ref, gq_ref, wuq_ref, gkv_ref, wk_ref,
                 wv_ref, gmat_ref, vec_ref,
                 rq_ref, rk_ref, rv_ref, rg_ref, qc_ref, kc_ref, v_ref, wlat_ref):
    lat0 = 4 * RET_W

    @pl.when(pl.program_id(0) == 0)
    def _():
        wlat_ref[:, 0:IN_BASE - lat0] = win_ref[:, lat0:IN_BASE]
        wlat_ref[:, IN_BASE - lat0:] = krc_ref[...]

    h = _rms(x_ref[...], gmix_ref[...]).astype(BF16)
    cos_r, sin_r, cos_m, sin_m = tab_ref[0], tab_ref[1], tab_ref[2], tab_ref[3]
    lat = jnp.dot(h, wlat_ref[...], preferred_element_type=F32)
    c_q = lat[:, 0:Q_LORA]
    c_kv = lat[:, Q_LORA:Q_LORA + KV_LORA]
    kr = lat[:, Q_LORA + KV_LORA:]

    cqn = _rms(c_q, gq_ref[...]).astype(BF16)
    ckvn = _rms(c_kv, gkv_ref[...]).astype(BF16)
    qraw = jnp.dot(cqn, wuq_ref[...], preferred_element_type=F32)
    kraw = jnp.dot(ckvn, wk_ref[...], preferred_element_type=F32)
    v_ref[...] = (jnp.dot(ckvn, wv_ref[...], preferred_element_type=F32)
                  + vec_ref[3:4, :]).astype(BF16)

    lane = lax.broadcasted_iota(jnp.int32, (1, LANES), 1)
    in_rope = (lane >= MLA_NOPE) & (lane < MLA_NOPE + MLA_ROPE)
    ss = jnp.sum(jnp.where(in_rope, kr * kr, 0.0), axis=-1, keepdims=True) * (1.0 / MLA_ROPE)
    krn = kr * lax.rsqrt(ss + EPS) * vec_ref[2:3, 0:LANES]
    kro = krn * cos_m + pltpu.roll(krn, SLAB_SHIFT, 1) * sin_m

    gmat = gmat_ref[...]
    for pr in range(MLA_HEADS // 2):
        cols = slice(pr * 2 * LANES, (pr + 1) * 2 * LANES)
        xs = qraw[:, cols]
        y = xs * lax.rsqrt(_group_mean(xs, gmat) + EPS) * vec_ref[0:1, 0:2 * LANES]
        for e in range(2):
            sl = slice(e * LANES, (e + 1) * LANES)
            dst = slice((2 * pr + e) * LANES, (2 * pr + e + 1) * LANES)
            ys = y[:, sl]
            qc_ref[:, dst] = (ys * cos_m + pltpu.roll(ys, SLAB_SHIFT, 1) * sin_m).astype(BF16)
            ks = kraw[:, dst]
            ms = jnp.sum(ks * ks, axis=-1, keepdims=True) * (1.0 / MLA_NOPE)
            kc_ref[:, dst] = (ks * lax.rsqrt(ms + EPS) * vec_ref[1:2, 0:LANES] + kro).astype(BF16)

    def ret_cols(i):
        return jnp.dot(h, win_ref[:, i * RET_W:(i + 1) * RET_W], preferred_element_type=F32)

    ret_scale = RET_DIM ** -0.5
    for dst_ref, scale, i in ((rq_ref, None, 0), (rk_ref, ret_scale, 1)):
        x = ret_cols(i)
        for hd in range(RET_HEADS):
            sl = slice(hd * RET_DIM, (hd + 1) * RET_DIM)
            roped = x[:, sl] * cos_r + pltpu.roll(x[:, sl], RET_DIM // 2, 1) * sin_r
            dst_ref[:, sl] = (roped if scale is None else roped * scale).astype(BF16)
    rv_ref[...] = ret_cols(2).astype(BF16)
    rg_ref[...] = ret_cols(3)


def _inproj(x, tabs, gmix, win, krc, gq, wuq, gkv, wk, wv, gmat, vecs):
    t = x.shape[0]
    tm = TOK_TILE
    row = lambda w: pl.BlockSpec((tm, w), lambda i: (i, 0))
    slab_w = MLA_HEADS * LANES
    outs = [(RET_W, BF16), (RET_W, BF16), (RET_W, BF16), (RET_W, F32),
            (slab_w, BF16), (slab_w, BF16), (slab_w, BF16)]
    return pl.pallas_call(
        _inproj_body,
        grid=(t // tm,),
        in_specs=[row(D_MODEL), _resident(gmix.shape),
                  _resident(win.shape), _resident(krc.shape),
                  pl.BlockSpec((4, tm, LANES), lambda i: (0, i, 0)),
                  _resident(gq.shape), _resident(wuq.shape), _resident(gkv.shape),
                  _resident(wk.shape), _resident(wv.shape), _resident(gmat.shape),
                  _resident(vecs.shape)],
        out_specs=[row(w) for w, _ in outs],
        out_shape=[jax.ShapeDtypeStruct((t, w), dt) for w, dt in outs],
        scratch_shapes=[pltpu.VMEM((D_MODEL, IN_W - 4 * RET_W), BF16)],
        compiler_params=pltpu.CompilerParams(
            dimension_semantics=("arbitrary",), vmem_limit_bytes=VMEM_LIMIT),
        name="inproj",
    )(x, gmix, win, krc, tabs, gq, wuq, gkv, wk, wv, gmat, vecs)


def _retention_body(q_hbm, k_hbm, v_hbm, g_hbm, dec_ref, gn_ref, o_ref,
                    state_ref, s_ref, kv_ref, prev_ref, q_buf, k_buf, v_buf, g_buf, sem,
                    *, tiles_per_seq, n_steps):
    step = pl.program_id(0) * tiles_per_seq + pl.program_id(1)

    def tile_copies(t):
        slot = lax.rem(t, RET_SLOTS)
        rows = pl.ds(pl.multiple_of(lax.rem(t, tiles_per_seq) * RET_TILE, RET_TILE), RET_TILE)
        b = lax.div(t, tiles_per_seq)
        return [pltpu.make_async_copy(src.at[b, rows, :], buf.at[slot], sem.at[i, slot])
                for i, (src, buf) in enumerate(((q_hbm, q_buf), (k_hbm, k_buf),
                                                (v_hbm, v_buf), (g_hbm, g_buf)))]

    @pl.when(step == 0)
    def _():
        for t in range(min(RET_SLOTS - 1, n_steps)):
            for cp in tile_copies(jnp.int32(t)):
                cp.start()

    @pl.when(step + (RET_SLOTS - 1) < n_steps)
    def _():
        for cp in tile_copies(step + (RET_SLOTS - 1)):
            cp.start()

    for cp in tile_copies(step):
        cp.wait()

    @pl.when(pl.program_id(1) == 0)
    def _():
        state_ref[...] = jnp.zeros_like(state_ref)

    slot = lax.rem(step, RET_SLOTS)
    q_ref, k_ref, v_ref, g_ref = q_buf.at[slot], k_buf.at[slot], v_buf.at[slot], g_buf.at[slot]
    c = RET_CHUNK
    nt = (((1,), (1,)), ((), ()))
    n = RET_TILE // c
    blocks = [(hd, ci) for ci in range(n) for hd in range(RET_HEADS)]
    tile = lambda hd, ci: (slice(ci * c, (ci + 1) * c), slice(hd * RET_DIM, (hd + 1) * RET_DIM))

    for hd, ci in blocks:
        rows, cols = tile(hd, ci)
        k = k_ref[rows, cols]
        v = v_ref[rows, cols]
        s = lax.dot_general(q_ref[rows, cols], k, nt, preferred_element_type=F32)
        s_ref[hd * n + ci] = (s * dec_ref[0, hd]).astype(BF16)
        kz = (k.astype(F32) * dec_ref[1, hd]).astype(BF16)
        kv_ref[hd * n + ci] = jnp.dot(kz.T, v, preferred_element_type=F32)
    for hd in range(RET_HEADS):
        st = state_ref[hd]
        for ci in range(n):
            prev_ref[hd * n + ci] = st.astype(BF16)
            st = st * dec_ref[3, hd] + kv_ref[hd * n + ci]
        state_ref[hd] = st
    for hd, ci in blocks:
        rows, cols = tile(hd, ci)
        q = q_ref[rows, cols]
        y = (jnp.dot(s_ref[hd * n + ci], v_ref[rows, cols], preferred_element_type=F32)
             + jnp.dot(q, prev_ref[hd * n + ci], preferred_element_type=F32) * dec_ref[2, hd])
        g = g_ref[rows, cols]
        o_ref[0, rows, cols] = (_rms(y, gn_ref[hd]) * (g * jax.nn.sigmoid(g))).astype(BF16)


def _retention(rq, rk, rv, rg, dec, gn, batch, seq):
    shp = (batch, seq, RET_W)
    nblk = RET_HEADS * (RET_TILE // RET_CHUNK)
    blk = pl.BlockSpec((1, RET_TILE, RET_W), lambda b, s: (b, s, 0))
    hbm = pl.BlockSpec(memory_space=pl.ANY)
    tiles_per_seq = seq // RET_TILE
    ring = lambda dt: pltpu.VMEM((RET_SLOTS, RET_TILE, RET_W), dt)
    return pl.pallas_call(
        functools.partial(_retention_body, tiles_per_seq=tiles_per_seq,
                          n_steps=batch * tiles_per_seq),
        grid=(batch, tiles_per_seq),
        in_specs=[hbm, hbm, hbm, hbm, _resident(dec.shape), _resident(gn.shape)],
        out_specs=blk,
        out_shape=jax.ShapeDtypeStruct(shp, BF16),
        scratch_shapes=[pltpu.VMEM((RET_HEADS, RET_DIM, RET_DIM), F32),
                        pltpu.VMEM((nblk, RET_CHUNK, RET_CHUNK), BF16),
                        pltpu.VMEM((nblk, RET_DIM, RET_DIM), F32),
                        pltpu.VMEM((nblk, RET_DIM, RET_DIM), BF16),
                        ring(BF16), ring(BF16), ring(BF16), ring(F32),
                        pltpu.SemaphoreType.DMA((4, RET_SLOTS))],
        compiler_params=pltpu.CompilerParams(
            dimension_semantics=("arbitrary", "arbitrary"), vmem_limit_bytes=VMEM_LIMIT),
        name="retention",
    )(rq.reshape(shp), rk.reshape(shp), rv.reshape(shp), rg.reshape(shp), dec, gn)


def _attention_body(q_ref, k_ref, v_ref, o_ref, m_ref, acc_ref):
    qi = pl.program_id(2)
    nt = (((1,), (1,)), ((), ()))
    half = ATT_TQ // 2
    m_ref[...] = jnp.full_like(m_ref, MASK_VALUE)
    acc_ref[...] = jnp.zeros_like(acc_ref)

    def steps(specs, skew):
        cols = lambda e: slice(e * LANES, (e + 1) * LANES)
        keys = lambda start, width: pl.ds(pl.multiple_of(start, half), width)

        def score(spec):
            e, q_rows, start, width, _ = spec
            return lax.dot_general(q_ref[0, q_rows, cols(e)], k_ref[0, keys(start, width), cols(e)],
                                   nt, preferred_element_type=F32)

        def softmax(spec, s):
            e, q_rows, _, width, mask_from = spec
            chunks = [s[:, c * LANES:(c + 1) * LANES] for c in range(width // LANES)]
            if mask_from is not None:
                r = lax.broadcasted_iota(jnp.int32, chunks[0].shape, 0)
                lane = lax.broadcasted_iota(jnp.int32, chunks[0].shape, 1)
                for c in range(mask_from, len(chunks)):
                    chunks[c] = jnp.where(r >= lane + (c - mask_from) * LANES, chunks[c],
                                          MASK_VALUE)
            m_old = m_ref[e, q_rows]
            m_new = jnp.maximum(
                m_old, jnp.max(functools.reduce(jnp.maximum, chunks), axis=-1, keepdims=True))
            m_ref[e, q_rows] = m_new
            return jnp.exp2(m_old - m_new), jnp.concatenate(
                [jnp.exp2(ch - m_new).astype(BF16) for ch in chunks], axis=1)

        def value(spec, alpha, p):
            e, q_rows, start, width, _ = spec
            acc_ref[e, q_rows] = alpha * acc_ref[e, q_rows] + jnp.dot(
                p, v_ref[0, keys(start, width), cols(e)], preferred_element_type=F32)

        n = len(specs)
        scores, probs = {}, {}
        for i in range(n + 2 * skew):
            if i < n:
                scores[i] = score(specs[i])
            if 0 <= i - skew < n:
                probs[i - skew] = softmax(specs[i - skew], scores.pop(i - skew))
            if 0 <= i - 2 * skew < n:
                value(specs[i - 2 * skew], *probs.pop(i - 2 * skew))

    def loop_body(kb, carry):
        for e in range(ATT_HEADS):
            steps([(e, slice(0, ATT_TQ), kb * ATT_TQ, ATT_TQ, None)], 1)
        return carry

    lax.fori_loop(0, qi, loop_body, 0)
    diag = qi * ATT_TQ
    steps([spec for e in range(ATT_HEADS) for spec in (
        (e, slice(0, half), diag, half, 0),
        (e, slice(half, ATT_TQ), diag, ATT_TQ, half // LANES))], DIAG_SKEW)
    lane = lax.broadcasted_iota(jnp.int32, (ATT_TQ, LANES), 1)
    for pr in range(ATT_HEADS // 2):
        o0, o1 = [acc * (1.0 / acc[:, MLA_V:MLA_V + 1])
                  for acc in (acc_ref[2 * pr], acc_ref[2 * pr + 1])]
        o_ref[0, :, pr * LANES:(pr + 1) * LANES] = jnp.where(
            lane < MLA_V, o0, pltpu.roll(o1, MLA_V, 1)).astype(BF16)


def _attention(qc, kc, vc, batch, seq):
    slab_w = MLA_HEADS * LANES
    grp_w = ATT_HEADS * LANES
    out_w = ATT_HEADS * MLA_V
    seq_blk = pl.BlockSpec((1, seq, grp_w), lambda b, j, i: (b, 0, j))
    return pl.pallas_call(
        _attention_body,
        grid=(batch, MLA_HEADS // ATT_HEADS, seq // ATT_TQ),
        in_specs=[pl.BlockSpec((1, ATT_TQ, grp_w), lambda b, j, i: (b, i, j)),
                  seq_blk, seq_blk],
        out_specs=pl.BlockSpec((1, ATT_TQ, out_w), lambda b, j, i: (b, i, j)),
        out_shape=jax.ShapeDtypeStruct((batch, seq, MLA_HEADS * MLA_V), BF16),
        scratch_shapes=[pltpu.VMEM((ATT_HEADS, ATT_TQ, LANES), F32),
                        pltpu.VMEM((ATT_HEADS, ATT_TQ, LANES), F32)],
        compiler_params=pltpu.CompilerParams(
            dimension_semantics=("parallel", "parallel", "arbitrary"),
            vmem_limit_bytes=VMEM_LIMIT),
        name="latent_attention",
    )(qc.reshape(batch, seq, slab_w), kc.reshape(batch, seq, slab_w),
      vc.reshape(batch, seq, slab_w))


def _retention_decay():
    c = RET_CHUNK
    lg = jnp.log(1.0 - 2.0 ** (-5.0 - jnp.arange(RET_HEADS, dtype=F32)))
    idx = jnp.arange(c, dtype=F32)
    rel = idx[:, None] - idx[None, :]
    decay = jnp.where(rel >= 0, jnp.exp(jnp.maximum(rel, 0.0)[None] * lg[:, None, None]), 0.0)
    zeta = jnp.exp((c - 1 - idx)[None, :] * lg[:, None])
    xi = jnp.exp((idx + 1.0)[None, :] * lg[:, None])
    cdec = jnp.exp(c * lg)
    full = (RET_HEADS, c, RET_DIM)
    return jnp.stack([decay,
                      jnp.broadcast_to(zeta[:, :, None], full),
                      jnp.broadcast_to(xi[:, :, None], full),
                      jnp.broadcast_to(cdec[:, None, None], full)])


def _swapped_rope_cols(w):
    a, b = w[..., :HALF_ROPE], w[..., HALF_ROPE:]
    return jnp.concatenate([a, b, b, a], axis=-1)


def _layer_params(l, p, win):
    row = lambda g: g[l].reshape(1, -1)
    kr_cols = jnp.concatenate(
        [jnp.zeros((D_MODEL, MLA_NOPE), BF16), _swapped_rope_cols(win[:, IN_BASE:])], axis=-1)

    dq = MLA_NOPE + MLA_ROPE
    wuq = p["w_uq"][l].reshape(Q_LORA, MLA_HEADS, dq)
    wuq = jnp.concatenate([wuq[..., :MLA_NOPE], _swapped_rope_cols(wuq[..., MLA_NOPE:])], axis=-1)
    wuq = wuq.reshape(Q_LORA, MLA_HEADS * LANES).astype(BF16)
    wukv = p["w_ukv"][l].reshape(KV_LORA, MLA_HEADS, MLA_NOPE + MLA_V)
    wk = jnp.concatenate([wukv[..., :MLA_NOPE], jnp.zeros_like(wukv[..., :MLA_NOPE])], axis=-1)
    wk = wk.reshape(KV_LORA, MLA_HEADS * LANES).astype(BF16)
    wv = jnp.concatenate([wukv[..., MLA_NOPE:], jnp.zeros_like(wukv[..., :LANES - MLA_V])], axis=-1)
    wv = wv.reshape(KV_LORA, MLA_HEADS * LANES).astype(BF16)

    scale = dq ** -0.5 * LOG2_E
    zeros64 = jnp.zeros((MLA_NOPE,), F32)
    gq = jnp.concatenate([p["qn_nope"][l], _swapped_rope_cols(p["qn_rope"][l])]) * scale
    gk = jnp.concatenate([p["kn_nope"][l], zeros64])
    gkr = jnp.concatenate([zeros64, _swapped_rope_cols(p["kn_rope"][l])])
    one_lane = jnp.zeros((LANES,), F32).at[MLA_V].set(1.0)
    vecs = jnp.stack([jnp.tile(r, MLA_HEADS) for r in (gq, gk, gkr, one_lane)]
                     + [jnp.zeros((MLA_HEADS * LANES,), F32)] * 4)
    return dict(
        ffn1_gain=row(p["ffn1_norm"]),
        ffn2_gain=row(p["ffn2_norm"]),
        inproj=(row(p["mix_norm"]), kr_cols, row(p["q_lat_norm"]), wuq, row(p["kv_lat_norm"]),
                wk, wv),
        vecs=vecs,
        gn=p["ret_head_norm"][l].reshape(RET_HEADS, 1, RET_DIM),
    )


def _group_matrix():
    idx = jnp.arange(2 * LANES)
    slab, lane = idx // LANES, idx % LANES
    same = slab[:, None] == slab[None, :]
    nope = (lane[:, None] < MLA_NOPE) & (lane[None, :] < MLA_NOPE)
    rope = ((lane[:, None] >= MLA_NOPE) & (lane[:, None] < MLA_NOPE + MLA_ROPE)
            & (lane[None, :] >= MLA_NOPE))
    g = jnp.where(nope, 1.0 / MLA_NOPE, jnp.where(rope, 1.0 / MLA_ROPE, 0.0))
    return jnp.where(same, g, 0.0).astype(BF16)


def kernel(x, positions, ffn1_norm, ffn1_w_gate, ffn1_w_up, ffn1_w_down, mix_norm, w_in, ret_head_norm, q_lat_norm, w_uq, kv_lat_norm, w_ukv, qn_nope, qn_rope, kn_nope, kn_rope, w_o, ffn2_norm, ffn2_w_gate, ffn2_w_up, ffn2_w_down):
    p = dict(ffn1_norm=ffn1_norm, ffn1_w_gate=ffn1_w_gate, ffn1_w_up=ffn1_w_up,
             ffn1_w_down=ffn1_w_down, mix_norm=mix_norm, w_in=w_in,
             ret_head_norm=ret_head_norm, q_lat_norm=q_lat_norm, w_uq=w_uq,
             kv_lat_norm=kv_lat_norm, w_ukv=w_ukv, qn_nope=qn_nope, qn_rope=qn_rope,
             kn_nope=kn_nope, kn_rope=kn_rope, w_o=w_o, ffn2_norm=ffn2_norm,
             ffn2_w_gate=ffn2_w_gate, ffn2_w_up=ffn2_w_up, ffn2_w_down=ffn2_w_down)
    batch, seq, d = x.shape
    depth = w_in.shape[0]
    assert d == D_MODEL and positions.shape == (batch, seq)
    assert seq % ATT_TQ == 0 and seq % RET_TILE == 0
    assert (batch * seq) % TOK_TILE == 0 and (batch * seq) % FFN_TILE == 0
    assert w_in.shape[1:] == (D_MODEL, IN_BASE + MLA_ROPE) and ffn1_w_gate.shape[1:] == (d, D_FF)
    dec = _retention_decay()
    gmat = _group_matrix()
    ffn1_stacks = [ffn1_w_gate, ffn1_w_up, ffn1_w_down]
    ffn2_stacks = [ffn2_w_gate, ffn2_w_up, ffn2_w_down]
    tabs, w_ffn1 = _prologue(positions, ffn1_stacks)
    xf = x.reshape(batch * seq, d)
    for l in range(depth):
        win = w_in[l].astype(BF16)
        lp = _layer_params(l, p, win)
        xf, side_w = _ffn(xf, lp["ffn1_gain"], *w_ffn1, side=(ffn2_stacks + [w_o], l))
        w_ffn2, wo = side_w[:3], side_w[3]
        gmix, kr_cols, gq, wuq, gkv, wk, wv = lp["inproj"]
        rq, rk, rv, rg, qc, kc, v = _inproj(xf, tabs, gmix, win, kr_cols, gq, wuq, gkv,
                                            wk, wv, gmat, lp["vecs"])
        y_ret = _retention(rq, rk, rv, rg, dec, lp["gn"], batch, seq)
        y_mla = _attention(qc, kc, v, batch, seq)
        xf, w_ffn1 = _ffn(
            xf, lp["ffn2_gain"], *w_ffn2,
            proj=(y_ret.reshape(batch * seq, -1), y_mla.reshape(batch * seq, -1), wo),
            side=(ffn1_stacks, l + 1) if l + 1 < depth else None)
    return xf.reshape(batch, seq, d)
```

```python
import functools

import jax
import jax.numpy as jnp
from jax import lax
from jax.experimental import pallas as pl
from jax.experimental.pallas import tpu as pltpu

D_MODEL = 1024
D_FF = 2816
RET_HEADS = 4
RET_DIM = 128
RET_CHUNK = 128
MLA_HEADS = 8
MLA_NOPE = 64
MLA_ROPE = 32
MLA_V = 64
Q_LORA = 256
KV_LORA = 128
ROPE_THETA = 10000.0
EPS = 1e-6

LANES = 128
BF16_ROWS = 16
RET_W = RET_HEADS * RET_DIM
IN_BASE = 4 * RET_W + Q_LORA + KV_LORA
IN_W = IN_BASE + LANES
HALF_ROPE = MLA_ROPE // 2
SLAB_SHIFT = LANES - MLA_ROPE

TOK_TILE = 512
FF_TILE = 256
PROLOGUE_STEPS = 16
ATT_TQ = 1024
ATT_HEADS = 4
DIAG_SKEW = 2
RET_TILE = 1024
RET_SLOTS = 4
V7X_VMEM_BYTES = 64 * 1024 * 1024
VMEM_LIMIT = V7X_VMEM_BYTES * 3 // 4
FFN_TILE = 1024
FFN_VMEM_LIMIT = V7X_VMEM_BYTES * 29 // 32
MASK_VALUE = -1e30
LOG2_E = 1.4426950408889634

F32 = jnp.float32
BF16 = jnp.bfloat16


def _rms(x, gain):
    return x * lax.rsqrt(jnp.mean(x * x, axis=-1, keepdims=True) + EPS) * gain


def _resident(shape):
    nd = len(shape)
    return pl.BlockSpec(shape, lambda *_: (0,) * nd)


def _tables_rows(pos_ref, c_ref, o_ref, rows):
    ang = pos_ref[rows, :].astype(F32) * c_ref[0:1, :]
    c, s = jnp.cos(ang), jnp.sin(ang)
    low = lax.broadcasted_iota(jnp.int32, c.shape, 1) < RET_DIM // 2
    o_ref[0, rows] = jnp.where(low, c, pltpu.roll(c, RET_DIM // 2, 1))
    o_ref[1, rows] = jnp.where(low, s, pltpu.roll(s, RET_DIM // 2, 1)) * c_ref[1:2, :]
    o_ref[2, rows] = c_ref[2:3, :] + c_ref[3:4, :] * c
    o_ref[3, rows] = s * c_ref[4:5, :]


def _table_consts():
    inv_r = 1.0 / (ROPE_THETA ** (jnp.arange(0, RET_DIM, 2, dtype=F32) / RET_DIM))
    inv_m = 1.0 / (ROPE_THETA ** (jnp.arange(0, MLA_ROPE, 2, dtype=F32) / MLA_ROPE))
    half = RET_DIM // 2
    z = lambda n: jnp.zeros((n,), F32)
    o = lambda n: jnp.ones((n,), F32)
    return jnp.stack([
        jnp.concatenate([inv_r, inv_m, inv_m, z(MLA_ROPE)]),
        jnp.concatenate([-o(half), o(half)]),
        jnp.concatenate([o(MLA_NOPE), z(LANES - MLA_NOPE)]),
        jnp.concatenate([z(MLA_NOPE), o(MLA_ROPE), z(MLA_ROPE)]),
        jnp.concatenate([z(MLA_NOPE), -o(HALF_ROPE), o(HALF_ROPE), z(MLA_ROPE)]),
        z(LANES), z(LANES), z(LANES)])


def _ffn_body(*refs, with_proj, n_side):
    refs = list(refs)
    take = lambda n: [refs.pop(0) for _ in range(n)]
    (x_ref,) = take(1)
    if with_proj:
        yr_ref, ym_ref, wo_ref = take(3)
    g_ref, wg_ref, wu_ref, wd_ref = take(4)
    side_in = take(n_side)
    (o_ref,) = take(1)
    side_out = take(n_side)
    (a_ref,) = take(1)

    x = x_ref[...]
    if with_proj:
        x = (x + jnp.dot(yr_ref[...], wo_ref[0:RET_W, :], preferred_element_type=F32)
             + jnp.dot(ym_ref[...], wo_ref[RET_W:, :], preferred_element_type=F32))
    h = _rms(x, g_ref[...]).astype(BF16)
    for c in range(D_FF // FF_TILE):
        cols = slice(c * FF_TILE, (c + 1) * FF_TILE)
        g = jnp.dot(h, wg_ref[:, cols], preferred_element_type=F32)
        u = jnp.dot(h, wu_ref[:, cols], preferred_element_type=F32)
        a_ref[:, cols] = (g * jax.nn.sigmoid(g) * u).astype(BF16)
    for src, dst in zip(side_in, side_out):
        dst[...] = src[...].astype(BF16)
    y = jnp.dot(a_ref[...], wd_ref[...], preferred_element_type=F32)
    o_ref[...] = x + 0.5 * y


def _prologue_body(*refs):
    pos_ref, c_ref = refs[:2]
    n = (len(refs) - 3) // 2
    srcs, tab_ref, dsts = refs[2:2 + n], refs[2 + n], refs[3 + n:]
    for src, dst in zip(srcs, dsts):
        dst[...] = src[...].astype(BF16)
    _tables_rows(pos_ref, c_ref, tab_ref, slice(0, pos_ref.shape[0]))


def _prologue(positions, ws):
    t = positions.size
    steps = PROLOGUE_STEPS
    rows = t // steps
    specs = [_side_cast_specs(w, 0, steps) for w in ws]
    outs = pl.pallas_call(
        _prologue_body,
        grid=(steps,),
        in_specs=[pl.BlockSpec((rows, 1), lambda i: (i, 0)), _resident((8, LANES))]
        + [s[0] for s in specs],
        out_specs=[pl.BlockSpec((4, rows, LANES), lambda i: (0, i, 0))] + [s[1] for s in specs],
        out_shape=[jax.ShapeDtypeStruct((4, t, LANES), F32)] + [s[2] for s in specs],
        compiler_params=pltpu.CompilerParams(
            dimension_semantics=("arbitrary",), vmem_limit_bytes=VMEM_LIMIT),
        name="prologue",
    )(positions.reshape(t, 1), _table_consts(), *ws)
    return outs[0], list(outs[1:])


def _side_cast_specs(w, l, steps):
    _, r, c = w.shape
    hold = 1
    while (r * hold) % (steps * BF16_ROWS) or steps % hold:
        hold += 1
    rows = r * hold // steps
    return (pl.BlockSpec((None, rows, c), lambda i: (l, i // hold, 0)),
            pl.BlockSpec((rows, c), lambda i: (i // hold, 0)),
            jax.ShapeDtypeStruct((r, c), BF16))


def _ffn(x, gain, wg, wu, wd, proj=None, side=None):
    t = x.shape[0]
    tm = FFN_TILE
    steps = t // tm
    row = lambda w: pl.BlockSpec((tm, w), lambda i: (i, 0))
    ins, specs = [x], [row(D_MODEL)]
    if proj is not None:
        yr, ym, wo = proj
        ins += [yr, ym, wo]
        specs += [row(RET_W), row(MLA_HEADS * MLA_V), _resident(wo.shape)]
    ins += [gain, wg, wu, wd]
    specs += [_resident(w.shape) for w in (gain, wg, wu, wd)]
    out_specs = [row(D_MODEL)]
    out_shape = [jax.ShapeDtypeStruct((t, D_MODEL), F32)]
    side_ws, side_layer = side if side is not None else ([], 0)
    for w in side_ws:
        in_spec, out_spec, shape = _side_cast_specs(w, side_layer, steps)
        ins.append(w)
        specs.append(in_spec)
        out_specs.append(out_spec)
        out_shape.append(shape)
    outs = pl.pallas_call(
        functools.partial(_ffn_body, with_proj=proj is not None, n_side=len(side_ws)),
        grid=(steps,),
        in_specs=specs,
        out_specs=out_specs,
        out_shape=out_shape,
        scratch_shapes=[pltpu.VMEM((tm, D_FF), BF16)],
        compiler_params=pltpu.CompilerParams(
            dimension_semantics=("arbitrary",), vmem_limit_bytes=FFN_VMEM_LIMIT),
        name="ffn_proj" if proj is not None else "ffn",
    )(*ins)
    return outs[0], list(outs[1:])


def _group_mean(x, gmat):
    x2 = x * x
    hi = x2.astype(BF16)
    lo = (x2 - hi.astype(F32)).astype(BF16)
    return (jnp.dot(hi, gmat, preferred_element_type=F32)
            + jnp.dot(lo, gmat, preferred_element_type=F32))


def _inproj_body(x_ref, gmix_ref, win_ref, krc_ref, tab_ref, gq_ref, wuq_ref, gkv_ref, wk_ref,
                 wv_ref, gmat_ref, vec_ref,
                 rq_ref, rk_ref, rv_ref, rg_ref, qc_ref, kc_ref, v_ref, wlat_ref):
    lat0 = 4 * RET_W

    @pl.when(pl.program_id(0) == 0)
    def _():
        wlat_ref[:, 0:IN_BASE - lat0] = win_ref[:, lat0:IN_BASE]
        wlat_ref[:, IN_BASE - lat0:] = krc_ref[...]

    h = _rms(x_ref[...], gmix_ref[...]).astype(BF16)
    cos_r, sin_r, cos_m, sin_m = tab_ref[0], tab_ref[1], tab_ref[2], tab_ref[3]
    lat = jnp.dot(h, wlat_ref[...], preferred_element_type=F32)
    c_q = lat[:, 0:Q_LORA]
    c_kv = lat[:, Q_LORA:Q_LORA + KV_LORA]
    kr = lat[:, Q_LORA + KV_LORA:]

    cqn = _rms(c_q, gq_ref[...]).astype(BF16)
    ckvn = _rms(c_kv, gkv_ref[...]).astype(BF16)
    qraw = jnp.dot(cqn, wuq_ref[...], preferred_element_type=F32)
    kraw = jnp.dot(ckvn, wk_ref[...], preferred_element_type=F32)
    v_ref[...] = (jnp.dot(ckvn, wv_ref[...], preferred_element_type=F32)
                  + vec_ref[3:4, :]).astype(BF16)

    lane = lax.broadcasted_iota(jnp.int32, (1, LANES), 1)
    in_rope = (lane >= MLA_NOPE) & (lane < MLA_NOPE + MLA_ROPE)
    ss = jnp.sum(jnp.where(in_rope, kr * kr, 0.0), axis=-1, keepdims=True) * (1.0 / MLA_ROPE)
    krn = kr * lax.rsqrt(ss + EPS) * vec_ref[2:3, 0:LANES]
    kro = krn * cos_m + pltpu.roll(krn, SLAB_SHIFT, 1) * sin_m

    gmat = gmat_ref[...]
    for pr in range(MLA_HEADS // 2):
        cols = slice(pr * 2 * LANES, (pr + 1) * 2 * LANES)
        xs = qraw[:, cols]
        y = xs * lax.rsqrt(_group_mean(xs, gmat) + EPS) * vec_ref[0:1, 0:2 * LANES]
        for e in range(2):
            sl = slice(e * LANES, (e + 1) * LANES)
            dst = slice((2 * pr + e) * LANES, (2 * pr + e + 1) * LANES)
            ys = y[:, sl]
            qc_ref[:, dst] = (ys * cos_m + pltpu.roll(ys, SLAB_SHIFT, 1) * sin_m).astype(BF16)
            ks = kraw[:, dst]
            ms = jnp.sum(ks * ks, axis=-1, keepdims=True) * (1.0 / MLA_NOPE)
            kc_ref[:, dst] = (ks * lax.rsqrt(ms + EPS) * vec_ref[1:2, 0:LANES] + kro).astype(BF16)

    def ret_cols(i):
        return jnp.dot(h, win_ref[:, i * RET_W:(i + 1) * RET_W], preferred_element_type=F32)

    ret_scale = RET_DIM ** -0.5
    for dst_ref, scale, i in ((rq_ref, None, 0), (rk_ref, ret_scale, 1)):
        x = ret_cols(i)
        for hd in range(RET_HEADS):
            sl = slice(hd * RET_DIM, (hd + 1) * RET_DIM)
            roped = x[:, sl] * cos_r + pltpu.roll(x[:, sl], RET_DIM // 2, 1) * sin_r
            dst_ref[:, sl] = (roped if scale is None else roped * scale).astype(BF16)
    rv_ref[...] = ret_cols(2).astype(BF16)
    rg_ref[...] = ret_cols(3)


def _inproj(x, tabs, gmix, win, krc, gq, wuq, gkv, wk, wv, gmat, vecs):
    t = x.shape[0]
    tm = TOK_TILE
    row = lambda w: pl.BlockSpec((tm, w), lambda i: (i, 0))
    slab_w = MLA_HEADS * LANES
    outs = [(RET_W, BF16), (RET_W, BF16), (RET_W, BF16), (RET_W, F32),
            (slab_w, BF16), (slab_w, BF16), (slab_w, BF16)]
    return pl.pallas_call(
        _inproj_body,
        grid=(t // tm,),
        in_specs=[row(D_MODEL), _resident(gmix.shape),
                  _resident(win.shape), _resident(krc.shape),
                  pl.BlockSpec((4, tm, LANES), lambda i: (0, i, 0)),
                  _resident(gq.shape), _resident(wuq.shape), _resident(gkv.shape),
                  _resident(wk.shape), _resident(wv.shape), _resident(gmat.shape),
                  _resident(vecs.shape)],
        out_specs=[row(w) for w, _ in outs],
        out_shape=[jax.ShapeDtypeStruct((t, w), dt) for w, dt in outs],
        scratch_shapes=[pltpu.VMEM((D_MODEL, IN_W - 4 * RET_W), BF16)],
        compiler_params=pltpu.CompilerParams(
            dimension_semantics=("arbitrary",), vmem_limit_bytes=VMEM_LIMIT),
        name="inproj",
    )(x, gmix, win, krc, tabs, gq, wuq, gkv, wk, wv, gmat, vecs)


def _retention_body(q_hbm, k_hbm, v_hbm, g_hbm, dec_ref, gn_ref, o_ref,
                    state_ref, s_ref, kv_ref, prev_ref, q_buf, k_buf, v_buf, g_buf, sem,
                    *, tiles_per_seq, n_steps):
    step = pl.program_id(0) * tiles_per_seq + pl.program_id(1)

    def tile_copies(t):
        slot = lax.rem(t, RET_SLOTS)
        rows = pl.ds(pl.multiple_of(lax.rem(t, tiles_per_seq) * RET_TILE, RET_TILE), RET_TILE)
        b = lax.div(t, tiles_per_seq)
        return [pltpu.make_async_copy(src.at[b, rows, :], buf.at[slot], sem.at[i, slot])
                for i, (src, buf) in enumerate(((q_hbm, q_buf), (k_hbm, k_buf),
                                                (v_hbm, v_buf), (g_hbm, g_buf)))]

    @pl.when(step == 0)
    def _():
        for t in range(min(RET_SLOTS - 1, n_steps)):
            for cp in tile_copies(jnp.int32(t)):
                cp.start()

    @pl.when(step + (RET_SLOTS - 1) < n_steps)
    def _():
        for cp in tile_copies(step + (RET_SLOTS - 1)):
            cp.start()

    for cp in tile_copies(step):
        cp.wait()

    @pl.when(pl.program_id(1) == 0)
    def _():
        state_ref[...] = jnp.zeros_like(state_ref)

    slot = lax.rem(step, RET_SLOTS)
    q_ref, k_ref, v_ref, g_ref = q_buf.at[slot], k_buf.at[slot], v_buf.at[slot], g_buf.at[slot]
    c = RET_CHUNK
    nt = (((1,), (1,)), ((), ()))
    n = RET_TILE // c
    blocks = [(hd, ci) for ci in range(n) for hd in range(RET_HEADS)]
    tile = lambda hd, ci: (slice(ci * c, (ci + 1) * c), slice(hd * RET_DIM, (hd + 1) * RET_DIM))

    for hd, ci in blocks:
        rows, cols = tile(hd, ci)
        k = k_ref[rows, cols]
        v = v_ref[rows, cols]
        s = lax.dot_general(q_ref[rows, cols], k, nt, preferred_element_type=F32)
        s_ref[hd * n + ci] = (s * dec_ref[0, hd]).astype(BF16)
        kz = (k.astype(F32) * dec_ref[1, hd]).astype(BF16)
        kv_ref[hd * n + ci] = jnp.dot(kz.T, v, preferred_element_type=F32)
    for hd in range(RET_HEADS):
        st = state_ref[hd]
        for ci in range(n):
            prev_ref[hd * n + ci] = st.astype(BF16)
            st = st * dec_ref[3, hd] + kv_ref[hd * n + ci]
        state_ref[hd] = st
    for hd, ci in blocks:
        rows, cols = tile(hd, ci)
        q = q_ref[rows, cols]
        y = (jnp.dot(s_ref[hd * n + ci], v_ref[rows, cols], preferred_element_type=F32)
             + jnp.dot(q, prev_ref[hd * n + ci], preferred_element_type=F32) * dec_ref[2, hd])
        g = g_ref[rows, cols]
        o_ref[0, rows, cols] = (_rms(y, gn_ref[hd]) * (g * jax.nn.sigmoid(g))).astype(BF16)


def _retention(rq, rk, rv, rg, dec, gn, batch, seq):
    shp = (batch, seq, RET_W)
    nblk = RET_HEADS * (RET_TILE // RET_CHUNK)
    blk = pl.BlockSpec((1, RET_TILE, RET_W), lambda b, s: (b, s, 0))
    hbm = pl.BlockSpec(memory_space=pl.ANY)
    tiles_per_seq = seq // RET_TILE
    ring = lambda dt: pltpu.VMEM((RET_SLOTS, RET_TILE, RET_W), dt)
    return pl.pallas_call(
        functools.partial(_retention_body, tiles_per_seq=tiles_per_seq,
                          n_steps=batch * tiles_per_seq),
        grid=(batch, tiles_per_seq),
        in_specs=[hbm, hbm, hbm, hbm, _resident(dec.shape), _resident(gn.shape)],
        out_specs=blk,
        out_shape=jax.ShapeDtypeStruct(shp, BF16),
        scratch_shapes=[pltpu.VMEM((RET_HEADS, RET_DIM, RET_DIM), F32),
                        pltpu.VMEM((nblk, RET_CHUNK, RET_CHUNK), BF16),
                        pltpu.VMEM((nblk, RET_DIM, RET_DIM), F32),
                        pltpu.VMEM((nblk, RET_DIM, RET_DIM), BF16),
                        ring(BF16), ring(BF16), ring(BF16), ring(F32),
                        pltpu.SemaphoreType.DMA((4, RET_SLOTS))],
        compiler_params=pltpu.CompilerParams(
            dimension_semantics=("arbitrary", "arbitrary"), vmem_limit_bytes=VMEM_LIMIT),
        name="retention",
    )(rq.reshape(shp), rk.reshape(shp), rv.reshape(shp), rg.reshape(shp), dec, gn)


def _attention_body(q_ref, k_ref, v_ref, o_ref, m_ref, acc_ref):
    qi = pl.program_id(2)
    nt = (((1,), (1,)), ((), ()))
    half = ATT_TQ // 2
    m_ref[...] = jnp.full_like(m_ref, MASK_VALUE)
    acc_ref[...] = jnp.zeros_like(acc_ref)

    def steps(specs, skew):
        cols = lambda e: slice(e * LANES, (e + 1) * LANES)
        keys = lambda start, width: pl.ds(pl.multiple_of(start, half), width)

        def score(spec):
            e, q_rows, start, width, _ = spec
            return lax.dot_general(q_ref[0, q_rows, cols(e)], k_ref[0, keys(start, width), cols(e)],
                                   nt, preferred_element_type=F32)

        def softmax(spec, s):
            e, q_rows, _, width, mask_from = spec
            chunks = [s[:, c * LANES:(c + 1) * LANES] for c in range(width // LANES)]
            if mask_from is not None:
                r = lax.broadcasted_iota(jnp.int32, chunks[0].shape, 0)
                lane = lax.broadcasted_iota(jnp.int32, chunks[0].shape, 1)
                for c in range(mask_from, len(chunks)):
                    chunks[c] = jnp.where(r >= lane + (c - mask_from) * LANES, chunks[c],
                                          MASK_VALUE)
            m_old = m_ref[e, q_rows]
            m_new = jnp.maximum(
                m_old, jnp.max(functools.reduce(jnp.maximum, chunks), axis=-1, keepdims=True))
            m_ref[e, q_rows] = m_new
            return jnp.exp2(m_old - m_new), jnp.concatenate(
                [jnp.exp2(ch - m_new).astype(BF16) for ch in chunks], axis=1)

        def value(spec, alpha, p):
            e, q_rows, start, width, _ = spec
            acc_ref[e, q_rows] = alpha * acc_ref[e, q_rows] + jnp.dot(
                p, v_ref[0, keys(start, width), cols(e)], preferred_element_type=F32)

        n = len(specs)
        scores, probs = {}, {}
        for i in range(n + 2 * skew):
            if i < n:
                scores[i] = score(specs[i])
            if 0 <= i - skew < n:
                probs[i - skew] = softmax(specs[i - skew], scores.pop(i - skew))
            if 0 <= i - 2 * skew < n:
                value(specs[i - 2 * skew], *probs.pop(i - 2 * skew))

    def loop_body(kb, carry):
        for e in range(ATT_HEADS):
            steps([(e, slice(0, ATT_TQ), kb * ATT_TQ, ATT_TQ, None)], 1)
        return carry

    lax.fori_loop(0, qi, loop_body, 0)
    diag = qi * ATT_TQ
    steps([spec for e in range(ATT_HEADS) for spec in (
        (e, slice(0, half), diag, half, 0),
        (e, slice(half, ATT_TQ), diag, ATT_TQ, half // LANES))], DIAG_SKEW)
    lane = lax.broadcasted_iota(jnp.int32, (ATT_TQ, LANES), 1)
    for pr in range(ATT_HEADS // 2):
        o0, o1 = [acc * (1.0 / acc[:, MLA_V:MLA_V + 1])
                  for acc in (acc_ref[2 * pr], acc_ref[2 * pr + 1])]
        o_ref[0, :, pr * LANES:(pr + 1) * LANES] = jnp.where(
            lane < MLA_V, o0, pltpu.roll(o1, MLA_V, 1)).astype(BF16)


def _attention(qc, kc, vc, batch, seq):
    slab_w = MLA_HEADS * LANES
    grp_w = ATT_HEADS * LANES
    out_w = ATT_HEADS * MLA_V
    seq_blk = pl.BlockSpec((1, seq, grp_w), lambda b, j, i: (b, 0, j))
    return pl.pallas_call(
        _attention_body,
        grid=(batch, MLA_HEADS // ATT_HEADS, seq // ATT_TQ),
        in_specs=[pl.BlockSpec((1, ATT_TQ, grp_w), lambda b, j, i: (b, i, j)),
                  seq_blk, seq_blk],
        out_specs=pl.BlockSpec((1, ATT_TQ, out_w), lambda b, j, i: (b, i, j)),
        out_shape=jax.ShapeDtypeStruct((batch, seq, MLA_HEADS * MLA_V), BF16),
        scratch_shapes=[pltpu.VMEM((ATT_HEADS, ATT_TQ, LANES), F32),
                        pltpu.VMEM((ATT_HEADS, ATT_TQ, LANES), F32)],
        compiler_params=pltpu.CompilerParams(
            dimension_semantics=("parallel", "parallel", "arbitrary"),
            vmem_limit_bytes=VMEM_LIMIT),
        name="latent_attention",
    )(qc.reshape(batch, seq, slab_w), kc.reshape(batch, seq, slab_w),
      vc.reshape(batch, seq, slab_w))


def _retention_decay():
    c = RET_CHUNK
    lg = jnp.log(1.0 - 2.0 ** (-5.0 - jnp.arange(RET_HEADS, dtype=F32)))
    idx = jnp.arange(c, dtype=F32)
    rel = idx[:, None] - idx[None, :]
    decay = jnp.where(rel >= 0, jnp.exp(jnp.maximum(rel, 0.0)[None] * lg[:, None, None]), 0.0)
    zeta = jnp.exp((c - 1 - idx)[None, :] * lg[:, None])
    xi = jnp.exp((idx + 1.0)[None, :] * lg[:, None])
    cdec = jnp.exp(c * lg)
    full = (RET_HEADS, c, RET_DIM)
    return jnp.stack([decay,
                      jnp.broadcast_to(zeta[:, :, None], full),
                      jnp.broadcast_to(xi[:, :, None], full),
                      jnp.broadcast_to(cdec[:, None, None], full)])


def _swapped_rope_cols(w):
    a, b = w[..., :HALF_ROPE], w[..., HALF_ROPE:]
    return jnp.concatenate([a, b, b, a], axis=-1)


def _layer_params(l, p, win):
    row = lambda g: g[l].reshape(1, -1)
    kr_cols = jnp.concatenate(
        [jnp.zeros((D_MODEL, MLA_NOPE), BF16), _swapped_rope_cols(win[:, IN_BASE:])], axis=-1)

    dq = MLA_NOPE + MLA_ROPE
    wuq = p["w_uq"][l].reshape(Q_LORA, MLA_HEADS, dq)
    wuq = jnp.concatenate([wuq[..., :MLA_NOPE], _swapped_rope_cols(wuq[..., MLA_NOPE:])], axis=-1)
    wuq = wuq.reshape(Q_LORA, MLA_HEADS * LANES).astype(BF16)
    wukv = p["w_ukv"][l].reshape(KV_LORA, MLA_HEADS, MLA_NOPE + MLA_V)
    wk = jnp.concatenate([wukv[..., :MLA_NOPE], jnp.zeros_like(wukv[..., :MLA_NOPE])], axis=-1)
    wk = wk.reshape(KV_LORA, MLA_HEADS * LANES).astype(BF16)
    wv = jnp.concatenate([wukv[..., MLA_NOPE:], jnp.zeros_like(wukv[..., :LANES - MLA_V])], axis=-1)
    wv = wv.reshape(KV_LORA, MLA_HEADS * LANES).astype(BF16)

    scale = dq ** -0.5 * LOG2_E
    zeros64 = jnp.zeros((MLA_NOPE,), F32)
    gq = jnp.concatenate([p["qn_nope"][l], _swapped_rope_cols(p["qn_rope"][l])]) * scale
    gk = jnp.concatenate([p["kn_nope"][l], zeros64])
    gkr = jnp.concatenate([zeros64, _swapped_rope_cols(p["kn_rope"][l])])
    one_lane = jnp.zeros((LANES,), F32).at[MLA_V].set(1.0)
    vecs = jnp.stack([jnp.tile(r, MLA_HEADS) for r in (gq, gk, gkr, one_lane)]
                     + [jnp.zeros((MLA_HEADS * LANES,), F32)] * 4)
    return dict(
        ffn1_gain=row(p["ffn1_norm"]),
        ffn2_gain=row(p["ffn2_norm"]),
        inproj=(row(p["mix_norm"]), kr_cols, row(p["q_lat_norm"]), wuq, row(p["kv_lat_norm"]),
                wk, wv),
        vecs=vecs,
        gn=p["ret_head_norm"][l].reshape(RET_HEADS, 1, RET_DIM),
    )


def _group_matrix():
    idx = jnp.arange(2 * LANES)
    slab, lane = idx // LANES, idx % LANES
    same = slab[:, None] == slab[None, :]
    nope = (lane[:, None] < MLA_NOPE) & (lane[None, :] < MLA_NOPE)
    rope = ((lane[:, None] >= MLA_NOPE) & (lane[:, None] < MLA_NOPE + MLA_ROPE)
            & (lane[None, :] >= MLA_NOPE))
    g = jnp.where(nope, 1.0 / MLA_NOPE, jnp.where(rope, 1.0 / MLA_ROPE, 0.0))
    return jnp.where(same, g, 0.0).astype(BF16)


def kernel(x, positions, ffn1_norm, ffn1_w_gate, ffn1_w_up, ffn1_w_down, mix_norm, w_in, ret_head_norm, q_lat_norm, w_uq, kv_lat_norm, w_ukv, qn_nope, qn_rope, kn_nope, kn_rope, w_o, ffn2_norm, ffn2_w_gate, ffn2_w_up, ffn2_w_down):
    p = dict(ffn1_norm=ffn1_norm, ffn1_w_gate=ffn1_w_gate, ffn1_w_up=ffn1_w_up,
             ffn1_w_down=ffn1_w_down, mix_norm=mix_norm, w_in=w_in,
             ret_head_norm=ret_head_norm, q_lat_norm=q_lat_norm, w_uq=w_uq,
             kv_lat_norm=kv_lat_norm, w_ukv=w_ukv, qn_nope=qn_nope, qn_rope=qn_rope,
             kn_nope=kn_nope, kn_rope=kn_rope, w_o=w_o, ffn2_norm=ffn2_norm,
             ffn2_w_gate=ffn2_w_gate, ffn2_w_up=ffn2_w_up, ffn2_w_down=ffn2_w_down)
    batch, seq, d = x.shape
    depth = w_in.shape[0]
    assert d == D_MODEL and positions.shape == (batch, seq)
    assert seq % ATT_TQ == 0 and seq % RET_TILE == 0
    assert (batch * seq) % TOK_TILE == 0 and (batch * seq) % FFN_TILE == 0
    assert w_in.shape[1:] == (D_MODEL, IN_BASE + MLA_ROPE) and ffn1_w_gate.shape[1:] == (d, D_FF)
    dec = _retention_decay()
    gmat = _group_matrix()
    ffn1_stacks = [ffn1_w_gate, ffn1_w_up, ffn1_w_down]
    ffn2_stacks = [ffn2_w_gate, ffn2_w_up, ffn2_w_down]
    tabs, w_ffn1 = _prologue(positions, ffn1_stacks)
    xf = x.reshape(batch * seq, d)
    for l in range(depth):
        win = w_in[l].astype(BF16)
        lp = _layer_params(l, p, win)
        xf, side_w = _ffn(xf, lp["ffn1_gain"], *w_ffn1, side=(ffn2_stacks + [w_o], l))
        w_ffn2, wo = side_w[:3], side_w[3]
        gmix, kr_cols, gq, wuq, gkv, wk, wv = lp["inproj"]
        rq, rk, rv, rg, qc, kc, v = _inproj(xf, tabs, gmix, win, kr_cols, gq, wuq, gkv,
                                            wk, wv, gmat, lp["vecs"])
        y_ret = _retention(rq, rk, rv, rg, dec, lp["gn"], batch, seq)
        y_mla = _attention(qc, kc, v, batch, seq)
        xf, w_ffn1 = _ffn(
            xf, lp["ffn2_gain"], *w_ffn2,
            proj=(y_ret.reshape(batch * seq, -1), y_mla.reshape(batch * seq, -1), wo),
            side=(ffn1_stacks, l + 1) if l + 1 < depth else None)
    return xf.reshape(batch, seq, d)
```

```python
import functools

import jax
import jax.numpy as jnp
from jax import lax
from jax.experimental import pallas as pl
from jax.experimental.pallas import tpu as pltpu

D_MODEL = 1024
D_FF = 2816
RET_HEADS = 4
RET_DIM = 128
RET_CHUNK = 128
MLA_HEADS = 8
MLA_NOPE = 64
MLA_ROPE = 32
MLA_V = 64
Q_LORA = 256
KV_LORA = 128
ROPE_THETA = 10000.0
EPS = 1e-6

LANES = 128
BF16_ROWS = 16
RET_W = RET_HEADS * RET_DIM
IN_BASE = 4 * RET_W + Q_LORA + KV_LORA
IN_W = IN_BASE + LANES
HALF_ROPE = MLA_ROPE // 2
SLAB_SHIFT = LANES - MLA_ROPE

TOK_TILE = 512
FF_TILE = 256
PROLOGUE_STEPS = 16
ATT_TQ = 1024
ATT_HEADS = 4
DIAG_SKEW = 2
RET_TILE = 1024
RET_SLOTS = 3
V7X_VMEM_BYTES = 64 * 1024 * 1024
VMEM_LIMIT = V7X_VMEM_BYTES * 3 // 4
FFN_TILE = 1024
FFN_VMEM_LIMIT = V7X_VMEM_BYTES * 29 // 32
MASK_VALUE = -1e30
LOG2_E = 1.4426950408889634

F32 = jnp.float32
BF16 = jnp.bfloat16


def _rms(x, gain):
    return x * lax.rsqrt(jnp.mean(x * x, axis=-1, keepdims=True) + EPS) * gain


def _resident(shape):
    nd = len(shape)
    return pl.BlockSpec(shape, lambda *_: (0,) * nd)


def _tables_rows(pos_ref, c_ref, o_ref, rows):
    ang = pos_ref[rows, :].astype(F32) * c_ref[0:1, :]
    c, s = jnp.cos(ang), jnp.sin(ang)
    low = lax.broadcasted_iota(jnp.int32, c.shape, 1) < RET_DIM // 2
    o_ref[0, rows] = jnp.where(low, c, pltpu.roll(c, RET_DIM // 2, 1))
    o_ref[1, rows] = jnp.where(low, s, pltpu.roll(s, RET_DIM // 2, 1)) * c_ref[1:2, :]
    o_ref[2, rows] = c_ref[2:3, :] + c_ref[3:4, :] * c
    o_ref[3, rows] = s * c_ref[4:5, :]


def _table_consts():
    inv_r = 1.0 / (ROPE_THETA ** (jnp.arange(0, RET_DIM, 2, dtype=F32) / RET_DIM))
    inv_m = 1.0 / (ROPE_THETA ** (jnp.arange(0, MLA_ROPE, 2, dtype=F32) / MLA_ROPE))
    half = RET_DIM // 2
    z = lambda n: jnp.zeros((n,), F32)
    o = lambda n: jnp.ones((n,), F32)
    return jnp.stack([
        jnp.concatenate([inv_r, inv_m, inv_m, z(MLA_ROPE)]),
        jnp.concatenate([-o(half), o(half)]),
        jnp.concatenate([o(MLA_NOPE), z(LANES - MLA_NOPE)]),
        jnp.concatenate([z(MLA_NOPE), o(MLA_ROPE), z(MLA_ROPE)]),
        jnp.concatenate([z(MLA_NOPE), -o(HALF_ROPE), o(HALF_ROPE), z(MLA_ROPE)]),
        z(LANES), z(LANES), z(LANES)])


def _ffn_body(*refs, with_proj, n_side):
    refs = list(refs)
    take = lambda n: [refs.pop(0) for _ in range(n)]
    (x_ref,) = take(1)
    if with_proj:
        yr_ref, ym_ref, wo_ref = take(3)
    g_ref, wg_ref, wu_ref, wd_ref = take(4)
    side_in = take(n_side)
    (o_ref,) = take(1)
    side_out = take(n_side)
    (a_ref,) = take(1)

    x = x_ref[...]
    if with_proj:
        x = (x + jnp.dot(yr_ref[...], wo_ref[0:RET_W, :], preferred_element_type=F32)
             + jnp.dot(ym_ref[...], wo_ref[RET_W:, :], preferred_element_type=F32))
    h = _rms(x, g_ref[...]).astype(BF16)
    for c in range(D_FF // FF_TILE):
        cols = slice(c * FF_TILE, (c + 1) * FF_TILE)
        g = jnp.dot(h, wg_ref[:, cols], preferred_element_type=F32)
        u = jnp.dot(h, wu_ref[:, cols], preferred_element_type=F32)
        a_ref[:, cols] = (g * jax.nn.sigmoid(g) * u).astype(BF16)
    for src, dst in zip(side_in, side_out):
        dst[...] = src[...].astype(BF16)
    y = jnp.dot(a_ref[...], wd_ref[...], preferred_element_type=F32)
    o_ref[...] = x + 0.5 * y


def _prologue_body(*refs):
    pos_ref, c_ref = refs[:2]
    n = (len(refs) - 3) // 2
    srcs, tab_ref, dsts = refs[2:2 + n], refs[2 + n], refs[3 + n:]
    for src, dst in zip(srcs, dsts):
        dst[...] = src[...].astype(BF16)
    _tables_rows(pos_ref, c_ref, tab_ref, slice(0, pos_ref.shape[0]))


def _prologue(positions, ws):
    t = positions.size
    steps = PROLOGUE_STEPS
    rows = t // steps
    specs = [_side_cast_specs(w, 0, steps) for w in ws]
    outs = pl.pallas_call(
        _prologue_body,
        grid=(steps,),
        in_specs=[pl.BlockSpec((rows, 1), lambda i: (i, 0)), _resident((8, LANES))]
        + [s[0] for s in specs],
        out_specs=[pl.BlockSpec((4, rows, LANES), lambda i: (0, i, 0))] + [s[1] for s in specs],
        out_shape=[jax.ShapeDtypeStruct((4, t, LANES), F32)] + [s[2] for s in specs],
        compiler_params=pltpu.CompilerParams(
            dimension_semantics=("arbitrary",), vmem_limit_bytes=VMEM_LIMIT),
        name="prologue",
    )(positions.reshape(t, 1), _table_consts(), *ws)
    return outs[0], list(outs[1:])


def _side_cast_specs(w, l, steps):
    _, r, c = w.shape
    hold = 1
    while (r * hold) % (steps * BF16_ROWS) or steps % hold:
        hold += 1
    rows = r * hold // steps
    return (pl.BlockSpec((None, rows, c), lambda i: (l, i // hold, 0)),
            pl.BlockSpec((rows, c), lambda i: (i // hold, 0)),
            jax.ShapeDtypeStruct((r, c), BF16))


def _ffn(x, gain, wg, wu, wd, proj=None, side=None):
    t = x.shape[0]
    tm = FFN_TILE
    steps = t // tm
    row = lambda w: pl.BlockSpec((tm, w), lambda i: (i, 0))
    ins, specs = [x], [row(D_MODEL)]
    if proj is not None:
        yr, ym, wo = proj
        ins += [yr, ym, wo]
        specs += [row(RET_W), row(MLA_HEADS * MLA_V), _resident(wo.shape)]
    ins += [gain, wg, wu, wd]
    specs += [_resident(w.shape) for w in (gain, wg, wu, wd)]
    out_specs = [row(D_MODEL)]
    out_shape = [jax.ShapeDtypeStruct((t, D_MODEL), F32)]
    side_ws, side_layer = side if side is not None else ([], 0)
    for w in side_ws:
        in_spec, out_spec, shape = _side_cast_specs(w, side_layer, steps)
        ins.append(w)
        specs.append(in_spec)
        out_specs.append(out_spec)
        out_shape.append(shape)
    outs = pl.pallas_call(
        functools.partial(_ffn_body, with_proj=proj is not None, n_side=len(side_ws)),
        grid=(steps,),
        in_specs=specs,
        out_specs=out_specs,
        out_shape=out_shape,
        scratch_shapes=[pltpu.VMEM((tm, D_FF), BF16)],
        compiler_params=pltpu.CompilerParams(
            dimension_semantics=("arbitrary",), vmem_limit_bytes=FFN_VMEM_LIMIT),
        name="ffn_proj" if proj is not None else "ffn",
    )(*ins)
    return outs[0], list(outs[1:])


def _group_mean(x, gmat):
    x2 = x * x
    hi = x2.astype(BF16)
    lo = (x2 - hi.astype(F32)).astype(BF16)
    return (jnp.dot(hi, gmat, preferred_element_type=F32)
            + jnp.dot(lo, gmat, preferred_element_type=F32))


def _inproj_body(x_ref, gmix_ref, win_ref, krc_ref, tab_ref, gq_ref, wuq_ref, gkv_ref, wk_ref,
                 wv_ref, gmat_ref, vec_ref,
                 rq_ref, rk_ref, rv_ref, rg_ref, qc_ref, kc_ref, v_ref, wlat_ref):
    lat0 = 4 * RET_W

    @pl.when(pl.program_id(0) == 0)
    def _():
        wlat_ref[:, 0:IN_BASE - lat0] = win_ref[:, lat0:IN_BASE]
        wlat_ref[:, IN_BASE - lat0:] = krc_ref[...]

    h = _rms(x_ref[...], gmix_ref[...]).astype(BF16)
    cos_r, sin_r, cos_m, sin_m = tab_ref[0], tab_ref[1], tab_ref[2], tab_ref[3]
    lat = jnp.dot(h, wlat_ref[...], preferred_element_type=F32)
    c_q = lat[:, 0:Q_LORA]
    c_kv = lat[:, Q_LORA:Q_LORA + KV_LORA]
    kr = lat[:, Q_LORA + KV_LORA:]

    cqn = _rms(c_q, gq_ref[...]).astype(BF16)
    ckvn = _rms(c_kv, gkv_ref[...]).astype(BF16)
    qraw = jnp.dot(cqn, wuq_ref[...], preferred_element_type=F32)
    kraw = jnp.dot(ckvn, wk_ref[...], preferred_element_type=F32)
    v_ref[...] = (jnp.dot(ckvn, wv_ref[...], preferred_element_type=F32)
                  + vec_ref[3:4, :]).astype(BF16)

    lane = lax.broadcasted_iota(jnp.int32, (1, LANES), 1)
    in_rope = (lane >= MLA_NOPE) & (lane < MLA_NOPE + MLA_ROPE)
    ss = jnp.sum(jnp.where(in_rope, kr * kr, 0.0), axis=-1, keepdims=True) * (1.0 / MLA_ROPE)
    krn = kr * lax.rsqrt(ss + EPS) * vec_ref[2:3, 0:LANES]
    kro = krn * cos_m + pltpu.roll(krn, SLAB_SHIFT, 1) * sin_m

    gmat = gmat_ref[...]
    for pr in range(MLA_HEADS // 2):
        cols = slice(pr * 2 * LANES, (pr + 1) * 2 * LANES)
        xs = qraw[:, cols]
        y = xs * lax.rsqrt(_group_mean(xs, gmat) + EPS) * vec_ref[0:1, 0:2 * LANES]
        for e in range(2):
            sl = slice(e * LANES, (e + 1) * LANES)
            dst = slice((2 * pr + e) * LANES, (2 * pr + e + 1) * LANES)
            ys = y[:, sl]
            qc_ref[:, dst] = (ys * cos_m + pltpu.roll(ys, SLAB_SHIFT, 1) * sin_m).astype(BF16)
            ks = kraw[:, dst]
            ms = jnp.sum(ks * ks, axis=-1, keepdims=True) * (1.0 / MLA_NOPE)
            kc_ref[:, dst] = (ks * lax.rsqrt(ms + EPS) * vec_ref[1:2, 0:LANES] + kro).astype(BF16)

    def ret_cols(i):
        return jnp.dot(h, win_ref[:, i * RET_W:(i + 1) * RET_W], preferred_element_type=F32)

    ret_scale = RET_DIM ** -0.5
    for dst_ref, scale, i in ((rq_ref, None, 0), (rk_ref, ret_scale, 1)):
        x = ret_cols(i)
        for hd in range(RET_HEADS):
            sl = slice(hd * RET_DIM, (hd + 1) * RET_DIM)
            roped = x[:, sl] * cos_r + pltpu.roll(x[:, sl], RET_DIM // 2, 1) * sin_r
            dst_ref[:, sl] = (roped if scale is None else roped * scale).astype(BF16)
    rv_ref[...] = ret_cols(2).astype(BF16)
    rg_ref[...] = ret_cols(3)


def _inproj(x, tabs, gmix, win, krc, gq, wuq, gkv, wk, wv, gmat, vecs):
    t = x.shape[0]
    tm = TOK_TILE
    row = lambda w: pl.BlockSpec((tm, w), lambda i: (i, 0))
    slab_w = MLA_HEADS * LANES
    outs = [(RET_W, BF16), (RET_W, BF16), (RET_W, BF16), (RET_W, F32),
            (slab_w, BF16), (slab_w, BF16), (slab_w, BF16)]
    return pl.pallas_call(
        _inproj_body,
        grid=(t // tm,),
        in_specs=[row(D_MODEL), _resident(gmix.shape),
                  _resident(win.shape), _resident(krc.shape),
                  pl.BlockSpec((4, tm, LANES), lambda i: (0, i, 0)),
                  _resident(gq.shape), _resident(wuq.shape), _resident(gkv.shape),
                  _resident(wk.shape), _resident(wv.shape), _resident(gmat.shape),
                  _resident(vecs.shape)],
        out_specs=[row(w) for w, _ in outs],
        out_shape=[jax.ShapeDtypeStruct((t, w), dt) for w, dt in outs],
        scratch_shapes=[pltpu.VMEM((D_MODEL, IN_W - 4 * RET_W), BF16)],
        compiler_params=pltpu.CompilerParams(
            dimension_semantics=("arbitrary",), vmem_limit_bytes=VMEM_LIMIT),
        name="inproj",
    )(x, gmix, win, krc, tabs, gq, wuq, gkv, wk, wv, gmat, vecs)


def _retention_body(q_hbm, k_hbm, v_hbm, g_hbm, dec_ref, gn_ref, o_ref,
                    state_ref, s_ref, kv_ref, prev_ref, q_buf, k_buf, v_buf, g_buf, sem,
                    *, tiles_per_seq, n_steps):
    step = pl.program_id(0) * tiles_per_seq + pl.program_id(1)

    def tile_copies(t):
        slot = lax.rem(t, RET_SLOTS)
        rows = pl.ds(pl.multiple_of(lax.rem(t, tiles_per_seq) * RET_TILE, RET_TILE), RET_TILE)
        b = lax.div(t, tiles_per_seq)
        return [pltpu.make_async_copy(src.at[b, rows, :], buf.at[slot], sem.at[i, slot])
                for i, (src, buf) in enumerate(((q_hbm, q_buf), (k_hbm, k_buf),
                                                (v_hbm, v_buf), (g_hbm, g_buf)))]

    @pl.when(step == 0)
    def _():
        for t in range(min(RET_SLOTS - 1, n_steps)):
            for i, cp in enumerate(tile_copies(jnp.int32(t))):
                cp.start(priority=i % 2)

    @pl.when(step + (RET_SLOTS - 1) < n_steps)
    def _():
        for i, cp in enumerate(tile_copies(step + (RET_SLOTS - 1))):
            cp.start(priority=i % 2)

    for cp in tile_copies(step):
        cp.wait()

    @pl.when(pl.program_id(1) == 0)
    def _():
        state_ref[...] = jnp.zeros_like(state_ref)

    slot = lax.rem(step, RET_SLOTS)
    q_ref, k_ref, v_ref, g_ref = q_buf.at[slot], k_buf.at[slot], v_buf.at[slot], g_buf.at[slot]
    c = RET_CHUNK
    nt = (((1,), (1,)), ((), ()))
    n = RET_TILE // c
    blocks = [(hd, ci) for ci in range(n) for hd in range(RET_HEADS)]
    tile = lambda hd, ci: (slice(ci * c, (ci + 1) * c), slice(hd * RET_DIM, (hd + 1) * RET_DIM))

    for hd, ci in blocks:
        rows, cols = tile(hd, ci)
        k = k_ref[rows, cols]
        v = v_ref[rows, cols]
        s = lax.dot_general(q_ref[rows, cols], k, nt, preferred_element_type=F32)
        s_ref[hd * n + ci] = (s * dec_ref[0, hd]).astype(BF16)
        kz = (k.astype(F32) * dec_ref[1, hd]).astype(BF16)
        kv_ref[hd * n + ci] = jnp.dot(kz.T, v, preferred_element_type=F32)
    for hd in range(RET_HEADS):
        st = state_ref[hd]
        for ci in range(n):
            prev_ref[hd * n + ci] = st.astype(BF16)
            st = st * dec_ref[3, hd] + kv_ref[hd * n + ci]
        state_ref[hd] = st
    for hd, ci in blocks:
        rows, cols = tile(hd, ci)
        q = q_ref[rows, cols]
        y = (jnp.dot(s_ref[hd * n + ci], v_ref[rows, cols], preferred_element_type=F32)
             + jnp.dot(q, prev_ref[hd * n + ci], preferred_element_type=F32) * dec_ref[2, hd])
        g = g_ref[rows, cols]
        o_ref[0, rows, cols] = (_rms(y, gn_ref[hd]) * (g * jax.nn.sigmoid(g))).astype(BF16)


def _retention(rq, rk, rv, rg, dec, gn, batch, seq):
    shp = (batch, seq, RET_W)
    nblk = RET_HEADS * (RET_TILE // RET_CHUNK)
    blk = pl.BlockSpec((1, RET_TILE, RET_W), lambda b, s: (b, s, 0))
    hbm = pl.BlockSpec(memory_space=pl.ANY)
    tiles_per_seq = seq // RET_TILE
    ring = lambda dt: pltpu.VMEM((RET_SLOTS, RET_TILE, RET_W), dt)
    return pl.pallas_call(
        functools.partial(_retention_body, tiles_per_seq=tiles_per_seq,
                          n_steps=batch * tiles_per_seq),
        grid=(batch, tiles_per_seq),
        in_specs=[hbm, hbm, hbm, hbm, _resident(dec.shape), _resident(gn.shape)],
        out_specs=blk,
        out_shape=jax.ShapeDtypeStruct(shp, BF16),
        scratch_shapes=[pltpu.VMEM((RET_HEADS, RET_DIM, RET_DIM), F32),
                        pltpu.VMEM((nblk, RET_CHUNK, RET_CHUNK), BF16),
                        pltpu.VMEM((nblk, RET_DIM, RET_DIM), F32),
                        pltpu.VMEM((nblk, RET_DIM, RET_DIM), BF16),
                        ring(BF16), ring(BF16), ring(BF16), ring(F32),
                        pltpu.SemaphoreType.DMA((4, RET_SLOTS))],
        compiler_params=pltpu.CompilerParams(
            dimension_semantics=("arbitrary", "arbitrary"), vmem_limit_bytes=VMEM_LIMIT),
        name="retention",
    )(rq.reshape(shp), rk.reshape(shp), rv.reshape(shp), rg.reshape(shp), dec, gn)


def _attention_body(q_ref, k_ref, v_ref, o_ref, m_ref, acc_ref):
    qi = pl.program_id(2)
    nt = (((1,), (1,)), ((), ()))
    half = ATT_TQ // 2
    m_ref[...] = jnp.full_like(m_ref, MASK_VALUE)
    acc_ref[...] = jnp.zeros_like(acc_ref)

    def steps(specs, skew):
        cols = lambda e: slice(e * LANES, (e + 1) * LANES)
        keys = lambda start, width: pl.ds(pl.multiple_of(start, half), width)

        def score(spec):
            e, q_rows, start, width, _ = spec
            return lax.dot_general(q_ref[0, q_rows, cols(e)], k_ref[0, keys(start, width), cols(e)],
                                   nt, preferred_element_type=F32)

        def softmax(spec, s):
            e, q_rows, _, width, mask_from = spec
            chunks = [s[:, c * LANES:(c + 1) * LANES] for c in range(width // LANES)]
            if mask_from is not None:
                r = lax.broadcasted_iota(jnp.int32, chunks[0].shape, 0)
                lane = lax.broadcasted_iota(jnp.int32, chunks[0].shape, 1)
                for c in range(mask_from, len(chunks)):
                    chunks[c] = jnp.where(r >= lane + (c - mask_from) * LANES, chunks[c],
                                          MASK_VALUE)
            m_old = m_ref[e, q_rows]
            m_new = jnp.maximum(
                m_old, jnp.max(functools.reduce(jnp.maximum, chunks), axis=-1, keepdims=True))
            m_ref[e, q_rows] = m_new
            return jnp.exp2(m_old - m_new), jnp.concatenate(
                [jnp.exp2(ch - m_new).astype(BF16) for ch in chunks], axis=1)

        def value(spec, alpha, p):
            e, q_rows, start, width, _ = spec
            acc_ref[e, q_rows] = alpha * acc_ref[e, q_rows] + jnp.dot(
                p, v_ref[0, keys(start, width), cols(e)], preferred_element_type=F32)

        n = len(specs)
        scores, probs = {}, {}
        for i in range(n + 2 * skew):
            if i < n:
                scores[i] = score(specs[i])
            if 0 <= i - skew < n:
                probs[i - skew] = softmax(specs[i - skew], scores.pop(i - skew))
            if 0 <= i - 2 * skew < n:
                value(specs[i - 2 * skew], *probs.pop(i - 2 * skew))

    def loop_body(kb, carry):
        for e in range(ATT_HEADS):
            steps([(e, slice(0, ATT_TQ), kb * ATT_TQ, ATT_TQ, None)], 1)
        return carry

    lax.fori_loop(0, qi, loop_body, 0)
    diag = qi * ATT_TQ
    steps([spec for e in range(ATT_HEADS) for spec in (
        (e, slice(0, half), diag, half, 0),
        (e, slice(half, ATT_TQ), diag, ATT_TQ, half // LANES))], DIAG_SKEW)
    lane = lax.broadcasted_iota(jnp.int32, (ATT_TQ, LANES), 1)
    for pr in range(ATT_HEADS // 2):
        o0, o1 = [acc * (1.0 / acc[:, MLA_V:MLA_V + 1])
                  for acc in (acc_ref[2 * pr], acc_ref[2 * pr + 1])]
        o_ref[0, :, pr * LANES:(pr + 1) * LANES] = jnp.where(
            lane < MLA_V, o0, pltpu.roll(o1, MLA_V, 1)).astype(BF16)


def _attention(qc, kc, vc, batch, seq):
    slab_w = MLA_HEADS * LANES
    grp_w = ATT_HEADS * LANES
    out_w = ATT_HEADS * MLA_V
    seq_blk = pl.BlockSpec((1, seq, grp_w), lambda b, j, i: (b, 0, j))
    return pl.pallas_call(
        _attention_body,
        grid=(batch, MLA_HEADS // ATT_HEADS, seq // ATT_TQ),
        in_specs=[pl.BlockSpec((1, ATT_TQ, grp_w), lambda b, j, i: (b, i, j)),
                  seq_blk, seq_blk],
        out_specs=pl.BlockSpec((1, ATT_TQ, out_w), lambda b, j, i: (b, i, j)),
        out_shape=jax.ShapeDtypeStruct((batch, seq, MLA_HEADS * MLA_V), BF16),
        scratch_shapes=[pltpu.VMEM((ATT_HEADS, ATT_TQ, LANES), F32),
                        pltpu.VMEM((ATT_HEADS, ATT_TQ, LANES), F32)],
        compiler_params=pltpu.CompilerParams(
            dimension_semantics=("parallel", "parallel", "arbitrary"),
            vmem_limit_bytes=VMEM_LIMIT),
        name="latent_attention",
    )(qc.reshape(batch, seq, slab_w), kc.reshape(batch, seq, slab_w),
      vc.reshape(batch, seq, slab_w))


def _retention_decay():
    c = RET_CHUNK
    lg = jnp.log(1.0 - 2.0 ** (-5.0 - jnp.arange(RET_HEADS, dtype=F32)))
    idx = jnp.arange(c, dtype=F32)
    rel = idx[:, None] - idx[None, :]
    decay = jnp.where(rel >= 0, jnp.exp(jnp.maximum(rel, 0.0)[None] * lg[:, None, None]), 0.0)
    zeta = jnp.exp((c - 1 - idx)[None, :] * lg[:, None])
    xi = jnp.exp((idx + 1.0)[None, :] * lg[:, None])
    cdec = jnp.exp(c * lg)
    full = (RET_HEADS, c, RET_DIM)
    return jnp.stack([decay,
                      jnp.broadcast_to(zeta[:, :, None], full),
                      jnp.broadcast_to(xi[:, :, None], full),
                      jnp.broadcast_to(cdec[:, None, None], full)])


def _swapped_rope_cols(w):
    a, b = w[..., :HALF_ROPE], w[..., HALF_ROPE:]
    return jnp.concatenate([a, b, b, a], axis=-1)


def _layer_params(l, p, win):
    row = lambda g: g[l].reshape(1, -1)
    kr_cols = jnp.concatenate(
        [jnp.zeros((D_MODEL, MLA_NOPE), BF16), _swapped_rope_cols(win[:, IN_BASE:])], axis=-1)

    dq = MLA_NOPE + MLA_ROPE
    wuq = p["w_uq"][l].reshape(Q_LORA, MLA_HEADS, dq)
    wuq = jnp.concatenate([wuq[..., :MLA_NOPE], _swapped_rope_cols(wuq[..., MLA_NOPE:])], axis=-1)
    wuq = wuq.reshape(Q_LORA, MLA_HEADS * LANES).astype(BF16)
    wukv = p["w_ukv"][l].reshape(KV_LORA, MLA_HEADS, MLA_NOPE + MLA_V)
    wk = jnp.concatenate([wukv[..., :MLA_NOPE], jnp.zeros_like(wukv[..., :MLA_NOPE])], axis=-1)
    wk = wk.reshape(KV_LORA, MLA_HEADS * LANES).astype(BF16)
    wv = jnp.concatenate([wukv[..., MLA_NOPE:], jnp.zeros_like(wukv[..., :LANES - MLA_V])], axis=-1)
    wv = wv.reshape(KV_LORA, MLA_HEADS * LANES).astype(BF16)

    scale = dq ** -0.5 * LOG2_E
    zeros64 = jnp.zeros((MLA_NOPE,), F32)
    gq = jnp.concatenate([p["qn_nope"][l], _swapped_rope_cols(p["qn_rope"][l])]) * scale
    gk = jnp.concatenate([p["kn_nope"][l], zeros64])
    gkr = jnp.concatenate([zeros64, _swapped_rope_cols(p["kn_rope"][l])])
    one_lane = jnp.zeros((LANES,), F32).at[MLA_V].set(1.0)
    vecs = jnp.stack([jnp.tile(r, MLA_HEADS) for r in (gq, gk, gkr, one_lane)]
                     + [jnp.zeros((MLA_HEADS * LANES,), F32)] * 4)
    return dict(
        ffn1_gain=row(p["ffn1_norm"]),
        ffn2_gain=row(p["ffn2_norm"]),
        inproj=(row(p["mix_norm"]), kr_cols, row(p["q_lat_norm"]), wuq, row(p["kv_lat_norm"]),
                wk, wv),
        vecs=vecs,
        gn=p["ret_head_norm"][l].reshape(RET_HEADS, 1, RET_DIM),
    )


def _group_matrix():
    idx = jnp.arange(2 * LANES)
    slab, lane = idx // LANES, idx % LANES
    same = slab[:, None] == slab[None, :]
    nope = (lane[:, None] < MLA_NOPE) & (lane[None, :] < MLA_NOPE)
    rope = ((lane[:, None] >= MLA_NOPE) & (lane[:, None] < MLA_NOPE + MLA_ROPE)
            & (lane[None, :] >= MLA_NOPE))
    g = jnp.where(nope, 1.0 / MLA_NOPE, jnp.where(rope, 1.0 / MLA_ROPE, 0.0))
    return jnp.where(same, g, 0.0).astype(BF16)


def kernel(x, positions, ffn1_norm, ffn1_w_gate, ffn1_w_up, ffn1_w_down, mix_norm, w_in, ret_head_norm, q_lat_norm, w_uq, kv_lat_norm, w_ukv, qn_nope, qn_rope, kn_nope, kn_rope, w_o, ffn2_norm, ffn2_w_gate, ffn2_w_up, ffn2_w_down):
    p = dict(ffn1_norm=ffn1_norm, ffn1_w_gate=ffn1_w_gate, ffn1_w_up=ffn1_w_up,
             ffn1_w_down=ffn1_w_down, mix_norm=mix_norm, w_in=w_in,
             ret_head_norm=ret_head_norm, q_lat_norm=q_lat_norm, w_uq=w_uq,
             kv_lat_norm=kv_lat_norm, w_ukv=w_ukv, qn_nope=qn_nope, qn_rope=qn_rope,
             kn_nope=kn_nope, kn_rope=kn_rope, w_o=w_o, ffn2_norm=ffn2_norm,
             ffn2_w_gate=ffn2_w_gate, ffn2_w_up=ffn2_w_up, ffn2_w_down=ffn2_w_down)
    batch, seq, d = x.shape
    depth = w_in.shape[0]
    assert d == D_MODEL and positions.shape == (batch, seq)
    assert seq % ATT_TQ == 0 and seq % RET_TILE == 0
    assert (batch * seq) % TOK_TILE == 0 and (batch * seq) % FFN_TILE == 0
    assert w_in.shape[1:] == (D_MODEL, IN_BASE + MLA_ROPE) and ffn1_w_gate.shape[1:] == (d, D_FF)
    dec = _retention_decay()
    gmat = _group_matrix()
    ffn1_stacks = [ffn1_w_gate, ffn1_w_up, ffn1_w_down]
    ffn2_stacks = [ffn2_w_gate, ffn2_w_up, ffn2_w_down]
    tabs, w_ffn1 = _prologue(positions, ffn1_stacks)
    xf = x.reshape(batch * seq, d)
    for l in range(depth):
        win = w_in[l].astype(BF16)
        lp = _layer_params(l, p, win)
        xf, side_w = _ffn(xf, lp["ffn1_gain"], *w_ffn1, side=(ffn2_stacks + [w_o], l))
        w_ffn2, wo = side_w[:3], side_w[3]
        gmix, kr_cols, gq, wuq, gkv, wk, wv = lp["inproj"]
        rq, rk, rv, rg, qc, kc, v = _inproj(xf, tabs, gmix, win, kr_cols, gq, wuq, gkv,
                                            wk, wv, gmat, lp["vecs"])
        y_ret = _retention(rq, rk, rv, rg, dec, lp["gn"], batch, seq)
        y_mla = _attention(qc, kc, v, batch, seq)
        xf, w_ffn1 = _ffn(
            xf, lp["ffn2_gain"], *w_ffn2,
            proj=(y_ret.reshape(batch * seq, -1), y_mla.reshape(batch * seq, -1), wo),
            side=(ffn1_stacks, l + 1) if l + 1 < depth else None)
    return xf.reshape(batch, seq, d)
```
